```python
import jax, jax.numpy as jnp
from jax import lax
import numpy as np

D_MODEL = 2048
BATCH = 2
SEQ = 4096
DEPTH = 4
DEC_BATCH = 8
DEC_SEQ = 1
PAST_LEN = 16384
PAGE_SIZE = 128

NSA_HEAD_DIM = 128
NSA_HEADS = D_MODEL // 256
NSA_KV_HEADS = NSA_HEADS // 4
NSA_GROUP = NSA_HEADS // NSA_KV_HEADS
NSA_WIDTH = NSA_HEADS * NSA_HEAD_DIM
CMP_BLOCK = 32
CMP_STRIDE = 16
CMP_HIDDEN = 128
SEL_BLOCK = 64
SEL_TOPK = 16
WINDOW = 512
Q_BLOCK = 128
FORCE_BONUS = 1.0e4
M_HEADS = D_MODEL // 256
M_HEAD_DIM = 128
M_WIDTH = M_HEADS * M_HEAD_DIM
M_CHUNK = 64
R_WIDTH = D_MODEL // 2
R_BLOCKS = 8
R_BLOCK_DIM = R_WIDTH // R_BLOCKS
CONV_W = 4
LRU_C = 8.0
D_FF = 4 * D_MODEL
DEEPNORM_ALPHA = (2 * DEPTH) ** 0.25
DEEPNORM_BETA = (8 * DEPTH) ** -0.25
LN_EPS = 1e-5
NEG_INF = -1e30
IN_SPLITS = (NSA_WIDTH, 6 * NSA_KV_HEADS * NSA_HEAD_DIM, 3 * NSA_HEADS,
             3 * M_WIDTH, 2 * M_HEADS, M_WIDTH, R_WIDTH, R_WIDTH, 3 * D_MODEL)
N_IN = sum(IN_SPLITS)

kernel_name = 'nsa_mlstm_rglru_deepnorm_step'


def layer_norm(x, g, b):
    xf = x.astype(jnp.float32)
    mu = xf.mean(-1, keepdims=True)
    var = jnp.mean(jnp.square(xf - mu), -1, keepdims=True)
    return ((xf - mu) * lax.rsqrt(var + LN_EPS) * g + b).astype(x.dtype)


def alibi_slopes():
    return jnp.asarray([2.0 ** (-8.0 * (h + 1) / NSA_HEADS) for h in range(NSA_HEADS)], jnp.float32)


def masked_softmax(s, mask, axis):
    p = jax.nn.softmax(jnp.where(mask, s, NEG_INF), axis=axis)
    return jnp.where(mask, p, 0.0)


def project_in(x, w_in):
    B, T = x.shape[:2]
    z = jnp.einsum('btd,dn->btn', x, w_in)
    offs = [int(o) for o in np.cumsum(IN_SPLITS)[:-1]]
    nq, nkv, ng, mqkv, mif, mo, rx, rgt, mg = jnp.split(z, offs, axis=-1)
    mqkv = mqkv.reshape(B, T, 3, M_HEADS, M_HEAD_DIM)
    return (nq.reshape(B, T, NSA_HEADS, NSA_HEAD_DIM),
            nkv.reshape(B, T, 6, NSA_KV_HEADS, NSA_HEAD_DIM),
            ng.reshape(B, T, NSA_HEADS, 3),
            mqkv[:, :, 0], mqkv[:, :, 1], mqkv[:, :, 2],
            mif.reshape(B, T, 2, M_HEADS), mo, rx, rgt, mg)


def compress(rows, pe, w1, b1, w2):
    B, L, G, dh = rows.shape
    n_chunk = L // CMP_STRIDE
    ch = rows[:, :n_chunk * CMP_STRIDE].reshape(B, n_chunk, CMP_STRIDE, G, dh)
    w1b = w1.reshape(CMP_BLOCK, dh, CMP_HIDDEN)
    w1h = w1.reshape(2, CMP_STRIDE, dh, CMP_HIDDEN)
    f32 = jnp.float32
    lead = jnp.einsum('bncgd,cde->bnge', ch, w1h[0], preferred_element_type=f32)
    trail = jnp.einsum('bncgd,cde->bnge', ch, w1h[1], preferred_element_type=f32)
    pe_term = jnp.einsum('cd,cde->e', pe, w1b, preferred_element_type=f32)
    hid = jax.nn.gelu(lead[:, :-1] + trail[:, 1:] + pe_term + b1)
    return jnp.einsum('bnge,ed->bngd', hid, w2, preferred_element_type=f32)


def to_sel_blocks(rows):
    B, L, G, dh = rows.shape
    return rows.reshape(B, L // SEL_BLOCK, SEL_BLOCK, G, dh).transpose(0, 3, 1, 2, 4)


def nsa_attend(q, q_pos, gate, kc, vc, c_end, ks_b, vs_b, kw, vw, w_pos):
    f32 = jnp.float32
    B, Q = q.shape[:2]
    G, R, dh = NSA_KV_HEADS, NSA_GROUP, NSA_HEAD_DIM
    scale = dh ** -0.5
    slopes = alibi_slopes().reshape(G, R)
    qg = q.reshape(B, Q, G, R, dh)
    dist_c = q_pos[:, None] - c_end[None, :]
    s_c = (jnp.einsum('bqgrd,bngd->bqgrn', qg, kc, preferred_element_type=f32) * scale
           - slopes[None, None, :, :, None] * dist_c.astype(f32)[None, :, None, None, :])
    p_c = masked_softmax(s_c, (dist_c >= 0)[None, :, None, None, :], -1)
    o_c = jnp.einsum('bqgrn,bngd->bqgrd', p_c, vc)
    n_c = kc.shape[1]
    n_sel = ks_b.shape[2]
    c_start = jnp.arange(n_c) * CMP_STRIDE
    s_start = jnp.arange(n_sel) * SEL_BLOCK
    overlap = ((c_start[:, None] < s_start[None, :] + SEL_BLOCK)
               & (c_start[:, None] + CMP_BLOCK > s_start[None, :])).astype(f32)
    imp = jnp.einsum('bqgrn,nj->bqgj', p_c, overlap)
    cur = q_pos // SEL_BLOCK
    jj = jnp.arange(n_sel)[None, :]
    forced = (jj == 0) | (jj == cur[:, None]) | (jj == cur[:, None] - 1)
    valid = s_start[None, :] <= q_pos[:, None]
    imp = jnp.where(forced[None, :, None, :], imp + FORCE_BONUS, imp)
    imp = jnp.where(valid[None, :, None, :], imp, -1.0)
    _, idx = lax.top_k(imp, min(SEL_TOPK, n_sel))
    bi = jnp.arange(B)[:, None, None, None]
    gi = jnp.arange(G)[None, None, :, None]
    kb = ks_b[bi, gi, idx]
    vb = vs_b[bi, gi, idx]
    sel_pos = idx[..., None] * SEL_BLOCK + jnp.arange(SEL_BLOCK)
    dist_s = (q_pos[None, :, None, None, None] - sel_pos)[:, :, :, None]
    s_s = (jnp.einsum('bqgrd,bqgkld->bqgrkl', qg, kb, preferred_element_type=f32) * scale
           - slopes[None, None, :, :, None, None] * dist_s.astype(f32))
    p_s = masked_softmax(s_s, dist_s >= 0, (-2, -1))
    o_s = jnp.einsum('bqgrkl,bqgkld->bqgrd', p_s, vb)
    dist_w = q_pos[:, None] - w_pos[None, :]
    mask_w = (dist_w >= 0) & (dist_w < WINDOW) & (w_pos >= 0)[None, :]
    s_w = (jnp.einsum('bqgrd,bwgd->bqgrw', qg, kw, preferred_element_type=f32) * scale
           - slopes[None, None, :, :, None] * dist_w.astype(f32)[None, :, None, None, :])
    p_w = masked_softmax(s_w, mask_w[None, :, None, None, :], -1)
    o_w = jnp.einsum('bqgrw,bwgd->bqgrd', p_w, vw)
    g = jax.nn.sigmoid(gate.astype(f32)).reshape(B, Q, G, R, 3)
    o = g[..., 0:1] * o_c + g[..., 1:2] * o_s + g[..., 2:3] * o_w
    return o.reshape(B, Q, NSA_WIDTH)


def nsa_prompt(q, kv, gate, pe, w1, b1, w2):
    B, T = q.shape[:2]
    kc = compress(kv[:, :, 0], pe[0], w1[0], b1[0], w2[0])
    vc = compress(kv[:, :, 1], pe[1], w1[1], b1[1], w2[1])
    c_end = jnp.arange(kc.shape[1]) * CMP_STRIDE + CMP_BLOCK - 1
    ks_b = to_sel_blocks(kv[:, :, 2])
    vs_b = to_sel_blocks(kv[:, :, 3])
    pad = ((0, 0), (WINDOW, 0), (0, 0), (0, 0))
    kw = jnp.pad(kv[:, :, 4], pad)
    vw = jnp.pad(kv[:, :, 5], pad)

    def block(i):
        t0 = i * Q_BLOCK
        q_pos = t0 + jnp.arange(Q_BLOCK)
        w_pos = t0 - WINDOW + jnp.arange(WINDOW + Q_BLOCK)
        return nsa_attend(lax.dynamic_slice_in_dim(q, t0, Q_BLOCK, axis=1), q_pos,
                          lax.dynamic_slice_in_dim(gate, t0, Q_BLOCK, axis=1),
                          kc, vc, c_end, ks_b, vs_b,
                          lax.dynamic_slice_in_dim(kw, t0, WINDOW + Q_BLOCK, axis=1),
                          lax.dynamic_slice_in_dim(vw, t0, WINDOW + Q_BLOCK, axis=1), w_pos)

    o = lax.map(block, jnp.arange(T // Q_BLOCK))
    return jnp.moveaxis(o, 0, 1).reshape(B, T, NSA_WIDTH)


def nsa_sample(q, kv, gate, pool, page_table, win_buf, pe, w1, b1, w2):
    DB, Tn = q.shape[:2]
    past = pool[page_table]
    past = past.reshape((DB, -1) + past.shape[3:])
    P = past.shape[1]
    rows = jnp.concatenate([past, kv[:, :, :4]], axis=1)
    L = P + Tn
    kc = compress(rows[:, :, 0], pe[0], w1[0], b1[0], w2[0])
    vc = compress(rows[:, :, 1], pe[1], w1[1], b1[1], w2[1])
    c_end = jnp.arange(kc.shape[1]) * CMP_STRIDE + CMP_BLOCK - 1
    n_sel = -(-L // SEL_BLOCK)
    sel = jnp.pad(rows[:, :, 2:4], ((0, 0), (0, n_sel * SEL_BLOCK - L), (0, 0), (0, 0), (0, 0)))
    ks_b = to_sel_blocks(sel[:, :, 0])
    vs_b = to_sel_blocks(sel[:, :, 1])
    win = jnp.concatenate([win_buf, kv[:, :, 4:6]], axis=1)
    Wb = win_buf.shape[1]
    w_pos = P - Wb + jnp.arange(Wb + Tn)
    q_pos = P + jnp.arange(Tn)
    o = nsa_attend(q, q_pos, gate, kc, vc, c_end, ks_b, vs_b, win[:, :, 0], win[:, :, 1], w_pos)
    return o, win[:, Wb + Tn - min(WINDOW, Wb + Tn):]


def mlstm_chunk(carry, inp):
    C, n, m = carry
    q, k, v, ig, lf = inp
    L = q.shape[1]
    b = jnp.cumsum(lf, axis=1)
    causal = jnp.tril(jnp.ones((L, L), bool))
    d_log = b[:, :, None] - b[:, None, :] + ig[:, None, :]
    d_log = jnp.where(causal[None, :, :, None], d_log, NEG_INF)
    inter = b + m[:, None]
    m_t = jnp.maximum(inter, d_log.max(axis=2))
    w_intra = jnp.exp(d_log - m_t[:, :, None])
    w_inter = jnp.exp(inter - m_t)
    s = jnp.einsum('bqhd,bshd->bqsh', q, k) * w_intra
    num = (jnp.einsum('bqsh,bshe->bqhe', s, v)
           + w_inter[..., None] * jnp.einsum('bqhd,bhde->bqhe', q, C))
    den = s.sum(axis=2) + w_inter * jnp.einsum('bqhd,bhd->bqh', q, n)
    h = num / jnp.maximum(jnp.abs(den), jnp.exp(-m_t))[..., None]
    m_new = m_t[:, -1]
    decay = jnp.exp(b[:, -1] + m - m_new)
    w_s = jnp.exp(b[:, -1:] - b + ig - m_new[:, None])
    C_new = decay[..., None, None] * C + jnp.einsum('bsh,bshd,bshe->bhde', w_s, k, v)
    n_new = decay[..., None] * n + jnp.einsum('bsh,bshd->bhd', w_s, k)
    return (C_new, n_new, m_new), h


def mlstm_branch(mq, mk, mv, mif, mo, b_if, norm_g, state, chunk):
    f32 = jnp.float32
    B, T = mq.shape[:2]
    q = mq.astype(f32)
    k = mk.astype(f32) * (M_HEAD_DIM ** -0.5)
    v = mv.astype(f32)
    g = mif.astype(f32) + b_if.astype(f32)
    ig = g[:, :, 0]
    lf = jax.nn.log_sigmoid(g[:, :, 1])
    nch = T // chunk

    def to_chunks(a):
        return jnp.moveaxis(a.reshape((B, nch, chunk) + a.shape[2:]), 1, 0)

    state, h = lax.scan(mlstm_chunk, state, (to_chunks(q), to_chunks(k), to_chunks(v),
                                             to_chunks(ig), to_chunks(lf)))
    h = jnp.moveaxis(h, 0, 1).reshape(B, T, M_HEADS, M_HEAD_DIM)
    mu = h.mean(-1, keepdims=True)
    var = jnp.mean(jnp.square(h - mu), -1, keepdims=True)
    hn = ((h - mu) * lax.rsqrt(var + LN_EPS)).reshape(B, T, M_WIDTH) * norm_g.astype(f32)
    return jax.nn.sigmoid(mo.astype(f32)) * hn, state


def _lin_combine(left, right):
    a_l, b_l = left
    a_r, b_r = right
    return a_l * a_r, a_r * b_l + b_r


def rglru_branch(rx, rgt, conv_w, conv_b, w_a, b_a, w_x, b_x, lam, h0, conv_buf):
    f32 = jnp.float32
    B, T = rx.shape[:2]
    xp = jnp.concatenate([conv_buf.astype(rx.dtype), rx], axis=1)
    xc = conv_b
    for j in range(CONV_W):
        xc = xc + xp[:, j:j + T] * conv_w[j]
    new_buf = xp[:, T:]
    xf = xc.astype(f32)
    xb = xf.reshape(B, T, R_BLOCKS, R_BLOCK_DIM)
    r = jax.nn.sigmoid(jnp.einsum('btnc,ncd->btnd', xb, w_a).reshape(B, T, R_WIDTH) + b_a)
    i = jax.nn.sigmoid(jnp.einsum('btnc,ncd->btnd', xb, w_x).reshape(B, T, R_WIDTH) + b_x)
    log_a = -LRU_C * r * jax.nn.softplus(-lam.astype(f32))
    a = jnp.exp(log_a)
    u = jnp.sqrt(-jnp.expm1(2.0 * log_a)) * (i * xf)
    u = u.at[:, 0].add(a[:, 0] * h0)
    _, h = lax.associative_scan(_lin_combine, (a, u), axis=1)
    return h * jax.nn.gelu(rgt.astype(f32)), h[:, -1], new_buf


def merge_branches(o_nsa, o_mlstm, o_rglru, gate_logits, w_branch, w_out):
    B, T = o_nsa.shape[:2]
    br = jnp.stack([o_nsa, o_mlstm, o_rglru], axis=2)
    up = jnp.einsum('btkc,kcd->btkd', br, w_branch)
    g = jax.nn.sigmoid(gate_logits.astype(jnp.float32)).reshape(B, T, 3, D_MODEL)
    return jnp.einsum('btd,de->bte', (g * up).sum(2), w_out)


def post_block(x, mix, g, b, w1, w2):
    h = layer_norm(DEEPNORM_ALPHA * x + mix.astype(x.dtype), g[0], b[0])
    f = jnp.einsum('btf,fd->btd', jnp.square(jax.nn.relu(jnp.einsum('btd,df->btf', h, w1))), w2)
    return layer_norm(DEEPNORM_ALPHA * h + f.astype(h.dtype), g[1], b[1])


def setup_inputs(seed: int = 0) -> dict:
    key = jax.random.key(seed)
    ks = jax.random.split(key, 32)
    f32 = jnp.float32
    n_pages = PAST_LEN // PAGE_SIZE
    n_pool = (DEC_BATCH * n_pages * 5) // 4
    win_buf = min(WINDOW, PAST_LEN)

    def nrm(k, shape, s):
        return s * jax.random.normal(k, shape, f32)

    u = jax.random.uniform(ks[24], (DEPTH, R_WIDTH), f32, 0.9, 0.999)
    return {
        'x_prompt': nrm(ks[0], (BATCH, SEQ, D_MODEL), 1.0),
        'x_sample': nrm(ks[1], (DEC_BATCH, DEC_SEQ, D_MODEL), 1.0),
        'cache_nsa_kv': nrm(ks[2], (DEPTH, n_pool, PAGE_SIZE, 4, NSA_KV_HEADS, NSA_HEAD_DIM), 1.0),
        'cache_win_kv': nrm(ks[3], (DEPTH, DEC_BATCH, win_buf, 2, NSA_KV_HEADS, NSA_HEAD_DIM), 1.0),
        'state_mlstm_C': nrm(ks[4], (DEPTH, DEC_BATCH, M_HEADS, M_HEAD_DIM, M_HEAD_DIM), 0.3),
        'state_mlstm_n': nrm(ks[5], (DEPTH, DEC_BATCH, M_HEADS, M_HEAD_DIM), 0.3),
        'state_mlstm_m': nrm(ks[6], (DEPTH, DEC_BATCH, M_HEADS), 1.0),
        'state_rglru_h': nrm(ks[7], (DEPTH, DEC_BATCH, R_WIDTH), 0.5),
        'state_rglru_conv': nrm(ks[8], (DEPTH, DEC_BATCH, CONV_W - 1, R_WIDTH), 1.0),
        'page_table': jax.random.permutation(ks[9], n_pool)[:DEC_BATCH * n_pages]
                      .reshape(DEC_BATCH, n_pages).astype(jnp.int32),
        'w_in': nrm(ks[10], (DEPTH, D_MODEL, N_IN), D_MODEL ** -0.5),
        'nsa_pe': nrm(ks[11], (DEPTH, 2, CMP_BLOCK, NSA_HEAD_DIM), 0.1),
        'nsa_phi_w1': nrm(ks[12], (DEPTH, 2, CMP_BLOCK * NSA_HEAD_DIM, CMP_HIDDEN),
                          (CMP_BLOCK * NSA_HEAD_DIM) ** -0.5),
        'nsa_phi_b1': nrm(ks[13], (DEPTH, 2, CMP_HIDDEN), 0.02),
        'nsa_phi_w2': nrm(ks[14], (DEPTH, 2, CMP_HIDDEN, NSA_HEAD_DIM), CMP_HIDDEN ** -0.5),
        'mlstm_b_if': jnp.stack([nrm(ks[15], (DEPTH, M_HEADS), 0.1),
                                 jnp.linspace(3.0, 6.0, M_HEADS)[None, :]
                                 + nrm(ks[16], (DEPTH, M_HEADS), 0.1)], axis=1),
        'mlstm_norm_g': 1.0 + nrm(ks[17], (DEPTH, M_WIDTH), 0.02),
        'rg_conv_w': nrm(ks[18], (DEPTH, CONV_W, R_WIDTH), CONV_W ** -0.5),
        'rg_conv_b': nrm(ks[19], (DEPTH, R_WIDTH), 0.02),
        'rg_w_a': nrm(ks[20], (DEPTH, R_BLOCKS, R_BLOCK_DIM, R_BLOCK_DIM), R_BLOCK_DIM ** -0.5),
        'rg_b_a': nrm(ks[21], (DEPTH, R_WIDTH), 0.02),
        'rg_w_x': nrm(ks[22], (DEPTH, R_BLOCKS, R_BLOCK_DIM, R_BLOCK_DIM), R_BLOCK_DIM ** -0.5),
        'rg_b_x': nrm(ks[23], (DEPTH, R_WIDTH), 0.02),
        'rg_lambda': jnp.log(u) - jnp.log1p(-u),
        'w_branch': nrm(ks[25], (DEPTH, 3, NSA_WIDTH, D_MODEL), DEEPNORM_BETA * NSA_WIDTH ** -0.5),
        'w_out': nrm(ks[26], (DEPTH, D_MODEL, D_MODEL), DEEPNORM_BETA * D_MODEL ** -0.5),
        'ln_g': 1.0 + nrm(ks[27], (DEPTH, 2, D_MODEL), 0.02),
        'ln_b': nrm(ks[28], (DEPTH, 2, D_MODEL), 0.02),
        'mlp_w1': nrm(ks[29], (DEPTH, D_MODEL, D_FF), DEEPNORM_BETA * D_MODEL ** -0.5),
        'mlp_w2': nrm(ks[30], (DEPTH, D_FF, D_MODEL), DEEPNORM_BETA * D_FF ** -0.5),
    }


def reference(x_prompt, x_sample, cache_nsa_kv, cache_win_kv, state_mlstm_C, state_mlstm_n,
              state_mlstm_m, state_rglru_h, state_rglru_conv, page_table,
              w_in, nsa_pe, nsa_phi_w1, nsa_phi_b1, nsa_phi_w2, mlstm_b_if, mlstm_norm_g,
              rg_conv_w, rg_conv_b, rg_w_a, rg_b_a, rg_w_x, rg_b_x, rg_lambda,
              w_branch, w_out, ln_g, ln_b, mlp_w1, mlp_w2):
    f32 = jnp.float32
    B, T = x_prompt.shape[:2]
    DB, Tn = x_sample.shape[:2]
    sdt = state_mlstm_C.dtype
    chunk_p = M_CHUNK if T % M_CHUNK == 0 else T
    n_win_p = min(WINDOW, T)
    xp, xs = x_prompt, x_sample
    new_p = [[] for _ in range(7)]
    new_s = [[] for _ in range(7)]
    for l in range(DEPTH):
        phi = (nsa_pe[l], nsa_phi_w1[l], nsa_phi_b1[l], nsa_phi_w2[l])
        rgp = (rg_conv_w[l], rg_conv_b[l], rg_w_a[l], rg_b_a[l], rg_w_x[l], rg_b_x[l], rg_lambda[l])
        q, kv, ng, mq, mk, mv, mif, mo, rx, rgt, mg = project_in(xp, w_in[l])
        o_a = nsa_prompt(q, kv, ng, *phi)
        zero_state = (jnp.zeros((B, M_HEADS, M_HEAD_DIM, M_HEAD_DIM), f32),
                      jnp.zeros((B, M_HEADS, M_HEAD_DIM), f32),
                      jnp.zeros((B, M_HEADS), f32))
        o_b, (Cp, np_, mp) = mlstm_branch(mq, mk, mv, mif, mo, mlstm_b_if[l], mlstm_norm_g[l],
                                          zero_state, chunk_p)
        o_c, hp, cbp = rglru_branch(rx, rgt, *rgp, jnp.zeros((B, R_WIDTH), f32),
                                    jnp.zeros((B, CONV_W - 1, R_WIDTH), rx.dtype))
        mix = merge_branches(o_a, o_b, o_c, mg, w_branch[l], w_out[l])
        xp = post_block(xp, mix, ln_g[l], ln_b[l], mlp_w1[l], mlp_w2[l])
        for lst, val in zip(new_p, (kv[:, :, :4], kv[:, T - n_win_p:, 4:6], Cp.astype(sdt),
                                    np_.astype(sdt), mp.astype(sdt), hp.astype(sdt),
                                    cbp.astype(sdt))):
            lst.append(val)
        q, kv, ng, mq, mk, mv, mif, mo, rx, rgt, mg = project_in(xs, w_in[l])
        o_a, win_new = nsa_sample(q, kv, ng, cache_nsa_kv[l], page_table, cache_win_kv[l], *phi)
        st = (state_mlstm_C[l].astype(f32), state_mlstm_n[l].astype(f32), state_mlstm_m[l].astype(f32))
        o_b, (Cs, ns_, ms) = mlstm_branch(mq, mk, mv, mif, mo, mlstm_b_if[l], mlstm_norm_g[l], st, Tn)
        o_c, hs, cbs = rglru_branch(rx, rgt, *rgp, state_rglru_h[l].astype(f32), state_rglru_conv[l])
        mix = merge_branches(o_a, o_b, o_c, mg, w_branch[l], w_out[l])
        xs = post_block(xs, mix, ln_g[l], ln_b[l], mlp_w1[l], mlp_w2[l])
        for lst, val in zip(new_s, (kv[:, :, :4], win_new, Cs.astype(sdt), ns_.astype(sdt),
                                    ms.astype(sdt), hs.astype(sdt), cbs.astype(sdt))):
            lst.append(val)
    P = [jnp.stack(a) for a in new_p]
    S = [jnp.stack(a) for a in new_s]
    return (xp, xs, P[0], S[0], P[1], S[1], P[2], S[2], P[3], S[3], P[4], S[4], P[5], S[5], P[6], S[6])
```

```python
import functools

import jax
import jax.numpy as jnp
from jax import lax
from jax.experimental import pallas as pl
from jax.experimental.pallas import tpu as pltpu

F32 = jnp.float32
BF16 = jnp.bfloat16
I32 = jnp.int32

D_MODEL = 2048
DEPTH = 4
PAGE_SIZE = 128
HD = 128
NSA_HEADS = 8
NSA_KV = 2
NSA_GROUP = 4
NSA_WIDTH = NSA_HEADS * HD
CMP_BLOCK = 32
CMP_STRIDE = 16
SEL_BLOCK = 64
SEL_TOPK = 16
WINDOW = 512
Q_BLOCK = 128
FORCE_BONUS = 1.0e4
M_HEADS = 8
M_WIDTH = M_HEADS * HD
R_WIDTH = 1024
R_BLOCKS = 8
CONV_W = 4
LRU_C = 8.0
D_FF = 4 * D_MODEL
DEEPNORM_ALPHA = (2 * DEPTH) ** 0.25
LN_EPS = 1e-5
NEG_INF = -1e30

OFF_MG = 0
OFF_Q = 6144
OFF_KV = 7168
OFF_MQKV = 8704
OFF_MO = 11776
OFF_RX = 12800
OFF_RGT = 13824
OFF_SM = 14848
N_AL = 14976
SM_I = 24
SM_F = 32

VMEM_LIMIT = 52 * 1024 * 1024


def _cp(*sem):
    return pltpu.CompilerParams(dimension_semantics=sem, vmem_limit_bytes=VMEM_LIMIT)


def _dot(a, b):
    return jnp.dot(a, b, preferred_element_type=F32)


def _dot_nt(a, b):
    return lax.dot_general(a, b, (((1,), (1,)), ((), ())), preferred_element_type=F32)


def _gelu(x):
    return 0.5 * x * (1.0 + jnp.tanh(0.7978845608028654 * (x + 0.044715 * (x * x * x))))


def _sigmoid(x):
    return 1.0 / (1.0 + jnp.exp(-x))


def _softplus(x):
    return jnp.maximum(x, 0.0) + jnp.log1p(jnp.exp(-jnp.abs(x)))


def _mm_body(a_ref, b_ref, o_ref, *, act):
    acc = _dot(a_ref[...].astype(BF16), b_ref[...])
    if act == "relu2":
        acc = jnp.square(jnp.maximum(acc, 0.0))
    o_ref[...] = acc.astype(o_ref.dtype)


def mm(a, b, *, tm, tn, out_dtype, act=None):
    M, K = a.shape
    N = b.shape[1]
    return pl.pallas_call(
        functools.partial(_mm_body, act=act),
        grid=(M // tm, N // tn),
        in_specs=[pl.BlockSpec((tm, K), lambda i, j: (i, 0)),
                  pl.BlockSpec((K, tn), lambda i, j: (0, j))],
        out_specs=pl.BlockSpec((tm, tn), lambda i, j: (i, j)),
        out_shape=jax.ShapeDtypeStruct((M, N), out_dtype),
        compiler_params=_cp("parallel", "arbitrary"),
        name="mm",
    )(a, b)


def _mm_ln_body(a_ref, b_ref, x_ref, g_ref, bb_ref, o_ref, acc_ref, *, nk):
    k = pl.program_id(1)

    @pl.when(k == 0)
    def _():
        acc_ref[...] = jnp.zeros_like(acc_ref)

    acc_ref[...] += _dot(a_ref[...].astype(BF16), b_ref[...])

    @pl.when(k == nk - 1)
    def _():
        y = DEEPNORM_ALPHA * x_ref[...] + acc_ref[...]
        mu = jnp.mean(y, axis=-1, keepdims=True)
        yc = y - mu
        var = jnp.mean(yc * yc, axis=-1, keepdims=True)
        o_ref[...] = yc * lax.rsqrt(var + LN_EPS) * g_ref[...] + bb_ref[...]


def mm_res_ln(a, b, x, g, bb, *, tm, tk):
    M, K = a.shape
    N = b.shape[1]
    nk = K // tk
    return pl.pallas_call(
        functools.partial(_mm_ln_body, nk=nk),
        grid=(M // tm, nk),
        in_specs=[pl.BlockSpec((tm, tk), lambda i, k: (i, k)),
                  pl.BlockSpec((tk, N), lambda i, k: (k, 0)),
                  pl.BlockSpec((tm, N), lambda i, k: (i, 0)),
                  pl.BlockSpec((1, N), lambda i, k: (0, 0)),
                  pl.BlockSpec((1, N), lambda i, k: (0, 0))],
        out_specs=pl.BlockSpec((tm, N), lambda i, k: (i, 0)),
        out_shape=jax.ShapeDtypeStruct((M, N), F32),
        scratch_shapes=[pltpu.VMEM((tm, N), F32)],
        compiler_params=_cp("parallel", "arbitrary"),
        name="mm_res_ln",
    )(a, b, x, g, bb)


def _merge_body(oa_ref, ob_ref, oc_ref, g0_ref, g1_ref, g2_ref, w_ref, o_ref):
    acc = _sigmoid(g0_ref[...]) * _dot(oa_ref[...].astype(BF16), w_ref[0])
    acc += _sigmoid(g1_ref[...]) * _dot(ob_ref[...].astype(BF16), w_ref[1])
    acc += _sigmoid(g2_ref[...]) * _dot(oc_ref[...].astype(BF16), w_ref[2])
    o_ref[...] = acc.astype(o_ref.dtype)


def merge(o_a, o_b, o_c, z, w_branch, *, tm, tn, out_dtype):
    M, W = o_a.shape
    N = w_branch.shape[2]
    assert OFF_MG % tn == 0 and N % tn == 0 and M % tm == 0
    gb = OFF_MG // tn
    nb = N // tn
    br = pl.BlockSpec((tm, W), lambda i, j: (i, 0))

    def gspec(k):
        return pl.BlockSpec((tm, tn), lambda i, j: (i, gb + k * nb + j))

    return pl.pallas_call(
        _merge_body,
        grid=(M // tm, nb),
        in_specs=[br, br, br, gspec(0), gspec(1), gspec(2),
                  pl.BlockSpec((3, W, tn), lambda i, j: (0, 0, j))],
        out_specs=pl.BlockSpec((tm, tn), lambda i, j: (i, j)),
        out_shape=jax.ShapeDtypeStruct((M, N), out_dtype),
        compiler_params=_cp("parallel", "arbitrary"),
        name="merge",
    )(o_a, o_b, o_c, z, z, z, w_branch)


def _compress_body(x_ref, w1_ref, pe_ref, b1_ref, w2_ref, o_ref, cb_ref):
    half = CMP_STRIDE * HD
    x = x_ref[0, 0].astype(BF16)
    w1 = w1_ref[0].astype(BF16)
    lead = _dot(x, w1[:half])
    trail = _dot(x, w1[half:])
    n = x.shape[0]
    pe8 = jnp.broadcast_to(pe_ref[0], (8, 2 * half)).astype(BF16)
    cb = _dot(pe8, w1) + b1_ref[0]
    trail_next = pltpu.roll(trail, n - 1, axis=0)
    hid = _gelu(lead + trail_next + cb[0:1])
    o_ref[0, 0] = _dot(hid.astype(BF16), w2_ref[0].astype(BF16))
    cb_ref[0] = cb


def compress_prompt(x, w1, pe, b1, w2):
    _, S, n, K = x.shape
    return pl.pallas_call(
        _compress_body,
        grid=(2, S),
        in_specs=[pl.BlockSpec((1, 1, n, K), lambda k, s: (k, s, 0, 0)),
                  pl.BlockSpec((1, 2 * K, HD), lambda k, s: (k, 0, 0)),
                  pl.BlockSpec((1, 1, 2 * K), lambda k, s: (k, 0, 0)),
                  pl.BlockSpec((1, 1, HD), lambda k, s: (k, 0, 0)),
                  pl.BlockSpec((1, HD, HD), lambda k, s: (k, 0, 0))],
        out_specs=[pl.BlockSpec((1, 1, n, HD), lambda k, s: (k, s, 0, 0)),
                   pl.BlockSpec((1, 8, HD), lambda k, s: (k, 0, 0))],
        out_shape=[jax.ShapeDtypeStruct((2, S, n, HD), F32),
                   jax.ShapeDtypeStruct((2, 8, HD), F32)],
        compiler_params=_cp("arbitrary", "arbitrary"),
        name="compress_prompt",
    )(x, w1, pe, b1, w2)


def compress_from_z(z3, w1, pe, b1, w2):
    B, T, _ = z3.shape
    G = NSA_KV
    n = T // CMP_STRIDE
    x = z3[:, :, OFF_KV:OFF_KV + 2 * G * HD].reshape(B, n, CMP_STRIDE, 2, G, HD)
    x = x.transpose(3, 0, 4, 1, 2, 5).reshape(2, B * G, n, CMP_STRIDE * HD)
    out, cb = compress_prompt(x, w1, pe.reshape(2, 1, CMP_BLOCK * HD), b1.reshape(2, 1, HD), w2)
    return out[0], out[1], cb


def alibi_slopes():
    return jnp.asarray([2.0 ** (-8.0 * (h + 1) / NSA_HEADS) for h in range(NSA_HEADS)], F32)


SEL_CHUNK = 512


def _online_update(carry, s, mask, v):
    m, l, acc = carry
    s = jnp.where(mask, s, NEG_INF)
    m_new = jnp.maximum(m, jnp.max(s, axis=1, keepdims=True))
    alpha = jnp.exp(m - m_new)
    p = jnp.where(mask, jnp.exp(s - m_new), 0.0)
    l = alpha * l + jnp.sum(p, axis=1, keepdims=True)
    acc = alpha * acc + _dot(p.astype(BF16), v)
    return m_new, l, acc


def _nsa_prompt_body(slopes_ref, q_ref, ks_ref, vs_ref, kw_ref, vw_ref, kc_ref, vc_ref,
                     sm_ref, o_ref, *, T):
    g = pl.program_id(1)
    i = pl.program_id(2)
    QB = Q_BLOCK
    R = NSA_GROUP
    RQ = R * QB
    NCP = T // CMP_STRIDE
    NC = NCP - 1
    NS = T // SEL_BLOCK
    NSP = max(NS, 128)
    t0 = i * QB
    scale = HD ** -0.5

    qb = q_ref[0]
    q4 = jnp.concatenate([qb[:, r * HD:(r + 1) * HD] for r in range(R)], axis=0).astype(BF16)
    row = lax.broadcasted_iota(I32, (RQ, 1), 0)
    qpos = t0 + (row & (QB - 1))
    slope = jnp.concatenate(
        [jnp.full((QB, 1), slopes_ref[g * R + r], F32) for r in range(R)], axis=0)

    kc = kc_ref[0, 0].astype(BF16)
    vc = vc_ref[0, 0].astype(BF16)
    n_idx = lax.broadcasted_iota(I32, (1, NCP), 1)
    dist = qpos - (n_idx * CMP_STRIDE + (CMP_BLOCK - 1))
    mask = (dist >= 0) & (n_idx < NC)
    s = _dot_nt(q4, kc) * scale - slope * dist.astype(F32)
    s = jnp.where(mask, s, NEG_INF)
    m = jnp.max(s, axis=1, keepdims=True)
    p = jnp.where(mask, jnp.exp(s - m), 0.0)
    l = jnp.sum(p, axis=1, keepdims=True)
    p = p / jnp.where(l > 0.0, l, 1.0)
    o_c = _dot(p.astype(BF16), vc)

    psum = p[0:QB]
    for r in range(1, R):
        psum = psum + p[r * QB:(r + 1) * QB]
    p_hi = psum.astype(BF16)
    p_lo = (psum - p_hi.astype(F32)).astype(BF16)
    jo = lax.broadcasted_iota(I32, (NS, NCP), 0) * SEL_BLOCK
    no = lax.broadcasted_iota(I32, (NS, NCP), 1)
    ov = ((no * CMP_STRIDE < jo + SEL_BLOCK) & (no * CMP_STRIDE + CMP_BLOCK > jo)
          & (no < NC)).astype(BF16)
    imp = _dot_nt(ov, p_hi) + _dot_nt(ov, p_lo)
    jj = lax.broadcasted_iota(I32, (NS, QB), 0)
    qp = t0 + lax.broadcasted_iota(I32, (NS, QB), 1)
    cur = qp // SEL_BLOCK
    forced = (jj == 0) | (jj == cur) | (jj == cur - 1)
    imp = jnp.where(forced, imp + FORCE_BONUS, imp)
    imp = jnp.where(jj * SEL_BLOCK <= qp, imp, -1.0)
    rank = jnp.zeros((NS, QB), I32)
    for j2 in range(NS):
        rv = imp[j2:j2 + 1, :]
        beats = (rv > imp) | ((rv == imp) & (jj > j2))
        rank = rank + beats.astype(I32)
    sel_t = (rank < SEL_TOPK).astype(F32)
    if NSP > NS:
        sel_t = jnp.concatenate([sel_t, jnp.zeros((NSP - NS, QB), F32)], axis=0)
    sel = sel_t.T.astype(BF16)

    def sel_chunk(c, carry):
        k0 = pl.multiple_of(c * SEL_CHUNK, SEL_CHUNK)
        kch = ks_ref[0, pl.ds(k0, SEL_CHUNK), :].astype(BF16)
        vch = vs_ref[0, pl.ds(k0, SEL_CHUNK), :].astype(BF16)
        je = lax.broadcasted_iota(I32, (NSP, SEL_CHUNK), 0)
        ke = lax.broadcasted_iota(I32, (NSP, SEL_CHUNK), 1)
        e = (je == c * (SEL_CHUNK // SEL_BLOCK) + ke // SEL_BLOCK).astype(BF16)
        msel = _dot(sel, e)
        msel = jnp.concatenate([msel] * R, axis=0)
        d = qpos - (k0 + lax.broadcasted_iota(I32, (1, SEL_CHUNK), 1))
        sc = _dot_nt(q4, kch) * scale - slope * d.astype(F32)
        return _online_update(carry, sc, (msel > 0.5) & (d >= 0), vch)

    init = (jnp.full((RQ, 1), NEG_INF, F32), jnp.zeros((RQ, 1), F32), jnp.zeros((RQ, HD), F32))
    n_chunks = (t0 + QB + SEL_CHUNK - 1) // SEL_CHUNK
    _, l_s, acc_s = lax.fori_loop(0, n_chunks, sel_chunk, init)
    o_s = acc_s / l_s

    carry = init
    for w in range((WINDOW + QB) // QB):
        kstart = t0 - WINDOW + w * QB
        kcl = pl.multiple_of(jnp.maximum(kstart, 0), QB)
        kch = kw_ref[0, pl.ds(kcl, QB), :].astype(BF16)
        vch = vw_ref[0, pl.ds(kcl, QB), :].astype(BF16)
        kpos = kstart + lax.broadcasted_iota(I32, (1, QB), 1)
        d = qpos - kpos
        sc = _dot_nt(q4, kch) * scale - slope * d.astype(F32)
        carry = _online_update(carry, sc, (d >= 0) & (d < WINDOW) & (kpos >= 0), vch)
    _, l_w, acc_w = carry
    o_w = acc_w / l_w

    smb = sm_ref[0]
    lane = lax.broadcasted_iota(I32, (QB, 128), 1)

    def gate(c):
        cols = []
        for r in range(R):
            idx = (g * R + r) * 3 + c
            cols.append(jnp.sum(jnp.where(lane == idx, smb, 0.0), axis=1, keepdims=True))
        return _sigmoid(jnp.concatenate(cols, axis=0))

    o = gate(0) * o_c + gate(1) * o_s + gate(2) * o_w
    o_ref[0] = jnp.concatenate([o[r * QB:(r + 1) * QB] for r in range(R)], axis=1)


def nsa_prompt(z3, kc, vc, slopes):
    B, T, _ = z3.shape
    G = NSA_KV
    kvb = OFF_KV // HD

    def kvspec(kind):
        return pl.BlockSpec((1, T, HD), lambda b, g, i: (b, 0, kvb + kind * G + g))

    cspec = pl.BlockSpec((1, 1, T // CMP_STRIDE, HD), lambda b, g, i: (b * G + g, 0, 0, 0))
    GW = NSA_GROUP * HD
    assert OFF_Q % GW == 0
    qspec = pl.BlockSpec((1, Q_BLOCK, GW), lambda b, g, i: (b, i, OFF_Q // GW + g))
    return pl.pallas_call(
        functools.partial(_nsa_prompt_body, T=T),
        grid=(B, G, T // Q_BLOCK),
        in_specs=[pl.BlockSpec(memory_space=pltpu.SMEM),
                  qspec, kvspec(2), kvspec(3), kvspec(4), kvspec(5), cspec, cspec,
                  pl.BlockSpec((1, Q_BLOCK, 128), lambda b, g, i: (b, i, OFF_SM // 128))],
        out_specs=pl.BlockSpec((1, Q_BLOCK, GW), lambda b, g, i: (b, i, g)),
        out_shape=jax.ShapeDtypeStruct((B, T, NSA_WIDTH), F32),
        compiler_params=_cp("parallel", "parallel", "arbitrary"),
        name="nsa_prompt",
    )(slopes, z3, z3, z3, z3, z3, kc[:, None], vc[:, None], z3)


M_CHUNK_K = 128


def _mlstm_body(bif_ref, q_ref, k_ref, v_ref, sm_ref, mo_ref, ng_ref,
                o_ref, c_ref, n_ref, m_ref, *, T):
    h = pl.program_id(1)
    L = M_CHUNK_K
    bi = bif_ref[0, h]
    bf = bif_ref[1, h]
    lane = lax.broadcasted_iota(I32, (L, 128), 1)
    ri = lax.broadcasted_iota(I32, (L, L), 0)
    ci = lax.broadcasted_iota(I32, (L, L), 1)
    eye = ri == ci
    tril = ri >= ci
    ng = ng_ref[...]

    def chunk(c, carry):
        C, n, m = carry
        r0 = pl.multiple_of(c * L, L)
        q = q_ref[0, pl.ds(r0, L), :]
        k = k_ref[0, pl.ds(r0, L), :] * (HD ** -0.5)
        v = v_ref[0, pl.ds(r0, L), :]
        smb = sm_ref[0, pl.ds(r0, L), :]
        ig_col = jnp.sum(jnp.where(lane == SM_I + h, smb, 0.0), axis=1, keepdims=True) + bi
        fp_col = jnp.sum(jnp.where(lane == SM_F + h, smb, 0.0), axis=1, keepdims=True) + bf
        lf_col = -_softplus(-fp_col)
        lf_row = jnp.sum(jnp.where(eye, lf_col, 0.0), axis=0, keepdims=True)
        ig_row = jnp.sum(jnp.where(eye, ig_col, 0.0), axis=0, keepdims=True)
        b_col = jnp.sum(jnp.where(tril, lf_row, 0.0), axis=1, keepdims=True)
        b_row = jnp.sum(jnp.where(ri <= ci, lf_col, 0.0), axis=0, keepdims=True)
        d = jnp.where(tril, b_col - b_row + ig_row, NEG_INF)
        inter = b_col + m
        m_t = jnp.maximum(inter, jnp.max(d, axis=1, keepdims=True))
        w_intra = jnp.exp(d - m_t)
        w_inter = jnp.exp(inter - m_t)
        qb = q.astype(BF16)
        vb = v.astype(BF16)
        s = _dot_nt(qb, k.astype(BF16)) * w_intra
        num = _dot(s.astype(BF16), vb) + w_inter * _dot(qb, C.astype(BF16))
        den = jnp.sum(s, axis=1, keepdims=True) + w_inter * jnp.sum(q * n, axis=1, keepdims=True)
        hh = num / jnp.maximum(jnp.abs(den), jnp.exp(-m_t))
        m_new = m_t[L - 1:L, :]
        b_last = b_col[L - 1:L, :]
        decay = jnp.exp(b_last + m - m_new)
        w_s = jnp.exp(b_last - b_col + ig_col - m_new)
        kw = k * w_s
        C_new = decay * C + _dot(kw.T.astype(BF16), vb)
        n_new = decay * n + jnp.sum(kw, axis=0, keepdims=True)
        mu = jnp.mean(hh, axis=1, keepdims=True)
        hc = hh - mu
        var = jnp.mean(hc * hc, axis=1, keepdims=True)
        hn = hc * lax.rsqrt(var + LN_EPS) * ng
        o_ref[0, pl.ds(r0, L), :] = _sigmoid(mo_ref[0, pl.ds(r0, L), :]) * hn
        return C_new, n_new, m_new

    init = (jnp.zeros((HD, HD), F32), jnp.zeros((1, HD), F32), jnp.zeros((1, 1), F32))
    C, n, m = lax.fori_loop(0, T // L, chunk, init)
    c_ref[0, 0] = C
    n_ref[0, 0] = n
    m_ref[0, 0] = jnp.broadcast_to(m, (1, 128))


def mlstm_prompt(z3, b_if, norm_g):
    B, T, _ = z3.shape
    H = M_HEADS
    qb = OFF_MQKV // HD

    def hspec(off):
        return pl.BlockSpec((1, T, HD), lambda b, h: (b, 0, off + h))

    return pl.pallas_call(
        functools.partial(_mlstm_body, T=T),
        grid=(B, H),
        in_specs=[pl.BlockSpec(memory_space=pltpu.SMEM),
                  hspec(qb), hspec(qb + H), hspec(qb + 2 * H),
                  pl.BlockSpec((1, T, 128), lambda b, h: (b, 0, OFF_SM // 128)),
                  hspec(OFF_MO // HD),
                  pl.BlockSpec((1, HD), lambda b, h: (0, h))],
        out_specs=[pl.BlockSpec((1, T, HD), lambda b, h: (b, 0, h)),
                   pl.BlockSpec((1, 1, HD, HD), lambda b, h: (b, h, 0, 0)),
                   pl.BlockSpec((1, 1, 1, HD), lambda b, h: (b, h, 0, 0)),
                   pl.BlockSpec((1, 1, 1, 128), lambda b, h: (b, h, 0, 0))],
        out_shape=[jax.ShapeDtypeStruct((B, T, M_WIDTH), F32),
                   jax.ShapeDtypeStruct((B, H, HD, HD), F32),
                   jax.ShapeDtypeStruct((B, H, 1, HD), F32),
                   jax.ShapeDtypeStruct((B, H, 1, 128), F32)],
        compiler_params=_cp("parallel", "parallel"),
        name="mlstm_prompt",
    )(b_if, z3, z3, z3, z3, z3, norm_g.reshape(1, M_WIDTH))


R_CHUNK = 128


def _rglru_gates(xc, wa, wx, vec):
    xb = xc.astype(BF16)
    r = _sigmoid(_dot(xb, wa) + vec[1:2])
    i = _sigmoid(_dot(xb, wx) + vec[2:3])
    log_a = -LRU_C * r * _softplus(-vec[3:4])
    a = jnp.exp(log_a)
    th = jnp.tanh(log_a)
    u = jnp.sqrt(-2.0 * th / (1.0 - th)) * (i * xc)
    return a, u


def _rglru_body(rx_ref, rgt_ref, cw_ref, vec_ref, wa_ref, wx_ref, o_ref, hl_ref, xpad_ref, *, T):
    Tc = R_CHUNK
    xpad_ref[0:8, :] = jnp.zeros((8, 128), F32)
    xpad_ref[8:T + 8, :] = rx_ref[0]
    cw = cw_ref[...]
    vec = vec_ref[...]
    wa = wa_ref[0].astype(BF16)
    wx = wx_ref[0].astype(BF16)
    rowmod = lax.broadcasted_iota(I32, (Tc, 128), 0) & 7

    def chunk(c, h):
        r0 = pl.multiple_of(c * Tc, Tc)
        xc = vec[0:1]
        for j in range(CONV_W):
            xc = xc + xpad_ref[pl.ds(r0 + (8 - (CONV_W - 1)) + j, Tc), :] * cw[j:j + 1]
        a, u = _rglru_gates(xc, wa, wx, vec)
        for sft in (1, 2, 4):
            a1 = pltpu.roll(a, sft, axis=0)
            u1 = pltpu.roll(u, sft, axis=0)
            ok = rowmod >= sft
            u = jnp.where(ok, a * u1 + u, u)
            a = jnp.where(ok, a * a1, a)
        hs = []
        for gi in range(Tc // 8):
            hg = a[gi * 8:(gi + 1) * 8] * h + u[gi * 8:(gi + 1) * 8]
            hs.append(hg)
            h = hg[7:8]
        hf = jnp.concatenate(hs, axis=0)
        o_ref[0, pl.ds(r0, Tc), :] = hf * _gelu(rgt_ref[0, pl.ds(r0, Tc), :])
        return h

    h = lax.fori_loop(0, T // Tc, chunk, jnp.zeros((1, 128), F32))
    hl_ref[0] = h


def rglru_prompt(z3, conv_w, vec, w_a, w_x):
    B, T, _ = z3.shape
    NB = R_BLOCKS
    return pl.pallas_call(
        functools.partial(_rglru_body, T=T),
        grid=(B, NB),
        in_specs=[pl.BlockSpec((1, T, 128), lambda b, n: (b, 0, OFF_RX // 128 + n)),
                  pl.BlockSpec((1, T, 128), lambda b, n: (b, 0, OFF_RGT // 128 + n)),
                  pl.BlockSpec((CONV_W, 128), lambda b, n: (0, n)),
                  pl.BlockSpec((4, 128), lambda b, n: (0, n)),
                  pl.BlockSpec((1, 128, 128), lambda b, n: (n, 0, 0)),
                  pl.BlockSpec((1, 128, 128), lambda b, n: (n, 0, 0))],
        out_specs=[pl.BlockSpec((1, T, 128), lambda b, n: (b, 0, n)),
                   pl.BlockSpec((1, 1, 128), lambda b, n: (b, 0, n))],
        out_shape=[jax.ShapeDtypeStruct((B, T, R_WIDTH), F32),
                   jax.ShapeDtypeStruct((B, 1, R_WIDTH), F32)],
        scratch_shapes=[pltpu.VMEM((T + 8, 128), F32)],
        compiler_params=_cp("parallel", "parallel"),
        name="rglru_prompt",
    )(z3, z3, conv_w, vec, w_a, w_x)


ROWS_PER_PAGE = PAGE_SIZE * 4 * NSA_KV
CHUNKS_PER_PAGE = PAGE_SIZE // CMP_STRIDE


def _sample_cmp_body(pt_ref, slopes_ref, page_ref, w1c_ref, cb_ref, w2_ref, q_ref,
                     oc_ref, idx_ref, xs_ref, lt_ref, *, NP, GRP):
    p = pl.program_id(1)
    P = NP * PAGE_SIZE
    CPG = GRP * CHUNKS_PER_PAGE
    NCH = NP * CHUNKS_PER_PAGE
    n_sel = P // SEL_BLOCK + 1
    NSP = -(-n_sel // 128) * 128
    scale = HD ** -0.5
    pp = p % GRP

    for kg in range(2 * NSA_KV):
        for c in range(CMP_STRIDE):
            v = page_ref[0, pl.ds(c * 8 + kg, CHUNKS_PER_PAGE, stride=CMP_STRIDE * 8), :]
            xs_ref[kg, pl.ds(pl.multiple_of(pp * CHUNKS_PER_PAGE, 8), CHUNKS_PER_PAGE),
                   c * HD:(c + 1) * HD] = v

    @pl.when(pp == GRP - 1)
    def _():
        r0 = pl.multiple_of((p // GRP) * CPG, CPG)
        for kind in range(2):
            for g in range(NSA_KV):
                x = xs_ref[kind * NSA_KV + g].astype(BF16)
                lt_ref[kind * NSA_KV + g, pl.ds(r0, CPG), :] = _dot(x, w1c_ref[kind])

    @pl.when(p == NP - 1)
    def _():
        q128 = jnp.concatenate([q_ref[0], jnp.zeros((128 - NSA_HEADS, HD), F32)], axis=0).astype(BF16)
        lane = lax.broadcasted_iota(I32, (1, 128), 1)
        slope_row = jnp.zeros((1, 128), F32)
        for h in range(NSA_HEADS):
            slope_row = jnp.where(lane == h, slopes_ref[h], slope_row)
        n_col = lax.broadcasted_iota(I32, (NCH, 1), 0)
        dist = P - (n_col * CMP_STRIDE + (CMP_BLOCK - 1))
        maskc = (dist >= 0) & (n_col < NCH - 1)
        dist_f = dist.astype(F32)

        def cmp_rows(kind, g):
            lead = lt_ref[kind * NSA_KV + g, :, 0:HD]
            trail = lt_ref[kind * NSA_KV + g, :, HD:2 * HD]
            hid = _gelu(lead + pltpu.roll(trail, NCH - 1, axis=0) + cb_ref[kind, 0:1])
            return _dot(hid.astype(BF16), w2_ref[kind].astype(BF16)).astype(BF16)

        oc = jnp.zeros((128, HD), F32)
        psum2 = jnp.zeros((NCH, 128), F32)
        for g in range(NSA_KV):
            kc = cmp_rows(0, g)
            vc = cmp_rows(1, g)
            s = _dot_nt(kc, q128) * scale - slope_row * dist_f
            s = jnp.where(maskc, s, NEG_INF)
            m = jnp.max(s, axis=0, keepdims=True)
            pt = jnp.where(maskc, jnp.exp(s - m), 0.0)
            l = jnp.sum(pt, axis=0, keepdims=True)
            pt = pt / jnp.where(l > 0.0, l, 1.0)
            in_g = (lane >= g * NSA_GROUP) & (lane < (g + 1) * NSA_GROUP)
            pg = jnp.where(in_g, pt, 0.0)
            oc = oc + _dot(pg.T.astype(BF16), vc)
            psum2 = psum2 + jnp.where(lane == g, jnp.sum(pg, axis=1, keepdims=True), 0.0)
        oc_ref[0] = oc[0:NSA_HEADS]

        p_hi = psum2.astype(BF16)
        p_lo = (psum2 - p_hi.astype(F32)).astype(BF16)
        jo = lax.broadcasted_iota(I32, (NSP, NCH), 0) * SEL_BLOCK
        no = lax.broadcasted_iota(I32, (NSP, NCH), 1)
        ov = ((no * CMP_STRIDE < jo + SEL_BLOCK) & (no * CMP_STRIDE + CMP_BLOCK > jo)
              & (no < NCH - 1)).astype(BF16)
        imp = _dot(ov, p_hi) + _dot(ov, p_lo)
        jcol = lax.broadcasted_iota(I32, (NSP, 1), 0)
        cur = P // SEL_BLOCK
        forced = (jcol == 0) | (jcol == cur) | (jcol == cur - 1)
        imp = jnp.where(forced, imp + FORCE_BONUS, imp)
        imp = jnp.where(jcol * SEL_BLOCK <= P, imp, -1.0)
        imp = jnp.where(jcol < n_sel, imp, -2.0)
        ri = lax.broadcasted_iota(I32, (NSP, NSP), 0)
        ci = lax.broadcasted_iota(I32, (NSP, NSP), 1)
        lane_n = lax.broadcasted_iota(I32, (NSP, 128), 1)
        jcol_f = jcol.astype(F32)
        idx_ref[0] = jnp.zeros((8, 128), I32)
        for g in range(NSA_KV):
            col = jnp.sum(jnp.where(lane_n == g, imp, 0.0), axis=1, keepdims=True)
            rowv = jnp.sum(jnp.where(ri == ci, col, 0.0), axis=0, keepdims=True)
            beats = (rowv > col) | ((rowv == col) & (ci < ri))
            rank = jnp.sum(beats.astype(F32), axis=1, keepdims=True)
            hit = rank == lane_n.astype(F32)
            idx_ref[0, g:g + 1, :] = jnp.sum(jnp.where(hit, jcol_f, 0.0), axis=0,
                                             keepdims=True).astype(I32)


def sample_cmp_select(page_table, pool, layer, n_pool, w1c, cb, w2, q8, slopes):
    DB, NP = page_table.shape
    GRP = min(32, NP)
    NCH = NP * CHUNKS_PER_PAGE
    K2 = CMP_STRIDE * HD
    base = layer * n_pool
    grid_spec = pltpu.PrefetchScalarGridSpec(
        num_scalar_prefetch=1,
        grid=(DB, NP),
        in_specs=[pl.BlockSpec(memory_space=pltpu.SMEM),
                  pl.BlockSpec((1, ROWS_PER_PAGE, HD), lambda b, p, pt: (base + pt[b * NP + p], 0, 0)),
                  pl.BlockSpec((2, K2, 2 * HD), lambda b, p, pt: (0, 0, 0)),
                  pl.BlockSpec((2, 8, HD), lambda b, p, pt: (0, 0, 0)),
                  pl.BlockSpec((2, HD, HD), lambda b, p, pt: (0, 0, 0)),
                  pl.BlockSpec((1, NSA_HEADS, HD), lambda b, p, pt: (b, 0, 0))],
        out_specs=[pl.BlockSpec((1, NSA_HEADS, HD), lambda b, p, pt: (b, 0, 0)),
                   pl.BlockSpec((1, 8, 128), lambda b, p, pt: (b, 0, 0))],
        scratch_shapes=[pltpu.VMEM((2 * NSA_KV, GRP * CHUNKS_PER_PAGE, K2), F32),
                        pltpu.VMEM((2 * NSA_KV, NCH, 2 * HD), F32)],
    )
    return pl.pallas_call(
        functools.partial(_sample_cmp_body, NP=NP, GRP=GRP),
        grid_spec=grid_spec,
        out_shape=[jax.ShapeDtypeStruct((DB, NSA_HEADS, HD), F32),
                   jax.ShapeDtypeStruct((DB, 8, 128), I32)],
        compiler_params=_cp("arbitrary", "arbitrary"),
        name="sample_cmp_select",
    )(page_table.reshape(-1), slopes, pool, w1c, cb, w2, q8)


def _sample_sel_body(idx_ref, pt_ref, slopes_ref, page_ref, win_ref, q_ref, new_ref, oc_ref, sm_ref,
                     o_ref, m_sc, l_sc, acc_sc, *, NP, WB):
    b = pl.program_id(0)
    g = pl.program_id(1)
    t = pl.program_id(2)
    P = NP * PAGE_SIZE
    n_past = P // SEL_BLOCK
    scale = HD ** -0.5
    q8 = q_ref[0].astype(BF16)
    rowi = lax.broadcasted_iota(I32, (NSA_HEADS, 1), 0)
    slope = jnp.zeros((NSA_HEADS, 1), F32)
    for h in range(NSA_HEADS):
        slope = jnp.where(rowi == h, slopes_ref[h], slope)
    ibase = (b * NSA_KV + g) * SEL_TOPK

    @pl.when(t == 0)
    def _():
        m_sc[...] = jnp.full_like(m_sc, NEG_INF)
        l_sc[...] = jnp.zeros_like(l_sc)
        acc_sc[...] = jnp.zeros_like(acc_sc)

    def update(s, mask, v):
        m, l, acc = m_sc[...], l_sc[...], acc_sc[...]
        s = jnp.where(mask, s, NEG_INF)
        m_new = jnp.maximum(m, jnp.max(s, axis=1, keepdims=True))
        alpha = jnp.exp(m - m_new)
        p = jnp.where(mask, jnp.exp(s - m_new), 0.0)
        m_sc[...] = m_new
        l_sc[...] = alpha * l + jnp.sum(p, axis=1, keepdims=True)
        acc_sc[...] = alpha * acc + _dot(p.astype(BF16), v)

    j = idx_ref[ibase + jnp.minimum(t, SEL_TOPK - 1)]

    @pl.when((t < SEL_TOPK) & (j < n_past))
    def _():
        kb = page_ref[0, pl.ds(2 * NSA_KV + g, PAGE_SIZE, stride=8), :].astype(BF16)
        vb = page_ref[0, pl.ds(3 * NSA_KV + g, PAGE_SIZE, stride=8), :].astype(BF16)
        kpos = (j // 2) * PAGE_SIZE + lax.broadcasted_iota(I32, (1, PAGE_SIZE), 1)
        s = _dot_nt(q8, kb) * scale - slope * (P - kpos).astype(F32)
        update(s, (kpos // SEL_BLOCK) == j, vb)

    @pl.when(t == SEL_TOPK)
    def _():
        def vec_dot(krow):
            return jnp.sum(q8.astype(F32) * krow.astype(BF16).astype(F32), axis=1, keepdims=True)

        def add_token(carry, s, valid, vrow):
            m, l, acc = carry
            s = jnp.where(valid, s, NEG_INF)
            m_new = jnp.maximum(m, s)
            alpha = jnp.exp(m - m_new)
            p = jnp.where(valid, jnp.exp(s - m_new), 0.0)
            return (m_new, alpha * l + p,
                    alpha * acc + p.astype(BF16).astype(F32) * vrow.astype(BF16).astype(F32))

        n_new = jnp.int32(0)
        for r in range(SEL_TOPK):
            n_new = n_new + (idx_ref[ibase + r] == n_past).astype(I32)
        has_new = jnp.full((NSA_HEADS, 1), n_new, I32) > 0
        k_new = new_ref[0, pl.ds(2 * NSA_KV + g, 1), :]
        v_new = new_ref[0, pl.ds(3 * NSA_KV + g, 1), :]
        _, l_s, acc_s = add_token((m_sc[...], l_sc[...], acc_sc[...]),
                                  vec_dot(k_new) * scale, has_new, v_new)
        o_s = acc_s / l_s

        kw = win_ref[0, pl.ds(g, WB, stride=2 * NSA_KV), :].astype(BF16)
        vw = win_ref[0, pl.ds(NSA_KV + g, WB, stride=2 * NSA_KV), :].astype(BF16)
        d = WB - lax.broadcasted_iota(I32, (1, WB), 1)
        s = _dot_nt(q8, kw) * scale - slope * d.astype(F32)
        mask = d < WINDOW
        s = jnp.where(mask, s, NEG_INF)
        m = jnp.max(s, axis=1, keepdims=True)
        p = jnp.where(mask, jnp.exp(s - m), 0.0)
        carry = (m, jnp.sum(p, axis=1, keepdims=True), _dot(p.astype(BF16), vw))
        kw_new = new_ref[0, pl.ds(4 * NSA_KV + g, 1), :]
        vw_new = new_ref[0, pl.ds(5 * NSA_KV + g, 1), :]
        _, l_w, acc_w = add_token(carry, vec_dot(kw_new) * scale, rowi >= 0, vw_new)
        o_w = acc_w / l_w

        smb = jnp.broadcast_to(sm_ref[0], (NSA_HEADS, 128))
        lane = lax.broadcasted_iota(I32, (NSA_HEADS, 128), 1)

        def gate(c):
            return _sigmoid(jnp.sum(jnp.where(lane == rowi * 3 + c, smb, 0.0), axis=1, keepdims=True))

        o_ref[0, 0] = gate(0) * oc_ref[0] + gate(1) * o_s + gate(2) * o_w


def sample_sel_win(idx, page_table, pool, win, layer, n_pool, q8, newkv, oc, sm, slopes):
    DB, NP = page_table.shape
    WB = win.shape[1] // (2 * NSA_KV)
    n_past = NP * PAGE_SIZE // SEL_BLOCK
    pbase = layer * n_pool
    wbase = layer * DB
    idx_flat = idx[:, :NSA_KV, :SEL_TOPK].reshape(-1)

    def page_map(b, g, t, idx_r, pt_r):
        j = idx_r[(b * NSA_KV + g) * SEL_TOPK + jnp.minimum(t, SEL_TOPK - 1)]
        j = jnp.minimum(j, n_past - 1)
        return (pbase + pt_r[b * NP + j // 2], 0, 0)

    grid_spec = pltpu.PrefetchScalarGridSpec(
        num_scalar_prefetch=2,
        grid=(DB, NSA_KV, SEL_TOPK + 1),
        in_specs=[pl.BlockSpec(memory_space=pltpu.SMEM),
                  pl.BlockSpec((1, ROWS_PER_PAGE, HD), page_map),
                  pl.BlockSpec((1, WB * 2 * NSA_KV, HD), lambda b, g, t, i_, p_: (wbase + b, 0, 0)),
                  pl.BlockSpec((1, NSA_HEADS, HD), lambda b, g, t, i_, p_: (b, 0, 0)),
                  pl.BlockSpec((1, 6 * NSA_KV, HD), lambda b, g, t, i_, p_: (b, 0, 0)),
                  pl.BlockSpec((1, NSA_HEADS, HD), lambda b, g, t, i_, p_: (b, 0, 0)),
                  pl.BlockSpec((1, 1, 128), lambda b, g, t, i_, p_: (b, 0, 0))],
        out_specs=pl.BlockSpec((1, 1, NSA_HEADS, HD), lambda b, g, t, i_, p_: (b, g, 0, 0)),
        scratch_shapes=[pltpu.VMEM((NSA_HEADS, 1), F32), pltpu.VMEM((NSA_HEADS, 1), F32),
                        pltpu.VMEM((NSA_HEADS, HD), F32)],
    )
    return pl.pallas_call(
        functools.partial(_sample_sel_body, NP=NP, WB=WB),
        grid_spec=grid_spec,
        out_shape=jax.ShapeDtypeStruct((DB, NSA_KV, NSA_HEADS, HD), F32),
        compiler_params=_cp("arbitrary", "arbitrary", "arbitrary"),
        name="sample_sel_win",
    )(idx_flat, page_table.reshape(-1), slopes, pool, win, q8, newkv, oc, sm)


def _mlstm_step_body(bif_ref, gs_ref, ms_ref, qkv_ref, mo_ref, ng_ref, c_ref, n_ref,
                     o_ref, co_ref, no_ref, mo_out_ref):
    b = pl.program_id(0)
    H = M_HEADS
    ri = lax.broadcasted_iota(I32, (HD, HD), 0)
    ci = lax.broadcasted_iota(I32, (HD, HD), 1)
    for h in range(H):
        ig = jnp.full((1, HD), gs_ref[b, h] + bif_ref[0, h], F32)
        fp = jnp.full((1, HD), gs_ref[b, H + h] + bif_ref[1, h], F32)
        m = jnp.full((1, HD), ms_ref[b, h], F32)
        lf = -_softplus(-fp)
        inter = lf + m
        m_t = jnp.maximum(inter, ig)
        w_intra = jnp.exp(ig - m_t)
        w_inter = jnp.exp(inter - m_t)
        q = qkv_ref[0, h:h + 1, :]
        k = qkv_ref[0, H + h:H + h + 1, :] * (HD ** -0.5)
        v = qkv_ref[0, 2 * H + h:2 * H + h + 1, :]
        C = c_ref[0, h]
        n = n_ref[0, h:h + 1, :]
        s = jnp.sum(q * k, axis=1, keepdims=True) * w_intra
        qC = _dot(jnp.broadcast_to(q, (8, HD)).astype(BF16), C.astype(BF16))[0:1]
        num = s * v + w_inter * qC
        den = s + w_inter * jnp.sum(q * n, axis=1, keepdims=True)
        hh = num / jnp.maximum(jnp.abs(den), jnp.exp(-m_t))
        k_col = jnp.sum(jnp.where(ri == ci, jnp.broadcast_to(k, (HD, HD)), 0.0), axis=1, keepdims=True)
        co_ref[0, h] = w_inter * C + (k_col * w_intra) * v
        no_ref[0, h:h + 1, :] = w_inter * n + w_intra * k
        mo_out_ref[0, h:h + 1, :] = m_t
        mu = jnp.mean(hh, axis=1, keepdims=True)
        hc = hh - mu
        var = jnp.mean(hc * hc, axis=1, keepdims=True)
        hn = hc * lax.rsqrt(var + LN_EPS) * ng_ref[h:h + 1, :]
        o_ref[0, h:h + 1, :] = _sigmoid(mo_ref[0, h:h + 1, :]) * hn


def mlstm_step(b_if, gates, m_state, qkv, mo, norm_g, C, n):
    DB = qkv.shape[0]
    H = M_HEADS
    smem = pl.BlockSpec(memory_space=pltpu.SMEM)
    row = pl.BlockSpec((1, H, HD), lambda b: (b, 0, 0))
    cspec = pl.BlockSpec((1, H, HD, HD), lambda b: (b, 0, 0, 0))
    return pl.pallas_call(
        _mlstm_step_body,
        grid=(DB,),
        in_specs=[smem, smem, smem, pl.BlockSpec((1, 3 * H, HD), lambda b: (b, 0, 0)), row,
                  pl.BlockSpec((H, HD), lambda b: (0, 0)), cspec, row],
        out_specs=[row, cspec, row, row],
        out_shape=[jax.ShapeDtypeStruct((DB, H, HD), F32), jax.ShapeDtypeStruct((DB, H, HD, HD), F32),
                   jax.ShapeDtypeStruct((DB, H, HD), F32), jax.ShapeDtypeStruct((DB, H, HD), F32)],
        compiler_params=_cp("arbitrary"),
        name="mlstm_step",
    )(b_if, gates, m_state, qkv, mo, norm_g.reshape(H, HD), C, n)


def _rglru_step_body(rx_ref, rgt_ref, buf_ref, h_ref, cw_ref, vec_ref, wa_ref, wx_ref, o_ref, ho_ref):
    cw = cw_ref[...]
    vec = vec_ref[...]
    xc = vec[0:1] + rx_ref[...] * cw[CONV_W - 1:CONV_W]
    for j in range(CONV_W - 1):
        xc = xc + buf_ref[j] * cw[j:j + 1]
    a, u = _rglru_gates(xc, wa_ref[0].astype(BF16), wx_ref[0].astype(BF16), vec)
    h = a * h_ref[...] + u
    ho_ref[...] = h
    o_ref[...] = h * _gelu(rgt_ref[...])


def rglru_step(rx, rgt, buf_t, h0, conv_w, vec, w_a, w_x):
    DB = rx.shape[0]
    col = pl.BlockSpec((DB, 128), lambda n: (0, n))
    return pl.pallas_call(
        _rglru_step_body,
        grid=(R_BLOCKS,),
        in_specs=[col, col, pl.BlockSpec((CONV_W - 1, DB, 128), lambda n: (0, 0, n)), col,
                  pl.BlockSpec((CONV_W, 128), lambda n: (0, n)),
                  pl.BlockSpec((4, 128), lambda n: (0, n)),
                  pl.BlockSpec((1, 128, 128), lambda n: (n, 0, 0)),
                  pl.BlockSpec((1, 128, 128), lambda n: (n, 0, 0))],
        out_specs=[col, col],
        out_shape=[jax.ShapeDtypeStruct((DB, R_WIDTH), F32), jax.ShapeDtypeStruct((DB, R_WIDTH), F32)],
        compiler_params=_cp("arbitrary"),
        name="rglru_step",
    )(rx, rgt, buf_t, h0, conv_w, vec, w_a, w_x)


def pack_w_in(w):
    D = w.shape[0]
    parts = [w[:, 8744:14888], w[:, 0:2560], w[:, 2584:5656], w[:, 5672:6696], w[:, 6696:7720],
             w[:, 7720:8744], w[:, 2560:2584], w[:, 5656:5672], jnp.zeros((D, N_AL - 14888), w.dtype)]
    return jnp.concatenate(parts, axis=1).astype(BF16)


def prompt_layer(xp, lw):
    B, T, D = xp.shape
    x2 = xp.reshape(B * T, D)
    M = B * T
    z = mm(x2, lw["w_in"], tm=min(1024, M), tn=1152, out_dtype=F32)
    z3 = z.reshape(B, T, N_AL)
    kc, vc, cb = compress_from_z(z3, lw["phi_w1"], lw["pe"], lw["phi_b1"], lw["phi_w2"])
    o_a = nsa_prompt(z3, kc, vc, lw["slopes"])
    o_b, Cp, np_, mp = mlstm_prompt(z3, lw["b_if"], lw["norm_g"])
    o_c, hp = rglru_prompt(z3, lw["conv_w"], lw["rg_vec"], lw["w_a"], lw["w_x"])
    merged = merge(o_a.reshape(B * T, -1), o_b.reshape(B * T, -1), o_c.reshape(B * T, -1), z,
                   lw["w_branch"], tm=min(512, M), tn=1024, out_dtype=BF16)
    h = mm_res_ln(merged, lw["w_out"], x2, lw["ln_g"][0:1], lw["ln_b"][0:1], tm=min(256, M), tk=D)
    f1 = mm(h, lw["mlp_w1"], tm=min(1024, M), tn=1024, out_dtype=BF16, act="relu2")
    x_new = mm_res_ln(f1, lw["mlp_w2"], h, lw["ln_g"][1:2], lw["ln_b"][1:2], tm=min(512, M), tk=1024)
    kv = z3[:, :, OFF_KV:OFF_KV + 6 * NSA_KV * HD].reshape(B, T, 6, NSA_KV, HD)
    n_win = min(WINDOW, T)
    states = (kv[:, :, :4], kv[:, T - n_win:, 4:6], Cp, np_[:, :, 0], mp[:, :, 0, 0], hp[:, 0],
              z3[:, T - (CONV_W - 1):, OFF_RX:OFF_RX + R_WIDTH])
    return x_new.reshape(B, T, D), states, cb


def sample_layer(xs, lw, cb, layer, page_table, pool, n_pool, win, win_l, C0, n0, m0, h0, conv0):
    DB, D = xs.shape
    z = mm(xs, lw["w_in"], tm=DB, tn=1152, out_dtype=F32)
    q8 = z[:, OFF_Q:OFF_Q + NSA_WIDTH].reshape(DB, NSA_HEADS, HD)
    newkv = z[:, OFF_KV:OFF_KV + 6 * NSA_KV * HD].reshape(DB, 6 * NSA_KV, HD)
    sm = z[:, OFF_SM:OFF_SM + 128].reshape(DB, 1, 128)
    oc, idx = sample_cmp_select(page_table, pool, layer, n_pool, lw["w1c"], cb, lw["phi_w2"], q8,
                                lw["slopes"])
    osw = sample_sel_win(idx, page_table, pool, win, layer, n_pool, q8, newkv, oc, sm, lw["slopes"])
    o_a = jnp.concatenate([osw[:, g, g * NSA_GROUP:(g + 1) * NSA_GROUP] for g in range(NSA_KV)],
                          axis=1).reshape(DB, NSA_WIDTH)
    qkv = z[:, OFF_MQKV:OFF_MQKV + 3 * M_WIDTH].reshape(DB, 3 * M_HEADS, HD)
    mo = z[:, OFF_MO:OFF_MO + M_WIDTH].reshape(DB, M_HEADS, HD)
    gates = z[:, OFF_SM + SM_I:OFF_SM + SM_I + 2 * M_HEADS]
    o_b, Cs, ns, ms = mlstm_step(lw["b_if"], gates, m0, qkv, mo, lw["norm_g"], C0, n0)
    rx = z[:, OFF_RX:OFF_RX + R_WIDTH]
    o_c, hs = rglru_step(rx, z[:, OFF_RGT:OFF_RGT + R_WIDTH], conv0.transpose(1, 0, 2), h0,
                         lw["conv_w"], lw["rg_vec"], lw["w_a"], lw["w_x"])
    merged = merge(o_a, o_b.reshape(DB, M_WIDTH), o_c, z, lw["w_branch"], tm=DB, tn=1024, out_dtype=F32)
    h = mm_res_ln(merged, lw["w_out"], xs, lw["ln_g"][0:1], lw["ln_b"][0:1], tm=DB, tk=D)
    f1 = mm(h, lw["mlp_w1"], tm=DB, tn=1024, out_dtype=F32, act="relu2")
    x_new = mm_res_ln(f1, lw["mlp_w2"], h, lw["ln_g"][1:2], lw["ln_b"][1:2], tm=DB, tk=1024)
    kvn = newkv.reshape(DB, 1, 6, NSA_KV, HD)
    Wb = win_l.shape[1]
    keep = min(WINDOW, Wb + 1)
    win_new = jnp.concatenate([win_l, kvn[:, :, 4:6]], axis=1)[:, Wb + 1 - keep:]
    conv_new = jnp.concatenate([conv0[:, 1:], rx[:, None]], axis=1)
    states = (kvn[:, :, :4], win_new, Cs, ns, ms[:, :, 0], hs, conv_new)
    return x_new, states


def kernel(x_prompt, x_sample, cache_nsa_kv, cache_win_kv, state_mlstm_C, state_mlstm_n, state_mlstm_m,
           state_rglru_h, state_rglru_conv, page_table, w_in, nsa_pe, nsa_phi_w1, nsa_phi_b1, nsa_phi_w2,
           mlstm_b_if, mlstm_norm_g, rg_conv_w, rg_conv_b, rg_w_a, rg_b_a, rg_w_x, rg_b_x, rg_lambda,
           w_branch, w_out, ln_g, ln_b, mlp_w1, mlp_w2):
    DB, Tn, D = x_sample.shape
    assert Tn == 1 and D == D_MODEL
    depth, n_pool = cache_nsa_kv.shape[:2]
    pool = cache_nsa_kv.reshape(depth * n_pool, ROWS_PER_PAGE, HD)
    Wb = cache_win_kv.shape[2]
    win = cache_win_kv.reshape(depth * DB, Wb * 2 * NSA_KV, HD)
    slopes = alibi_slopes()
    half = CMP_STRIDE * HD
    xp = x_prompt
    xs = x_sample.reshape(DB, D)
    new_p = [[] for _ in range(7)]
    new_s = [[] for _ in range(7)]
    for l in range(depth):
        lw = {
            "w_in": pack_w_in(w_in[l]),
            "pe": nsa_pe[l], "phi_w1": nsa_phi_w1[l], "phi_b1": nsa_phi_b1[l], "phi_w2": nsa_phi_w2[l],
            "w1c": jnp.concatenate([nsa_phi_w1[l][:, :half], nsa_phi_w1[l][:, half:]], axis=2).astype(BF16),
            "slopes": slopes, "b_if": mlstm_b_if[l], "norm_g": mlstm_norm_g[l],
            "conv_w": rg_conv_w[l],
            "rg_vec": jnp.stack([rg_conv_b[l], rg_b_a[l], rg_b_x[l], rg_lambda[l]]),
            "w_a": rg_w_a[l], "w_x": rg_w_x[l],
            "w_branch": w_branch[l].astype(BF16), "w_out": w_out[l].astype(BF16),
            "ln_g": ln_g[l], "ln_b": ln_b[l],
            "mlp_w1": mlp_w1[l].astype(BF16), "mlp_w2": mlp_w2[l].astype(BF16),
        }
        xp, st_p, cb = prompt_layer(xp, lw)
        xs, st_s = sample_layer(xs, lw, cb, l, page_table, pool, n_pool, win, cache_win_kv[l],
                                state_mlstm_C[l], state_mlstm_n[l], state_mlstm_m[l],
                                state_rglru_h[l], state_rglru_conv[l])
        for lst, val in zip(new_p, st_p):
            lst.append(val)
        for lst, val in zip(new_s, st_s):
            lst.append(val)
    P = [jnp.stack(a) for a in new_p]
    S = [jnp.stack(a) for a in new_s]
    return (xp, xs.reshape(DB, Tn, D), P[0], S[0], P[1], S[1], P[2], S[2], P[3], S[3], P[4], S[4],
            P[5], S[5], P[6], S[6])
```

```python
import functools

import jax
import jax.numpy as jnp
from jax import lax
from jax.experimental import pallas as pl
from jax.experimental.pallas import tpu as pltpu

F32 = jnp.float32
BF16 = jnp.bfloat16
I32 = jnp.int32

D_MODEL = 2048
DEPTH = 4
PAGE_SIZE = 128
HD = 128
NSA_HEADS = 8
NSA_KV = 2
NSA_GROUP = 4
NSA_WIDTH = NSA_HEADS * HD
CMP_BLOCK = 32
CMP_STRIDE = 16
SEL_BLOCK = 64
SEL_TOPK = 16
WINDOW = 512
Q_BLOCK = 128
FORCE_BONUS = 1.0e4
M_HEADS = 8
M_WIDTH = M_HEADS * HD
R_WIDTH = 1024
R_BLOCKS = 8
CONV_W = 4
LRU_C = 8.0
D_FF = 4 * D_MODEL
DEEPNORM_ALPHA = (2 * DEPTH) ** 0.25
LN_EPS = 1e-5
NEG_INF = -1e30

OFF_MG = 0
OFF_Q = 6144
OFF_KV = 7168
OFF_MQKV = 8704
OFF_MO = 11776
OFF_RX = 12800
OFF_RGT = 13824
OFF_SM = 14848
N_AL = 14976
SM_I = 24
SM_F = 32

VMEM_LIMIT = 52 * 1024 * 1024


def _cp(*sem):
    return pltpu.CompilerParams(dimension_semantics=sem, vmem_limit_bytes=VMEM_LIMIT)


def _dot(a, b):
    return jnp.dot(a, b, preferred_element_type=F32)


def _dot_nt(a, b):
    return lax.dot_general(a, b, (((1,), (1,)), ((), ())), preferred_element_type=F32)


def _gelu(x):
    return 0.5 * x * (1.0 + jnp.tanh(0.7978845608028654 * (x + 0.044715 * (x * x * x))))


def _sigmoid(x):
    return 1.0 / (1.0 + jnp.exp(-x))


def _softplus(x):
    return jnp.maximum(x, 0.0) + jnp.log1p(jnp.exp(-jnp.abs(x)))


def _mm_body(a_ref, b_ref, o_ref, *, act):
    acc = _dot(a_ref[...].astype(BF16), b_ref[...])
    if act == "relu2":
        acc = jnp.square(jnp.maximum(acc, 0.0))
    o_ref[...] = acc.astype(o_ref.dtype)


def mm(a, b, *, tm, tn, out_dtype, act=None):
    M, K = a.shape
    N = b.shape[1]
    return pl.pallas_call(
        functools.partial(_mm_body, act=act),
        grid=(M // tm, N // tn),
        in_specs=[pl.BlockSpec((tm, K), lambda i, j: (i, 0)),
                  pl.BlockSpec((K, tn), lambda i, j: (0, j))],
        out_specs=pl.BlockSpec((tm, tn), lambda i, j: (i, j)),
        out_shape=jax.ShapeDtypeStruct((M, N), out_dtype),
        compiler_params=_cp("parallel", "arbitrary"),
        name="mm",
    )(a, b)


def _mm_ln_body(a_ref, b_ref, x_ref, g_ref, bb_ref, o_ref, acc_ref, *, nk):
    k = pl.program_id(1)

    @pl.when(k == 0)
    def _():
        acc_ref[...] = jnp.zeros_like(acc_ref)

    acc_ref[...] += _dot(a_ref[...].astype(BF16), b_ref[...])

    @pl.when(k == nk - 1)
    def _():
        y = DEEPNORM_ALPHA * x_ref[...] + acc_ref[...]
        mu = jnp.mean(y, axis=-1, keepdims=True)
        yc = y - mu
        var = jnp.mean(yc * yc, axis=-1, keepdims=True)
        o_ref[...] = yc * lax.rsqrt(var + LN_EPS) * g_ref[...] + bb_ref[...]


def mm_res_ln(a, b, x, g, bb, *, tm, tk):
    M, K = a.shape
    N = b.shape[1]
    nk = K // tk
    return pl.pallas_call(
        functools.partial(_mm_ln_body, nk=nk),
        grid=(M // tm, nk),
        in_specs=[pl.BlockSpec((tm, tk), lambda i, k: (i, k)),
                  pl.BlockSpec((tk, N), lambda i, k: (k, 0)),
                  pl.BlockSpec((tm, N), lambda i, k: (i, 0)),
                  pl.BlockSpec((1, N), lambda i, k: (0, 0)),
                  pl.BlockSpec((1, N), lambda i, k: (0, 0))],
        out_specs=pl.BlockSpec((tm, N), lambda i, k: (i, 0)),
        out_shape=jax.ShapeDtypeStruct((M, N), F32),
        scratch_shapes=[pltpu.VMEM((tm, N), F32)],
        compiler_params=_cp("parallel", "arbitrary"),
        name="mm_res_ln",
    )(a, b, x, g, bb)


def _merge_body(oa_ref, ob_ref, oc_ref, g0_ref, g1_ref, g2_ref, w_ref, o_ref):
    acc = _sigmoid(g0_ref[...]) * _dot(oa_ref[...].astype(BF16), w_ref[0])
    acc += _sigmoid(g1_ref[...]) * _dot(ob_ref[...].astype(BF16), w_ref[1])
    acc += _sigmoid(g2_ref[...]) * _dot(oc_ref[...].astype(BF16), w_ref[2])
    o_ref[...] = acc.astype(o_ref.dtype)


def merge(o_a, o_b, o_c, z, w_branch, *, tm, tn, out_dtype):
    M, W = o_a.shape
    N = w_branch.shape[2]
    assert OFF_MG % tn == 0 and N % tn == 0 and M % tm == 0
    gb = OFF_MG // tn
    nb = N // tn
    br = pl.BlockSpec((tm, W), lambda i, j: (i, 0))

    def gspec(k):
        return pl.BlockSpec((tm, tn), lambda i, j: (i, gb + k * nb + j))

    return pl.pallas_call(
        _merge_body,
        grid=(M // tm, nb),
        in_specs=[br, br, br, gspec(0), gspec(1), gspec(2),
                  pl.BlockSpec((3, W, tn), lambda i, j: (0, 0, j))],
        out_specs=pl.BlockSpec((tm, tn), lambda i, j: (i, j)),
        out_shape=jax.ShapeDtypeStruct((M, N), out_dtype),
        compiler_params=_cp("parallel", "arbitrary"),
        name="merge",
    )(o_a, o_b, o_c, z, z, z, w_branch)


def _compress_body(x_ref, w1_ref, pe_ref, b1_ref, w2_ref, o_ref, cb_ref):
    half = CMP_STRIDE * HD
    x = x_ref[0, 0].astype(BF16)
    w1 = w1_ref[0].astype(BF16)
    lead = _dot(x, w1[:half])
    trail = _dot(x, w1[half:])
    n = x.shape[0]
    pe8 = jnp.broadcast_to(pe_ref[0], (8, 2 * half)).astype(BF16)
    cb = _dot(pe8, w1) + b1_ref[0]
    trail_next = pltpu.roll(trail, n - 1, axis=0)
    hid = _gelu(lead + trail_next + cb[0:1])
    o_ref[0, 0] = _dot(hid.astype(BF16), w2_ref[0].astype(BF16))
    cb_ref[0] = cb


def compress_prompt(x, w1, pe, b1, w2):
    _, S, n, K = x.shape
    return pl.pallas_call(
        _compress_body,
        grid=(2, S),
        in_specs=[pl.BlockSpec((1, 1, n, K), lambda k, s: (k, s, 0, 0)),
                  pl.BlockSpec((1, 2 * K, HD), lambda k, s: (k, 0, 0)),
                  pl.BlockSpec((1, 1, 2 * K), lambda k, s: (k, 0, 0)),
                  pl.BlockSpec((1, 1, HD), lambda k, s: (k, 0, 0)),
                  pl.BlockSpec((1, HD, HD), lambda k, s: (k, 0, 0))],
        out_specs=[pl.BlockSpec((1, 1, n, HD), lambda k, s: (k, s, 0, 0)),
                   pl.BlockSpec((1, 8, HD), lambda k, s: (k, 0, 0))],
        out_shape=[jax.ShapeDtypeStruct((2, S, n, HD), F32),
                   jax.ShapeDtypeStruct((2, 8, HD), F32)],
        compiler_params=_cp("arbitrary", "arbitrary"),
        name="compress_prompt",
    )(x, w1, pe, b1, w2)


def compress_from_z(z3, w1, pe, b1, w2):
    B, T, _ = z3.shape
    G = NSA_KV
    n = T // CMP_STRIDE
    x = z3[:, :, OFF_KV:OFF_KV + 2 * G * HD].reshape(B, n, CMP_STRIDE, 2, G, HD)
    x = x.transpose(3, 0, 4, 1, 2, 5).reshape(2, B * G, n, CMP_STRIDE * HD)
    out, cb = compress_prompt(x, w1, pe.reshape(2, 1, CMP_BLOCK * HD), b1.reshape(2, 1, HD), w2)
    return out[0], out[1], cb


def alibi_slopes():
    return jnp.asarray([2.0 ** (-8.0 * (h + 1) / NSA_HEADS) for h in range(NSA_HEADS)], F32)


SEL_CHUNK = 512


def _online_update(carry, s, mask, v):
    m, l, acc = carry
    s = jnp.where(mask, s, NEG_INF)
    m_new = jnp.maximum(m, jnp.max(s, axis=1, keepdims=True))
    alpha = jnp.exp(m - m_new)
    p = jnp.where(mask, jnp.exp(s - m_new), 0.0)
    l = alpha * l + jnp.sum(p, axis=1, keepdims=True)
    acc = alpha * acc + _dot(p.astype(BF16), v)
    return m_new, l, acc


MASK_BIG = 1.0e30
LOG2E = 1.4426950408889634
WIN_KEYS = WINDOW + Q_BLOCK


def _nsa_prompt_body(slopes_ref, q_ref, ks_ref, vs_ref, kw_ref, vw_ref, kc_ref, vc_ref,
                     sm_ref, o_ref, ks_bf, vs_aug, et_bf, kw_pad, vw_aug, wbias, sbias, y_a, y_b,
                     *, T):
    assert T % (2 * SEL_CHUNK) == 0
    g = pl.program_id(1)
    i = pl.program_id(2)
    QB = Q_BLOCK
    R = NSA_GROUP
    RQ = R * QB
    NCP = T // CMP_STRIDE
    NC = NCP - 1
    NS = T // SEL_BLOCK
    NSP = 128
    assert NS <= NSP
    t0 = i * QB
    scale = HD ** -0.5
    c2 = scale * LOG2E

    @pl.when(i == 0)
    def _():
        lane = lax.broadcasted_iota(I32, (T, HD), 1)
        ones_col = jnp.where(lane == 0, 1.0, 0.0).astype(BF16)
        ks_bf[...] = ks_ref[0].astype(BF16)
        vs_aug[:, 0:HD] = vs_ref[0].astype(BF16)
        vs_aug[:, HD:2 * HD] = ones_col
        kk = lax.broadcasted_iota(I32, (T, HD), 0)
        et_bf[...] = jnp.where(kk // SEL_BLOCK == lane, MASK_BIG, 0.0).astype(BF16)
        kw_pad[0:WINDOW, :] = jnp.zeros((WINDOW, HD), BF16)
        kw_pad[WINDOW:WINDOW + T, :] = kw_ref[0].astype(BF16)
        vw_aug[0:WINDOW, :] = jnp.zeros((WINDOW, 2 * HD), BF16)
        vw_aug[WINDOW:WINDOW + T, 0:HD] = vw_ref[0].astype(BF16)
        vw_aug[WINDOW:WINDOW + T, HD:2 * HD] = ones_col
        wd = (lax.broadcasted_iota(I32, (QB, WIN_KEYS), 0) + WINDOW
              - lax.broadcasted_iota(I32, (QB, WIN_KEYS), 1))
        band = (wd >= 0) & (wd < WINDOW)
        wdf = wd.astype(F32)
        lf = lax.broadcasted_iota(I32, (QB, SEL_CHUNK), 1).astype(F32)
        for r in range(R):
            sl = slopes_ref[g * R + r]
            wbias[r * QB:(r + 1) * QB, :] = jnp.where(band, (-LOG2E * sl) * wdf, -MASK_BIG)
            sbias[r * QB:(r + 1) * QB, :] = (LOG2E * sl) * lf

    qb = q_ref[0]
    q4 = jnp.concatenate([qb[:, r * HD:(r + 1) * HD] for r in range(R)], axis=0).astype(BF16)
    row = lax.broadcasted_iota(I32, (RQ, 1), 0)
    qpos = t0 + (row & (QB - 1))
    slope = jnp.concatenate(
        [jnp.full((QB, 1), slopes_ref[g * R + r], F32) for r in range(R)], axis=0)

    kc = kc_ref[0, 0].astype(BF16)
    vc = vc_ref[0, 0].astype(BF16)
    n_idx = lax.broadcasted_iota(I32, (1, NCP), 1)
    dist = qpos - (n_idx * CMP_STRIDE + (CMP_BLOCK - 1))
    mask = (dist >= 0) & (n_idx < NC)
    s = _dot_nt(q4, kc) * scale - slope * dist.astype(F32)
    s = jnp.where(mask, s, NEG_INF)
    m = jnp.max(s, axis=1, keepdims=True)
    p = jnp.where(mask, jnp.exp(s - m), 0.0)
    l = jnp.sum(p, axis=1, keepdims=True)
    p = p / jnp.where(l > 0.0, l, 1.0)
    o_c = _dot(p.astype(BF16), vc)

    psum = p[0:QB]
    for r in range(1, R):
        psum = psum + p[r * QB:(r + 1) * QB]
    p_hi = psum.astype(BF16)
    p_lo = (psum - p_hi.astype(F32)).astype(BF16)
    jo = lax.broadcasted_iota(I32, (NS, NCP), 0) * SEL_BLOCK
    no = lax.broadcasted_iota(I32, (NS, NCP), 1)
    ov = ((no * CMP_STRIDE < jo + SEL_BLOCK) & (no * CMP_STRIDE + CMP_BLOCK > jo)
          & (no < NC)).astype(BF16)
    imp = _dot_nt(ov, p_hi) + _dot_nt(ov, p_lo)
    jj = lax.broadcasted_iota(I32, (NS, QB), 0)
    qp = t0 + lax.broadcasted_iota(I32, (NS, QB), 1)
    cur = qp // SEL_BLOCK
    forced = (jj == 0) | (jj == cur) | (jj == cur - 1)
    imp = jnp.where(forced, imp + FORCE_BONUS, imp)
    imp = jnp.where(jj * SEL_BLOCK <= qp, imp, -1.0)
    ranks = [jnp.zeros((8, QB), F32) for _ in range(NS // 8)]
    for j2 in range(NS):
        rv = imp[j2:j2 + 1, :]
        for v in range(NS // 8):
            blk = imp[8 * v:8 * v + 8]
            if 8 * v > j2:
                beats = rv >= blk
            elif 8 * v + 7 < j2:
                beats = rv > blk
            else:
                beats = (rv > blk) | ((rv == blk) & (jj[8 * v:8 * v + 8] > j2))
            ranks[v] = ranks[v] + jnp.where(beats, 1.0, 0.0)
    rank = jnp.concatenate(ranks, axis=0)
    selm_t = jnp.where(rank < SEL_TOPK, 0.0, -1.0)
    if NSP > NS:
        selm_t = jnp.concatenate([selm_t, jnp.zeros((NSP - NS, QB), F32)], axis=0)
    selm = selm_t.T.astype(BF16)

    sl_col = LOG2E * slope
    rel = (lax.broadcasted_iota(I32, (QB, SEL_CHUNK), 1)
           - lax.broadcasted_iota(I32, (QB, SEL_CHUNK), 0))

    def scores(c):
        k0 = pl.multiple_of(c * SEL_CHUNK, SEL_CHUNK)
        mb = _dot_nt(selm, et_bf[pl.ds(k0, SEL_CHUNK), :])
        mb = jnp.where(rel <= t0 - k0, mb, -MASK_BIG)
        return (_dot_nt(q4, ks_bf[pl.ds(k0, SEL_CHUNK), :]) * c2 + sbias[...]
                + jnp.concatenate([mb] * R, axis=0))

    def consume(c, y, carry):
        m, acc = carry
        k0 = pl.multiple_of(c * SEL_CHUNK, SEL_CHUNK)
        off = sl_col * (k0 - t0).astype(F32)
        m_new = jnp.maximum(m, jnp.max(y, axis=1, keepdims=True) + off)
        p = jnp.exp2(y - (m_new - off))
        acc = jnp.exp2(m - m_new) * acc + _dot(p.astype(BF16), vs_aug[pl.ds(k0, SEL_CHUNK), :])
        return m_new, acc

    last_chunk = T // SEL_CHUNK - 1

    def pair(pi, carry):
        c0 = 2 * pi
        y_b[...] = scores(c0 + 1)
        carry = consume(c0, y_a[...], carry)
        y_a[...] = scores(jnp.minimum(c0 + 2, last_chunk))
        return consume(c0 + 1, y_b[...], carry)

    init = (jnp.full((RQ, 1), -MASK_BIG, F32), jnp.zeros((RQ, 2 * HD), F32))
    n_chunks = (t0 + QB + SEL_CHUNK - 1) // SEL_CHUNK
    y_a[...] = scores(0)
    _, acc_s = lax.fori_loop(0, (n_chunks + 1) // 2, pair, init)
    o_s = acc_s[:, 0:HD] / acc_s[:, HD:HD + 1]

    w0 = pl.multiple_of(t0, QB)
    vrow = jnp.where(lax.broadcasted_iota(I32, (1, WIN_KEYS), 1) >= WINDOW - t0, 0.0, -MASK_BIG)
    y = _dot_nt(q4, kw_pad[pl.ds(w0, WIN_KEYS), :]) * c2 + wbias[...] + vrow
    p = jnp.exp2(y - jnp.max(y, axis=1, keepdims=True))
    acc_w = _dot(p.astype(BF16), vw_aug[pl.ds(w0, WIN_KEYS), :])
    o_w = acc_w[:, 0:HD] / acc_w[:, HD:HD + 1]

    smb = sm_ref[0]
    lane = lax.broadcasted_iota(I32, (QB, 128), 1)

    def gate(c):
        cols = []
        for r in range(R):
            idx = (g * R + r) * 3 + c
            cols.append(jnp.sum(jnp.where(lane == idx, smb, 0.0), axis=1, keepdims=True))
        return _sigmoid(jnp.concatenate(cols, axis=0))

    o = gate(0) * o_c + gate(1) * o_s + gate(2) * o_w
    o_ref[0] = jnp.concatenate([o[r * QB:(r + 1) * QB] for r in range(R)], axis=1)


def nsa_prompt(z3, kc, vc, slopes):
    B, T, _ = z3.shape
    G = NSA_KV
    kvb = OFF_KV // HD

    def kvspec(kind):
        return pl.BlockSpec((1, T, HD), lambda b, g, i: (b, 0, kvb + kind * G + g))

    cspec = pl.BlockSpec((1, 1, T // CMP_STRIDE, HD), lambda b, g, i: (b * G + g, 0, 0, 0))
    GW = NSA_GROUP * HD
    assert OFF_Q % GW == 0
    qspec = pl.BlockSpec((1, Q_BLOCK, GW), lambda b, g, i: (b, i, OFF_Q // GW + g))
    return pl.pallas_call(
        functools.partial(_nsa_prompt_body, T=T),
        grid=(B, G, T // Q_BLOCK),
        in_specs=[pl.BlockSpec(memory_space=pltpu.SMEM),
                  qspec, kvspec(2), kvspec(3), kvspec(4), kvspec(5), cspec, cspec,
                  pl.BlockSpec((1, Q_BLOCK, 128), lambda b, g, i: (b, i, OFF_SM // 128))],
        out_specs=pl.BlockSpec((1, Q_BLOCK, GW), lambda b, g, i: (b, i, g)),
        out_shape=jax.ShapeDtypeStruct((B, T, NSA_WIDTH), F32),
        scratch_shapes=[pltpu.VMEM((T, HD), BF16),
                        pltpu.VMEM((T, 2 * HD), BF16),
                        pltpu.VMEM((T, HD), BF16),
                        pltpu.VMEM((T + WINDOW, HD), BF16),
                        pltpu.VMEM((T + WINDOW, 2 * HD), BF16),
                        pltpu.VMEM((NSA_GROUP * Q_BLOCK, WIN_KEYS), F32),
                        pltpu.VMEM((NSA_GROUP * Q_BLOCK, SEL_CHUNK), F32),
                        pltpu.VMEM((NSA_GROUP * Q_BLOCK, SEL_CHUNK), F32),
                        pltpu.VMEM((NSA_GROUP * Q_BLOCK, SEL_CHUNK), F32)],
        compiler_params=_cp("arbitrary", "arbitrary", "arbitrary"),
        name="nsa_prompt",
    )(slopes, z3, z3, z3, z3, z3, kc[:, None], vc[:, None], z3)


M_CHUNK_K = 128


def _mlstm_body(bif_ref, q_ref, k_ref, v_ref, sm_ref, mo_ref, ng_ref,
                o_ref, c_ref, n_ref, m_ref, *, T):
    h = pl.program_id(1)
    L = M_CHUNK_K
    bi = bif_ref[0, h]
    bf = bif_ref[1, h]
    lane = lax.broadcasted_iota(I32, (L, 128), 1)
    ri = lax.broadcasted_iota(I32, (L, L), 0)
    ci = lax.broadcasted_iota(I32, (L, L), 1)
    eye = ri == ci
    tril = ri >= ci
    ng = ng_ref[...]

    def chunk(c, carry):
        C, n, m = carry
        r0 = pl.multiple_of(c * L, L)
        q = q_ref[0, pl.ds(r0, L), :]
        k = k_ref[0, pl.ds(r0, L), :] * (HD ** -0.5)
        v = v_ref[0, pl.ds(r0, L), :]
        smb = sm_ref[0, pl.ds(r0, L), :]
        ig_col = jnp.sum(jnp.where(lane == SM_I + h, smb, 0.0), axis=1, keepdims=True) + bi
        fp_col = jnp.sum(jnp.where(lane == SM_F + h, smb, 0.0), axis=1, keepdims=True) + bf
        lf_col = -_softplus(-fp_col)
        lf_row = jnp.sum(jnp.where(eye, lf_col, 0.0), axis=0, keepdims=True)
        ig_row = jnp.sum(jnp.where(eye, ig_col, 0.0), axis=0, keepdims=True)
        b_col = jnp.sum(jnp.where(tril, lf_row, 0.0), axis=1, keepdims=True)
        b_row = jnp.sum(jnp.where(ri <= ci, lf_col, 0.0), axis=0, keepdims=True)
        d = jnp.where(tril, b_col - b_row + ig_row, NEG_INF)
        inter = b_col + m
        m_t = jnp.maximum(inter, jnp.max(d, axis=1, keepdims=True))
        w_intra = jnp.exp(d - m_t)
        w_inter = jnp.exp(inter - m_t)
        qb = q.astype(BF16)
        vb = v.astype(BF16)
        s = _dot_nt(qb, k.astype(BF16)) * w_intra
        num = _dot(s.astype(BF16), vb) + w_inter * _dot(qb, C.astype(BF16))
        den = jnp.sum(s, axis=1, keepdims=True) + w_inter * jnp.sum(q * n, axis=1, keepdims=True)
        hh = num / jnp.maximum(jnp.abs(den), jnp.exp(-m_t))
        m_new = m_t[L - 1:L, :]
        b_last = b_col[L - 1:L, :]
        decay = jnp.exp(b_last + m - m_new)
        w_s = jnp.exp(b_last - b_col + ig_col - m_new)
        kw = k * w_s
        C_new = decay * C + _dot(kw.T.astype(BF16), vb)
        n_new = decay * n + jnp.sum(kw, axis=0, keepdims=True)
        mu = jnp.mean(hh, axis=1, keepdims=True)
        hc = hh - mu
        var = jnp.mean(hc * hc, axis=1, keepdims=True)
        hn = hc * lax.rsqrt(var + LN_EPS) * ng
        o_ref[0, pl.ds(r0, L), :] = _sigmoid(mo_ref[0, pl.ds(r0, L), :]) * hn
        return C_new, n_new, m_new

    init = (jnp.zeros((HD, HD), F32), jnp.zeros((1, HD), F32), jnp.zeros((1, 1), F32))
    C, n, m = lax.fori_loop(0, T // L, chunk, init)
    c_ref[0, 0] = C
    n_ref[0, 0] = n
    m_ref[0, 0] = jnp.broadcast_to(m, (1, 128))


def mlstm_prompt(z3, b_if, norm_g):
    B, T, _ = z3.shape
    H = M_HEADS
    qb = OFF_MQKV // HD

    def hspec(off):
        return pl.BlockSpec((1, T, HD), lambda b, h: (b, 0, off + h))

    return pl.pallas_call(
        functools.partial(_mlstm_body, T=T),
        grid=(B, H),
        in_specs=[pl.BlockSpec(memory_space=pltpu.SMEM),
                  hspec(qb), hspec(qb + H), hspec(qb + 2 * H),
                  pl.BlockSpec((1, T, 128), lambda b, h: (b, 0, OFF_SM // 128)),
                  hspec(OFF_MO // HD),
                  pl.BlockSpec((1, HD), lambda b, h: (0, h))],
        out_specs=[pl.BlockSpec((1, T, HD), lambda b, h: (b, 0, h)),
                   pl.BlockSpec((1, 1, HD, HD), lambda b, h: (b, h, 0, 0)),
                   pl.BlockSpec((1, 1, 1, HD), lambda b, h: (b, h, 0, 0)),
                   pl.BlockSpec((1, 1, 1, 128), lambda b, h: (b, h, 0, 0))],
        out_shape=[jax.ShapeDtypeStruct((B, T, M_WIDTH), F32),
                   jax.ShapeDtypeStruct((B, H, HD, HD), F32),
                   jax.ShapeDtypeStruct((B, H, 1, HD), F32),
                   jax.ShapeDtypeStruct((B, H, 1, 128), F32)],
        compiler_params=_cp("parallel", "parallel"),
        name="mlstm_prompt",
    )(b_if, z3, z3, z3, z3, z3, norm_g.reshape(1, M_WIDTH))


R_CHUNK = 128


def _rglru_gates(xc, wa, wx, vec):
    xb = xc.astype(BF16)
    r = _sigmoid(_dot(xb, wa) + vec[1:2])
    i = _sigmoid(_dot(xb, wx) + vec[2:3])
    log_a = -LRU_C * r * _softplus(-vec[3:4])
    a = jnp.exp(log_a)
    th = jnp.tanh(log_a)
    u = jnp.sqrt(-2.0 * th / (1.0 - th)) * (i * xc)
    return a, u


def _rglru_body(rx_ref, rgt_ref, cw_ref, vec_ref, wa_ref, wx_ref, o_ref, hl_ref, xpad_ref, *, T):
    Tc = R_CHUNK
    xpad_ref[0:8, :] = jnp.zeros((8, 128), F32)
    xpad_ref[8:T + 8, :] = rx_ref[0]
    cw = cw_ref[...]
    vec = vec_ref[...]
    wa = wa_ref[0].astype(BF16)
    wx = wx_ref[0].astype(BF16)
    rowmod = lax.broadcasted_iota(I32, (Tc, 128), 0) & 7

    def chunk(c, h):
        r0 = pl.multiple_of(c * Tc, Tc)
        xc = vec[0:1]
        for j in range(CONV_W):
            xc = xc + xpad_ref[pl.ds(r0 + (8 - (CONV_W - 1)) + j, Tc), :] * cw[j:j + 1]
        a, u = _rglru_gates(xc, wa, wx, vec)
        for sft in (1, 2, 4):
            a1 = pltpu.roll(a, sft, axis=0)
            u1 = pltpu.roll(u, sft, axis=0)
            ok = rowmod >= sft
            u = jnp.where(ok, a * u1 + u, u)
            a = jnp.where(ok, a * a1, a)
        hs = []
        for gi in range(Tc // 8):
            hg = a[gi * 8:(gi + 1) * 8] * h + u[gi * 8:(gi + 1) * 8]
            hs.append(hg)
            h = hg[7:8]
        hf = jnp.concatenate(hs, axis=0)
        o_ref[0, pl.ds(r0, Tc), :] = hf * _gelu(rgt_ref[0, pl.ds(r0, Tc), :])
        return h

    h = lax.fori_loop(0, T // Tc, chunk, jnp.zeros((1, 128), F32))
    hl_ref[0] = h


def rglru_prompt(z3, conv_w, vec, w_a, w_x):
    B, T, _ = z3.shape
    NB = R_BLOCKS
    return pl.pallas_call(
        functools.partial(_rglru_body, T=T),
        grid=(B, NB),
        in_specs=[pl.BlockSpec((1, T, 128), lambda b, n: (b, 0, OFF_RX // 128 + n)),
                  pl.BlockSpec((1, T, 128), lambda b, n: (b, 0, OFF_RGT // 128 + n)),
                  pl.BlockSpec((CONV_W, 128), lambda b, n: (0, n)),
                  pl.BlockSpec((4, 128), lambda b, n: (0, n)),
                  pl.BlockSpec((1, 128, 128), lambda b, n: (n, 0, 0)),
                  pl.BlockSpec((1, 128, 128), lambda b, n: (n, 0, 0))],
        out_specs=[pl.BlockSpec((1, T, 128), lambda b, n: (b, 0, n)),
                   pl.BlockSpec((1, 1, 128), lambda b, n: (b, 0, n))],
        out_shape=[jax.ShapeDtypeStruct((B, T, R_WIDTH), F32),
                   jax.ShapeDtypeStruct((B, 1, R_WIDTH), F32)],
        scratch_shapes=[pltpu.VMEM((T + 8, 128), F32)],
        compiler_params=_cp("parallel", "parallel"),
        name="rglru_prompt",
    )(z3, z3, conv_w, vec, w_a, w_x)


ROWS_PER_PAGE = PAGE_SIZE * 4 * NSA_KV
CHUNKS_PER_PAGE = PAGE_SIZE // CMP_STRIDE
PAGES_PER_STEP = 16


def _sample_cmp_body(pt_ref, slopes_ref, *refs, NP, GRP):
    page_refs = refs[:GRP]
    w1c_ref, cb_ref, w2_ref, q_ref, oc_ref, idx_ref, xs_ref, lt_ref = refs[GRP:]
    p = pl.program_id(1)
    P = NP * PAGE_SIZE
    CPG = GRP * CHUNKS_PER_PAGE
    NCH = NP * CHUNKS_PER_PAGE
    n_sel = P // SEL_BLOCK + 1
    NSP = -(-n_sel // 128) * 128
    scale = HD ** -0.5

    for k in range(GRP):
        for kg in range(2 * NSA_KV):
            for c in range(CMP_STRIDE):
                v = page_refs[k][0, pl.ds(c * 8 + kg, CHUNKS_PER_PAGE, stride=CMP_STRIDE * 8), :]
                xs_ref[kg, k * CHUNKS_PER_PAGE:(k + 1) * CHUNKS_PER_PAGE, c * HD:(c + 1) * HD] = v

    r0 = pl.multiple_of(p * CPG, CPG)
    for kind in range(2):
        x = jnp.concatenate([xs_ref[kind * NSA_KV + g] for g in range(NSA_KV)], axis=0).astype(BF16)
        lt = _dot(x, w1c_ref[kind])
        for g in range(NSA_KV):
            lt_ref[kind * NSA_KV + g, pl.ds(r0, CPG), :] = lt[g * CPG:(g + 1) * CPG]

    @pl.when(p == NP // GRP - 1)
    def _():
        q128 = jnp.concatenate([q_ref[0], jnp.zeros((128 - NSA_HEADS, HD), F32)], axis=0).astype(BF16)
        lane = lax.broadcasted_iota(I32, (1, 128), 1)
        slope_row = jnp.zeros((1, 128), F32)
        for h in range(NSA_HEADS):
            slope_row = jnp.where(lane == h, slopes_ref[h], slope_row)
        n_col = lax.broadcasted_iota(I32, (NCH, 1), 0)
        dist = P - (n_col * CMP_STRIDE + (CMP_BLOCK - 1))
        maskc = (dist >= 0) & (n_col < NCH - 1)
        dist_f = dist.astype(F32)

        def cmp_rows(kind, g):
            lead = lt_ref[kind * NSA_KV + g, :, 0:HD]
            trail = lt_ref[kind * NSA_KV + g, :, HD:2 * HD]
            hid = _gelu(lead + pltpu.roll(trail, NCH - 1, axis=0) + cb_ref[kind, 0:1])
            return _dot(hid.astype(BF16), w2_ref[kind].astype(BF16)).astype(BF16)

        oc = jnp.zeros((128, HD), F32)
        psum2 = jnp.zeros((NCH, 128), F32)
        for g in range(NSA_KV):
            kc = cmp_rows(0, g)
            vc = cmp_rows(1, g)
            s = _dot_nt(kc, q128) * scale - slope_row * dist_f
            s = jnp.where(maskc, s, NEG_INF)
            m = jnp.max(s, axis=0, keepdims=True)
            pt = jnp.where(maskc, jnp.exp(s - m), 0.0)
            l = jnp.sum(pt, axis=0, keepdims=True)
            pt = pt / jnp.where(l > 0.0, l, 1.0)
            in_g = (lane >= g * NSA_GROUP) & (lane < (g + 1) * NSA_GROUP)
            pg = jnp.where(in_g, pt, 0.0)
            oc = oc + _dot(pg.T.astype(BF16), vc)
            psum2 = psum2 + jnp.where(lane == g, jnp.sum(pg, axis=1, keepdims=True), 0.0)
        oc_ref[0] = oc[0:NSA_HEADS]

        p_hi = psum2.astype(BF16)
        p_lo = (psum2 - p_hi.astype(F32)).astype(BF16)
        jo = lax.broadcasted_iota(I32, (NSP, NCH), 0) * SEL_BLOCK
        no = lax.broadcasted_iota(I32, (NSP, NCH), 1)
        ov = ((no * CMP_STRIDE < jo + SEL_BLOCK) & (no * CMP_STRIDE + CMP_BLOCK > jo)
              & (no < NCH - 1)).astype(BF16)
        imp = _dot(ov, p_hi) + _dot(ov, p_lo)
        jcol = lax.broadcasted_iota(I32, (NSP, 1), 0)
        cur = P // SEL_BLOCK
        forced = (jcol == 0) | (jcol == cur) | (jcol == cur - 1)
        imp = jnp.where(forced, imp + FORCE_BONUS, imp)
        imp = jnp.where(jcol * SEL_BLOCK <= P, imp, -1.0)
        imp = jnp.where(jcol < n_sel, imp, -2.0)
        ri = lax.broadcasted_iota(I32, (NSP, NSP), 0)
        ci = lax.broadcasted_iota(I32, (NSP, NSP), 1)
        lane_n = lax.broadcasted_iota(I32, (NSP, 128), 1)
        jcol_f = jcol.astype(F32)
        idx_ref[0] = jnp.zeros((8, 128), I32)
        for g in range(NSA_KV):
            col = jnp.sum(jnp.where(lane_n == g, imp, 0.0), axis=1, keepdims=True)
            rowv = jnp.sum(jnp.where(ri == ci, col, 0.0), axis=0, keepdims=True)
            beats = (rowv > col) | ((rowv == col) & (ci < ri))
            rank = jnp.sum(beats.astype(F32), axis=1, keepdims=True)
            hit = rank == lane_n.astype(F32)
            idx_ref[0, g:g + 1, :] = jnp.sum(jnp.where(hit, jcol_f, 0.0), axis=0,
                                             keepdims=True).astype(I32)


def sample_cmp_select(page_table, pool, layer, n_pool, w1c, cb, w2, q8, slopes):
    DB, NP = page_table.shape
    GRP = min(PAGES_PER_STEP, NP)
    assert NP % GRP == 0
    NCH = NP * CHUNKS_PER_PAGE
    K2 = CMP_STRIDE * HD
    base = layer * n_pool

    def page_spec(k):
        return pl.BlockSpec((1, ROWS_PER_PAGE, HD),
                            lambda b, p, pt: (base + pt[b * NP + p * GRP + k], 0, 0))

    grid_spec = pltpu.PrefetchScalarGridSpec(
        num_scalar_prefetch=1,
        grid=(DB, NP // GRP),
        in_specs=[pl.BlockSpec(memory_space=pltpu.SMEM)] + [page_spec(k) for k in range(GRP)] + [
                  pl.BlockSpec((2, K2, 2 * HD), lambda b, p, pt: (0, 0, 0)),
                  pl.BlockSpec((2, 8, HD), lambda b, p, pt: (0, 0, 0)),
                  pl.BlockSpec((2, HD, HD), lambda b, p, pt: (0, 0, 0)),
                  pl.BlockSpec((1, NSA_HEADS, HD), lambda b, p, pt: (b, 0, 0))],
        out_specs=[pl.BlockSpec((1, NSA_HEADS, HD), lambda b, p, pt: (b, 0, 0)),
                   pl.BlockSpec((1, 8, 128), lambda b, p, pt: (b, 0, 0))],
        scratch_shapes=[pltpu.VMEM((2 * NSA_KV, GRP * CHUNKS_PER_PAGE, K2), F32),
                        pltpu.VMEM((2 * NSA_KV, NCH, 2 * HD), F32)],
    )
    return pl.pallas_call(
        functools.partial(_sample_cmp_body, NP=NP, GRP=GRP),
        grid_spec=grid_spec,
        out_shape=[jax.ShapeDtypeStruct((DB, NSA_HEADS, HD), F32),
                   jax.ShapeDtypeStruct((DB, 8, 128), I32)],
        compiler_params=_cp("arbitrary", "arbitrary"),
        name="sample_cmp_select",
    )(page_table.reshape(-1), slopes, *([pool] * GRP), w1c, cb, w2, q8)


def _sample_sel_body(idx_ref, pt_ref, slopes_ref, *refs, NP, WB):
    page_refs = refs[:SEL_TOPK]
    win_ref, q_ref, new_ref, oc_ref, sm_ref, o_ref = refs[SEL_TOPK:]
    b = pl.program_id(0)
    g = pl.program_id(1)
    P = NP * PAGE_SIZE
    n_past = P // SEL_BLOCK
    scale = HD ** -0.5
    q8 = q_ref[0].astype(BF16)
    rowi = lax.broadcasted_iota(I32, (NSA_HEADS, 1), 0)
    slope = jnp.zeros((NSA_HEADS, 1), F32)
    for h in range(NSA_HEADS):
        slope = jnp.where(rowi == h, slopes_ref[h], slope)
    ibase = (b * NSA_KV + g) * SEL_TOPK

    def vec_dot(krow):
        return jnp.sum(q8.astype(F32) * krow.astype(BF16).astype(F32), axis=1, keepdims=True)

    def add_token(carry, s, valid, vrow):
        m, l, acc = carry
        s = jnp.where(valid, s, NEG_INF)
        m_new = jnp.maximum(m, s)
        alpha = jnp.exp(m - m_new)
        p = jnp.where(valid, jnp.exp(s - m_new), 0.0)
        return (m_new, alpha * l + p,
                alpha * acc + p.astype(BF16).astype(F32) * vrow.astype(BF16).astype(F32))

    carry = (jnp.full((NSA_HEADS, 1), NEG_INF, F32), jnp.zeros((NSA_HEADS, 1), F32),
             jnp.zeros((NSA_HEADS, HD), F32))
    n_new = jnp.int32(0)
    lane = lax.broadcasted_iota(I32, (1, PAGE_SIZE), 1)
    for t in range(SEL_TOPK):
        j = idx_ref[ibase + t]
        n_new = n_new + (j == n_past).astype(I32)
        kb = page_refs[t][0, pl.ds(2 * NSA_KV + g, PAGE_SIZE, stride=8), :].astype(BF16)
        vb = page_refs[t][0, pl.ds(3 * NSA_KV + g, PAGE_SIZE, stride=8), :].astype(BF16)
        kpos = (j // 2) * PAGE_SIZE + lane
        s = _dot_nt(q8, kb) * scale - slope * (P - kpos).astype(F32)
        jv = jnp.full((1, PAGE_SIZE), j, I32)
        carry = _online_update(carry, s, ((kpos // SEL_BLOCK) == jv) & (jv < n_past), vb)

    has_new = jnp.full((NSA_HEADS, 1), n_new, I32) > 0
    k_new = new_ref[0, pl.ds(2 * NSA_KV + g, 1), :]
    v_new = new_ref[0, pl.ds(3 * NSA_KV + g, 1), :]
    _, l_s, acc_s = add_token(carry, vec_dot(k_new) * scale, has_new, v_new)
    o_s = acc_s / l_s

    kw = win_ref[0, pl.ds(g, WB, stride=2 * NSA_KV), :].astype(BF16)
    vw = win_ref[0, pl.ds(NSA_KV + g, WB, stride=2 * NSA_KV), :].astype(BF16)
    d = WB - lax.broadcasted_iota(I32, (1, WB), 1)
    s = _dot_nt(q8, kw) * scale - slope * d.astype(F32)
    mask = d < WINDOW
    s = jnp.where(mask, s, NEG_INF)
    m = jnp.max(s, axis=1, keepdims=True)
    p = jnp.where(mask, jnp.exp(s - m), 0.0)
    carry = (m, jnp.sum(p, axis=1, keepdims=True), _dot(p.astype(BF16), vw))
    kw_new = new_ref[0, pl.ds(4 * NSA_KV + g, 1), :]
    vw_new = new_ref[0, pl.ds(5 * NSA_KV + g, 1), :]
    _, l_w, acc_w = add_token(carry, vec_dot(kw_new) * scale, rowi >= 0, vw_new)
    o_w = acc_w / l_w

    smb = jnp.broadcast_to(sm_ref[0], (NSA_HEADS, 128))
    lane_h = lax.broadcasted_iota(I32, (NSA_HEADS, 128), 1)

    def gate(c):
        return _sigmoid(jnp.sum(jnp.where(lane_h == rowi * 3 + c, smb, 0.0), axis=1, keepdims=True))

    o_ref[0, 0] = gate(0) * oc_ref[0] + gate(1) * o_s + gate(2) * o_w


def sample_sel_win(idx, page_table, pool, win, layer, n_pool, q8, newkv, oc, sm, slopes):
    DB, NP = page_table.shape
    WB = win.shape[1] // (2 * NSA_KV)
    n_past = NP * PAGE_SIZE // SEL_BLOCK
    pbase = layer * n_pool
    wbase = layer * DB
    idx_flat = idx[:, :NSA_KV, :SEL_TOPK].reshape(-1)

    def page_spec(t):
        def page_map(b, g, idx_r, pt_r):
            j = jnp.minimum(idx_r[(b * NSA_KV + g) * SEL_TOPK + t], n_past - 1)
            return (pbase + pt_r[b * NP + j // 2], 0, 0)
        return pl.BlockSpec((1, ROWS_PER_PAGE, HD), page_map)

    grid_spec = pltpu.PrefetchScalarGridSpec(
        num_scalar_prefetch=2,
        grid=(DB, NSA_KV),
        in_specs=[pl.BlockSpec(memory_space=pltpu.SMEM)] + [page_spec(t) for t in range(SEL_TOPK)] + [
                  pl.BlockSpec((1, WB * 2 * NSA_KV, HD), lambda b, g, i_, p_: (wbase + b, 0, 0)),
                  pl.BlockSpec((1, NSA_HEADS, HD), lambda b, g, i_, p_: (b, 0, 0)),
                  pl.BlockSpec((1, 6 * NSA_KV, HD), lambda b, g, i_, p_: (b, 0, 0)),
                  pl.BlockSpec((1, NSA_HEADS, HD), lambda b, g, i_, p_: (b, 0, 0)),
                  pl.BlockSpec((1, 1, 128), lambda b, g, i_, p_: (b, 0, 0))],
        out_specs=pl.BlockSpec((1, 1, NSA_HEADS, HD), lambda b, g, i_, p_: (b, g, 0, 0)),
    )
    return pl.pallas_call(
        functools.partial(_sample_sel_body, NP=NP, WB=WB),
        grid_spec=grid_spec,
        out_shape=jax.ShapeDtypeStruct((DB, NSA_KV, NSA_HEADS, HD), F32),
        compiler_params=_cp("arbitrary", "arbitrary"),
        name="sample_sel_win",
    )(idx_flat, page_table.reshape(-1), slopes, *([pool] * SEL_TOPK), win, q8, newkv, oc, sm)


def _mlstm_step_body(bif_ref, gs_ref, ms_ref, qkv_ref, mo_ref, ng_ref, c_ref, n_ref,
                     o_ref, co_ref, no_ref, mo_out_ref):
    b = pl.program_id(0)
    H = M_HEADS
    ri = lax.broadcasted_iota(I32, (HD, HD), 0)
    ci = lax.broadcasted_iota(I32, (HD, HD), 1)
    for h in range(H):
        ig = jnp.full((1, HD), gs_ref[b, h] + bif_ref[0, h], F32)
        fp = jnp.full((1, HD), gs_ref[b, H + h] + bif_ref[1, h], F32)
        m = jnp.full((1, HD), ms_ref[b, h], F32)
        lf = -_softplus(-fp)
        inter = lf + m
        m_t = jnp.maximum(inter, ig)
        w_intra = jnp.exp(ig - m_t)
        w_inter = jnp.exp(inter - m_t)
        q = qkv_ref[0, h:h + 1, :]
        k = qkv_ref[0, H + h:H + h + 1, :] * (HD ** -0.5)
        v = qkv_ref[0, 2 * H + h:2 * H + h + 1, :]
        C = c_ref[0, h]
        n = n_ref[0, h:h + 1, :]
        s = jnp.sum(q * k, axis=1, keepdims=True) * w_intra
        qC = _dot(jnp.broadcast_to(q, (8, HD)).astype(BF16), C.astype(BF16))[0:1]
        num = s * v + w_inter * qC
        den = s + w_inter * jnp.sum(q * n, axis=1, keepdims=True)
        hh = num / jnp.maximum(jnp.abs(den), jnp.exp(-m_t))
        k_col = jnp.sum(jnp.where(ri == ci, jnp.broadcast_to(k, (HD, HD)), 0.0), axis=1, keepdims=True)
        co_ref[0, h] = w_inter * C + (k_col * w_intra) * v
        no_ref[0, h:h + 1, :] = w_inter * n + w_intra * k
        mo_out_ref[0, h:h + 1, :] = m_t
        mu = jnp.mean(hh, axis=1, keepdims=True)
        hc = hh - mu
        var = jnp.mean(hc * hc, axis=1, keepdims=True)
        hn = hc * lax.rsqrt(var + LN_EPS) * ng_ref[h:h + 1, :]
        o_ref[0, h:h + 1, :] = _sigmoid(mo_ref[0, h:h + 1, :]) * hn


def mlstm_step(b_if, gates, m_state, qkv, mo, norm_g, C, n):
    DB = qkv.shape[0]
    H = M_HEADS
    smem = pl.BlockSpec(memory_space=pltpu.SMEM)
    row = pl.BlockSpec((1, H, HD), lambda b: (b, 0, 0))
    cspec = pl.BlockSpec((1, H, HD, HD), lambda b: (b, 0, 0, 0))
    return pl.pallas_call(
        _mlstm_step_body,
        grid=(DB,),
        in_specs=[smem, smem, smem, pl.BlockSpec((1, 3 * H, HD), lambda b: (b, 0, 0)), row,
                  pl.BlockSpec((H, HD), lambda b: (0, 0)), cspec, row],
        out_specs=[row, cspec, row, row],
        out_shape=[jax.ShapeDtypeStruct((DB, H, HD), F32), jax.ShapeDtypeStruct((DB, H, HD, HD), F32),
                   jax.ShapeDtypeStruct((DB, H, HD), F32), jax.ShapeDtypeStruct((DB, H, HD), F32)],
        compiler_params=_cp("arbitrary"),
        name="mlstm_step",
    )(b_if, gates, m_state, qkv, mo, norm_g.reshape(H, HD), C, n)


def _rglru_step_body(rx_ref, rgt_ref, buf_ref, h_ref, cw_ref, vec_ref, wa_ref, wx_ref, o_ref, ho_ref):
    cw = cw_ref[...]
    vec = vec_ref[...]
    xc = vec[0:1] + rx_ref[...] * cw[CONV_W - 1:CONV_W]
    for j in range(CONV_W - 1):
        xc = xc + buf_ref[j] * cw[j:j + 1]
    a, u = _rglru_gates(xc, wa_ref[0].astype(BF16), wx_ref[0].astype(BF16), vec)
    h = a * h_ref[...] + u
    ho_ref[...] = h
    o_ref[...] = h * _gelu(rgt_ref[...])


def rglru_step(rx, rgt, buf_t, h0, conv_w, vec, w_a, w_x):
    DB = rx.shape[0]
    col = pl.BlockSpec((DB, 128), lambda n: (0, n))
    return pl.pallas_call(
        _rglru_step_body,
        grid=(R_BLOCKS,),
        in_specs=[col, col, pl.BlockSpec((CONV_W - 1, DB, 128), lambda n: (0, 0, n)), col,
                  pl.BlockSpec((CONV_W, 128), lambda n: (0, n)),
                  pl.BlockSpec((4, 128), lambda n: (0, n)),
                  pl.BlockSpec((1, 128, 128), lambda n: (n, 0, 0)),
                  pl.BlockSpec((1, 128, 128), lambda n: (n, 0, 0))],
        out_specs=[col, col],
        out_shape=[jax.ShapeDtypeStruct((DB, R_WIDTH), F32), jax.ShapeDtypeStruct((DB, R_WIDTH), F32)],
        compiler_params=_cp("arbitrary"),
        name="rglru_step",
    )(rx, rgt, buf_t, h0, conv_w, vec, w_a, w_x)


def pack_w_in(w):
    D = w.shape[0]
    parts = [w[:, 8744:14888], w[:, 0:2560], w[:, 2584:5656], w[:, 5672:6696], w[:, 6696:7720],
             w[:, 7720:8744], w[:, 2560:2584], w[:, 5656:5672], jnp.zeros((D, N_AL - 14888), w.dtype)]
    return jnp.concatenate(parts, axis=1).astype(BF16)


def prompt_layer(xp, lw):
    B, T, D = xp.shape
    x2 = xp.reshape(B * T, D)
    M = B * T
    z = mm(x2, lw["w_in"], tm=min(1024, M), tn=1152, out_dtype=F32)
    z3 = z.reshape(B, T, N_AL)
    kc, vc, cb = compress_from_z(z3, lw["phi_w1"], lw["pe"], lw["phi_b1"], lw["phi_w2"])
    o_a = nsa_prompt(z3, kc, vc, lw["slopes"])
    o_b, Cp, np_, mp = mlstm_prompt(z3, lw["b_if"], lw["norm_g"])
    o_c, hp = rglru_prompt(z3, lw["conv_w"], lw["rg_vec"], lw["w_a"], lw["w_x"])
    merged = merge(o_a.reshape(B * T, -1), o_b.reshape(B * T, -1), o_c.reshape(B * T, -1), z,
                   lw["w_branch"], tm=min(512, M), tn=1024, out_dtype=BF16)
    h = mm_res_ln(merged, lw["w_out"], x2, lw["ln_g"][0:1], lw["ln_b"][0:1], tm=min(256, M), tk=D)
    f1 = mm(h, lw["mlp_w1"], tm=min(1024, M), tn=1024, out_dtype=BF16, act="relu2")
    x_new = mm_res_ln(f1, lw["mlp_w2"], h, lw["ln_g"][1:2], lw["ln_b"][1:2], tm=min(512, M), tk=1024)
    kv = z3[:, :, OFF_KV:OFF_KV + 6 * NSA_KV * HD].reshape(B, T, 6, NSA_KV, HD)
    n_win = min(WINDOW, T)
    states = (kv[:, :, :4], kv[:, T - n_win:, 4:6], Cp, np_[:, :, 0], mp[:, :, 0, 0], hp[:, 0],
              z3[:, T - (CONV_W - 1):, OFF_RX:OFF_RX + R_WIDTH])
    return x_new.reshape(B, T, D), states, cb


def sample_layer(xs, lw, cb, layer, page_table, pool, n_pool, win, win_l, C0, n0, m0, h0, conv0):
    DB, D = xs.shape
    z = mm(xs, lw["w_in"], tm=DB, tn=1152, out_dtype=F32)
    q8 = z[:, OFF_Q:OFF_Q + NSA_WIDTH].reshape(DB, NSA_HEADS, HD)
    newkv = z[:, OFF_KV:OFF_KV + 6 * NSA_KV * HD].reshape(DB, 6 * NSA_KV, HD)
    sm = z[:, OFF_SM:OFF_SM + 128].reshape(DB, 1, 128)
    oc, idx = sample_cmp_select(page_table, pool, layer, n_pool, lw["w1c"], cb, lw["phi_w2"], q8,
                                lw["slopes"])
    osw = sample_sel_win(idx, page_table, pool, win, layer, n_pool, q8, newkv, oc, sm, lw["slopes"])
    o_a = jnp.concatenate([osw[:, g, g * NSA_GROUP:(g + 1) * NSA_GROUP] for g in range(NSA_KV)],
                          axis=1).reshape(DB, NSA_WIDTH)
    qkv = z[:, OFF_MQKV:OFF_MQKV + 3 * M_WIDTH].reshape(DB, 3 * M_HEADS, HD)
    mo = z[:, OFF_MO:OFF_MO + M_WIDTH].reshape(DB, M_HEADS, HD)
    gates = z[:, OFF_SM + SM_I:OFF_SM + SM_I + 2 * M_HEADS]
    o_b, Cs, ns, ms = mlstm_step(lw["b_if"], gates, m0, qkv, mo, lw["norm_g"], C0, n0)
    rx = z[:, OFF_RX:OFF_RX + R_WIDTH]
    o_c, hs = rglru_step(rx, z[:, OFF_RGT:OFF_RGT + R_WIDTH], conv0.transpose(1, 0, 2), h0,
                         lw["conv_w"], lw["rg_vec"], lw["w_a"], lw["w_x"])
    merged = merge(o_a, o_b.reshape(DB, M_WIDTH), o_c, z, lw["w_branch"], tm=DB, tn=1024, out_dtype=F32)
    h = mm_res_ln(merged, lw["w_out"], xs, lw["ln_g"][0:1], lw["ln_b"][0:1], tm=DB, tk=D)
    f1 = mm(h, lw["mlp_w1"], tm=DB, tn=1024, out_dtype=F32, act="relu2")
    x_new = mm_res_ln(f1, lw["mlp_w2"], h, lw["ln_g"][1:2], lw["ln_b"][1:2], tm=DB, tk=1024)
    kvn = newkv.reshape(DB, 1, 6, NSA_KV, HD)
    Wb = win_l.shape[1]
    keep = min(WINDOW, Wb + 1)
    win_new = jnp.concatenate([win_l, kvn[:, :, 4:6]], axis=1)[:, Wb + 1 - keep:]
    conv_new = jnp.concatenate([conv0[:, 1:], rx[:, None]], axis=1)
    states = (kvn[:, :, :4], win_new, Cs, ns, ms[:, :, 0], hs, conv_new)
    return x_new, states


def kernel(x_prompt, x_sample, cache_nsa_kv, cache_win_kv, state_mlstm_C, state_mlstm_n, state_mlstm_m,
           state_rglru_h, state_rglru_conv, page_table, w_in, nsa_pe, nsa_phi_w1, nsa_phi_b1, nsa_phi_w2,
           mlstm_b_if, mlstm_norm_g, rg_conv_w, rg_conv_b, rg_w_a, rg_b_a, rg_w_x, rg_b_x, rg_lambda,
           w_branch, w_out, ln_g, ln_b, mlp_w1, mlp_w2):
    DB, Tn, D = x_sample.shape
    assert Tn == 1 and D == D_MODEL
    depth, n_pool = cache_nsa_kv.shape[:2]
    pool = cache_nsa_kv.reshape(depth * n_pool, ROWS_PER_PAGE, HD)
    Wb = cache_win_kv.shape[2]
    win = cache_win_kv.reshape(depth * DB, Wb * 2 * NSA_KV, HD)
    slopes = alibi_slopes()
    half = CMP_STRIDE * HD
    xp = x_prompt
    xs = x_sample.reshape(DB, D)
    new_p = [[] for _ in range(7)]
    new_s = [[] for _ in range(7)]
    for l in range(depth):
        lw = {
            "w_in": pack_w_in(w_in[l]),
            "pe": nsa_pe[l], "phi_w1": nsa_phi_w1[l], "phi_b1": nsa_phi_b1[l], "phi_w2": nsa_phi_w2[l],
            "w1c": jnp.concatenate([nsa_phi_w1[l][:, :half], nsa_phi_w1[l][:, half:]], axis=2).astype(BF16),
            "slopes": slopes, "b_if": mlstm_b_if[l], "norm_g": mlstm_norm_g[l],
            "conv_w": rg_conv_w[l],
            "rg_vec": jnp.stack([rg_conv_b[l], rg_b_a[l], rg_b_x[l], rg_lambda[l]]),
            "w_a": rg_w_a[l], "w_x": rg_w_x[l],
            "w_branch": w_branch[l].astype(BF16), "w_out": w_out[l].astype(BF16),
            "ln_g": ln_g[l], "ln_b": ln_b[l],
            "mlp_w1": mlp_w1[l].astype(BF16), "mlp_w2": mlp_w2[l].astype(BF16),
        }
        xp, st_p, cb = prompt_layer(xp, lw)
        xs, st_s = sample_layer(xs, lw, cb, l, page_table, pool, n_pool, win, cache_win_kv[l],
                                state_mlstm_C[l], state_mlstm_n[l], state_mlstm_m[l],
                                state_rglru_h[l], state_rglru_conv[l])
        for lst, val in zip(new_p, st_p):
            lst.append(val)
        for lst, val in zip(new_s, st_s):
            lst.append(val)
    P = [jnp.stack(a) for a in new_p]
    S = [jnp.stack(a) for a in new_s]
    return (xp, xs.reshape(DB, Tn, D), P[0], S[0], P[1], S[1], P[2], S[2], P[3], S[3], P[4], S[4],
            P[5], S[5], P[6], S[6])
```

```python
import functools

import jax
import jax.numpy as jnp
from jax import lax
from jax.experimental import pallas as pl
from jax.experimental.pallas import tpu as pltpu

F32 = jnp.float32
BF16 = jnp.bfloat16
I32 = jnp.int32

D_MODEL = 2048
DEPTH = 4
PAGE_SIZE = 128
HD = 128
NSA_HEADS = 8
NSA_KV = 2
NSA_GROUP = 4
NSA_WIDTH = NSA_HEADS * HD
CMP_BLOCK = 32
CMP_STRIDE = 16
SEL_BLOCK = 64
SEL_TOPK = 16
WINDOW = 512
Q_BLOCK = 128
FORCE_BONUS = 1.0e4
M_HEADS = 8
M_WIDTH = M_HEADS * HD
R_WIDTH = 1024
R_BLOCKS = 8
CONV_W = 4
LRU_C = 8.0
D_FF = 4 * D_MODEL
DEEPNORM_ALPHA = (2 * DEPTH) ** 0.25
LN_EPS = 1e-5
NEG_INF = -1e30

OFF_MG = 0
OFF_Q = 6144
OFF_MQKV = 7168
OFF_MO = 10240
OFF_RX = 11264
OFF_RGT = 12288
OFF_KV = 13312
OFF_SM = 14848
N_AL = 14976
SM_I = 24
SM_F = 32

VMEM_LIMIT = 52 * 1024 * 1024


def _cp(*sem):
    return pltpu.CompilerParams(dimension_semantics=sem, vmem_limit_bytes=VMEM_LIMIT)


def _dot(a, b):
    return jnp.dot(a, b, preferred_element_type=F32)


def _dot_nt(a, b):
    return lax.dot_general(a, b, (((1,), (1,)), ((), ())), preferred_element_type=F32)


def _gelu(x):
    return 0.5 * x * (1.0 + jnp.tanh(0.7978845608028654 * (x + 0.044715 * (x * x * x))))


def _sigmoid(x):
    return 0.5 * jnp.tanh(0.5 * x) + 0.5


def _softplus(x):
    return jnp.maximum(x, 0.0) + jnp.log1p(jnp.exp(-jnp.abs(x)))


def _mm_body(a_ref, b_ref, o_ref, *, act):
    acc = _dot(a_ref[...].astype(BF16), b_ref[...])
    if act == "relu2":
        acc = jnp.square(jnp.maximum(acc, 0.0))
    o_ref[...] = acc.astype(o_ref.dtype)


def mm(a, b, layer, *, tm, tn, out_dtype, act=None):
    M, K = a.shape
    N = b.shape[2]
    assert M % tm == 0 and N % tn == 0
    return pl.pallas_call(
        functools.partial(_mm_body, act=act),
        grid=(M // tm, N // tn),
        in_specs=[pl.BlockSpec((tm, K), lambda i, j: (i, 0)),
                  pl.BlockSpec((None, K, tn), lambda i, j: (layer, 0, j))],
        out_specs=pl.BlockSpec((tm, tn), lambda i, j: (i, j)),
        out_shape=jax.ShapeDtypeStruct((M, N), out_dtype),
        compiler_params=_cp("parallel", "arbitrary"),
        name="mm",
    )(a, b)


LN_SUB_BLOCKS = 4


def _mm_ln_body(a_ref, b_ref, x_ref, g_ref, bb_ref, o_ref, acc_ref, *, nk):
    k = pl.program_id(1)

    if nk > 1:
        @pl.when(k == 0)
        def _():
            acc_ref[...] = _dot(a_ref[...].astype(BF16), b_ref[...])

        @pl.when((k > 0) & (k < nk - 1))
        def _():
            acc_ref[...] += _dot(a_ref[...].astype(BF16), b_ref[...])

    @pl.when(k == nk - 1)
    def _():
        tm = a_ref.shape[0]
        sb = tm // LN_SUB_BLOCKS if tm % (8 * LN_SUB_BLOCKS) == 0 else tm
        for r0 in range(0, tm, sb):
            rows = slice(r0, r0 + sb)
            y = DEEPNORM_ALPHA * x_ref[rows, :] + _dot(a_ref[rows, :].astype(BF16), b_ref[...])
            if nk > 1:
                y = y + acc_ref[rows, :]
            mu = jnp.mean(y, axis=-1, keepdims=True)
            yc = y - mu
            var = jnp.mean(yc * yc, axis=-1, keepdims=True)
            o_ref[rows, :] = yc * lax.rsqrt(var + LN_EPS) * g_ref[...] + bb_ref[...]


def mm_res_ln(a, b, layer, x, g, bb, *, tm, tk):
    M, K = a.shape
    N = b.shape[2]
    nk = K // tk
    assert M % tm == 0 and K % tk == 0
    return pl.pallas_call(
        functools.partial(_mm_ln_body, nk=nk),
        grid=(M // tm, nk),
        in_specs=[pl.BlockSpec((tm, tk), lambda i, k: (i, k)),
                  pl.BlockSpec((None, tk, N), lambda i, k: (layer, k, 0)),
                  pl.BlockSpec((tm, N), lambda i, k: (i, 0)),
                  pl.BlockSpec((1, N), lambda i, k: (0, 0)),
                  pl.BlockSpec((1, N), lambda i, k: (0, 0))],
        out_specs=pl.BlockSpec((tm, N), lambda i, k: (i, 0)),
        out_shape=jax.ShapeDtypeStruct((M, N), F32),
        scratch_shapes=[pltpu.VMEM((tm, N), F32)],
        compiler_params=_cp("parallel", "arbitrary"),
        name="mm_res_ln",
    )(a, b, x, g, bb)


def _merge_body(oa_ref, ob_ref, oc_ref, g0_ref, g1_ref, g2_ref, w_ref, o_ref):
    tm = oa_ref.shape[0]
    sb = tm // LN_SUB_BLOCKS if tm % (16 * LN_SUB_BLOCKS) == 0 else tm
    for r0 in range(0, tm, sb):
        rows = slice(r0, r0 + sb)
        acc = _sigmoid(g0_ref[rows, :]) * _dot(oa_ref[rows, :].astype(BF16), w_ref[0])
        acc += _sigmoid(g1_ref[rows, :]) * _dot(ob_ref[rows, :].astype(BF16), w_ref[1])
        acc += _sigmoid(g2_ref[rows, :]) * _dot(oc_ref[rows, :].astype(BF16), w_ref[2])
        o_ref[rows, :] = acc.astype(o_ref.dtype)


def merge(o_a, o_b, o_c, z, w_branch, layer, *, tm, tn, out_dtype):
    M, W = o_a.shape
    N = w_branch.shape[3]
    assert OFF_MG % tn == 0 and N % tn == 0 and M % tm == 0
    gb = OFF_MG // tn
    nb = N // tn
    br = pl.BlockSpec((tm, W), lambda i, j: (i, 0))

    def gspec(k):
        return pl.BlockSpec((tm, tn), lambda i, j: (i, gb + k * nb + j))

    return pl.pallas_call(
        _merge_body,
        grid=(M // tm, nb),
        in_specs=[br, br, br, gspec(0), gspec(1), gspec(2),
                  pl.BlockSpec((None, 3, W, tn), lambda i, j: (layer, 0, 0, j))],
        out_specs=pl.BlockSpec((tm, tn), lambda i, j: (i, j)),
        out_shape=jax.ShapeDtypeStruct((M, N), out_dtype),
        compiler_params=_cp("parallel", "arbitrary"),
        name="merge",
    )(o_a, o_b, o_c, z, z, z, w_branch)


def _kv_rows_body(z_ref, *rest):
    o_ref = rest[-1]
    tm = z_ref.shape[0]
    for kg in range(4 * NSA_KV):
        o_ref[pl.ds(kg, tm, stride=4 * NSA_KV), :] = z_ref[:, kg * HD:(kg + 1) * HD]


def kv_rows(z, buf, layer, depth):
    M = z.shape[0]
    tm = min(512, M)
    nb = M // tm
    W = 4 * NSA_KV * HD
    assert OFF_KV % W == 0 and M % tm == 0
    in_specs = [pl.BlockSpec((tm, W), lambda i: (i, OFF_KV // W))]
    args = [z]
    aliases = {}
    if buf is not None:
        in_specs.append(pl.BlockSpec(memory_space=pl.ANY))
        args.append(buf)
        aliases = {1: 0}
    return pl.pallas_call(
        _kv_rows_body,
        grid=(nb,),
        in_specs=in_specs,
        out_specs=pl.BlockSpec((tm * 4 * NSA_KV, HD), lambda i: (layer * nb + i, 0)),
        out_shape=jax.ShapeDtypeStruct((depth * M * 4 * NSA_KV, HD), F32),
        input_output_aliases=aliases,
        compiler_params=_cp("arbitrary"),
        name="kv_rows",
    )(*args)


def _compress_body(x_ref, w1_ref, pe_ref, b1_ref, w2_ref, o_ref, cb_ref):
    half = CMP_STRIDE * HD
    x = x_ref[0, 0].astype(BF16)
    w1 = w1_ref[0].astype(BF16)
    lead = _dot(x, w1[:half])
    trail = _dot(x, w1[half:])
    n = x.shape[0]
    pe8 = jnp.broadcast_to(pe_ref[0], (8, 2 * half)).astype(BF16)
    cb = _dot(pe8, w1) + b1_ref[0]
    trail_next = pltpu.roll(trail, n - 1, axis=0)
    hid = _gelu(lead + trail_next + cb[0:1])
    o_ref[0, 0] = _dot(hid.astype(BF16), w2_ref[0].astype(BF16))
    cb_ref[0] = cb


def compress_prompt(x, w1, pe, b1, w2):
    _, S, n, K = x.shape
    return pl.pallas_call(
        _compress_body,
        grid=(2, S),
        in_specs=[pl.BlockSpec((1, 1, n, K), lambda k, s: (k, s, 0, 0)),
                  pl.BlockSpec((1, 2 * K, HD), lambda k, s: (k, 0, 0)),
                  pl.BlockSpec((1, 1, 2 * K), lambda k, s: (k, 0, 0)),
                  pl.BlockSpec((1, 1, HD), lambda k, s: (k, 0, 0)),
                  pl.BlockSpec((1, HD, HD), lambda k, s: (k, 0, 0))],
        out_specs=[pl.BlockSpec((1, 1, n, HD), lambda k, s: (k, s, 0, 0)),
                   pl.BlockSpec((1, 8, HD), lambda k, s: (k, 0, 0))],
        out_shape=[jax.ShapeDtypeStruct((2, S, n, HD), F32),
                   jax.ShapeDtypeStruct((2, 8, HD), F32)],
        compiler_params=_cp("arbitrary", "arbitrary"),
        name="compress_prompt",
    )(x, w1, pe, b1, w2)


def compress_from_z(z3, w1, pe, b1, w2):
    B, T, _ = z3.shape
    G = NSA_KV
    n = T // CMP_STRIDE
    x = z3[:, :, OFF_KV:OFF_KV + 2 * G * HD].reshape(B, n, CMP_STRIDE, 2, G, HD)
    x = x.transpose(3, 0, 4, 1, 2, 5).reshape(2, B * G, n, CMP_STRIDE * HD)
    out, cb = compress_prompt(x, w1, pe.reshape(2, 1, CMP_BLOCK * HD), b1.reshape(2, 1, HD), w2)
    return out[0], out[1], cb


def alibi_slopes():
    return jnp.asarray([2.0 ** (-8.0 * (h + 1) / NSA_HEADS) for h in range(NSA_HEADS)], F32)


SEL_CHUNK = 512


def _online_update(carry, s, mask, v):
    m, l, acc = carry
    s = jnp.where(mask, s, NEG_INF)
    m_new = jnp.maximum(m, jnp.max(s, axis=1, keepdims=True))
    alpha = jnp.exp(m - m_new)
    p = jnp.where(mask, jnp.exp(s - m_new), 0.0)
    l = alpha * l + jnp.sum(p, axis=1, keepdims=True)
    acc = alpha * acc + _dot(p.astype(BF16), v)
    return m_new, l, acc


MASK_BIG = 1.0e30
LOG2E = 1.4426950408889634
WIN_KEYS = WINDOW + Q_BLOCK


def _nsa_prompt_body(slopes_ref, q_ref, ks_ref, vs_ref, kw_ref, vw_ref, kc_ref, vc_ref,
                     sm_ref, o_ref, ks_bf, vs_aug, et_bf, kw_pad, vw_aug, wbias, sbias, y_a, y_b,
                     *, T):
    assert T % (2 * SEL_CHUNK) == 0
    g = pl.program_id(1)
    i = pl.program_id(2)
    QB = Q_BLOCK
    R = NSA_GROUP
    RQ = R * QB
    NCP = T // CMP_STRIDE
    NC = NCP - 1
    NS = T // SEL_BLOCK
    NSP = 128
    assert NS <= NSP
    t0 = i * QB
    scale = HD ** -0.5
    c2 = scale * LOG2E

    @pl.when(i == 0)
    def _():
        lane = lax.broadcasted_iota(I32, (T, HD), 1)
        ones_col = jnp.where(lane == 0, 1.0, 0.0).astype(BF16)
        ks_bf[...] = ks_ref[0].astype(BF16)
        vs_aug[:, 0:HD] = vs_ref[0].astype(BF16)
        vs_aug[:, HD:2 * HD] = ones_col
        kk = lax.broadcasted_iota(I32, (T, HD), 0)
        et_bf[...] = jnp.where(kk // SEL_BLOCK == lane, MASK_BIG, 0.0).astype(BF16)
        kw_pad[0:WINDOW, :] = jnp.zeros((WINDOW, HD), BF16)
        kw_pad[WINDOW:WINDOW + T, :] = kw_ref[0].astype(BF16)
        vw_aug[0:WINDOW, :] = jnp.zeros((WINDOW, 2 * HD), BF16)
        vw_aug[WINDOW:WINDOW + T, 0:HD] = vw_ref[0].astype(BF16)
        vw_aug[WINDOW:WINDOW + T, HD:2 * HD] = ones_col
        wd = (lax.broadcasted_iota(I32, (QB, WIN_KEYS), 0) + WINDOW
              - lax.broadcasted_iota(I32, (QB, WIN_KEYS), 1))
        band = (wd >= 0) & (wd < WINDOW)
        wdf = wd.astype(F32)
        lf = lax.broadcasted_iota(I32, (QB, SEL_CHUNK), 1).astype(F32)
        for r in range(R):
            sl = slopes_ref[g * R + r]
            wbias[r * QB:(r + 1) * QB, :] = jnp.where(band, (-LOG2E * sl) * wdf, -MASK_BIG)
            sbias[r * QB:(r + 1) * QB, :] = (LOG2E * sl) * lf

    qb = q_ref[0]
    q4 = jnp.concatenate([qb[:, r * HD:(r + 1) * HD] for r in range(R)], axis=0).astype(BF16)
    row = lax.broadcasted_iota(I32, (RQ, 1), 0)
    qpos = t0 + (row & (QB - 1))
    slope = jnp.concatenate(
        [jnp.full((QB, 1), slopes_ref[g * R + r], F32) for r in range(R)], axis=0)

    kc = kc_ref[0, 0].astype(BF16)
    vc = vc_ref[0, 0].astype(BF16)
    n_idx = lax.broadcasted_iota(I32, (1, NCP), 1)
    dist = qpos - (n_idx * CMP_STRIDE + (CMP_BLOCK - 1))
    mask = (dist >= 0) & (n_idx < NC)
    s = _dot_nt(q4, kc) * scale - slope * dist.astype(F32)
    s = jnp.where(mask, s, NEG_INF)
    m = jnp.max(s, axis=1, keepdims=True)
    p = jnp.where(mask, jnp.exp(s - m), 0.0)
    l = jnp.sum(p, axis=1, keepdims=True)
    p = p * (1.0 / jnp.where(l > 0.0, l, 1.0))
    o_c = _dot(p.astype(BF16), vc)

    psum = p[0:QB]
    for r in range(1, R):
        psum = psum + p[r * QB:(r + 1) * QB]
    p_hi = psum.astype(BF16)
    p_lo = (psum - p_hi.astype(F32)).astype(BF16)
    jo = lax.broadcasted_iota(I32, (NS, NCP), 0) * SEL_BLOCK
    no = lax.broadcasted_iota(I32, (NS, NCP), 1)
    ov = ((no * CMP_STRIDE < jo + SEL_BLOCK) & (no * CMP_STRIDE + CMP_BLOCK > jo)
          & (no < NC)).astype(BF16)
    imp = _dot_nt(ov, p_hi) + _dot_nt(ov, p_lo)
    jj = lax.broadcasted_iota(I32, (NS, QB), 0)
    qp = t0 + lax.broadcasted_iota(I32, (NS, QB), 1)
    cur = qp // SEL_BLOCK
    forced = (jj == 0) | (jj == cur) | (jj == cur - 1)
    imp = jnp.where(forced, imp + FORCE_BONUS, imp)
    imp = jnp.where(jj * SEL_BLOCK <= qp, imp, -1.0)
    ranks = [jnp.zeros((8, QB), F32) for _ in range(NS // 8)]
    for j2 in range(NS):
        rv = imp[j2:j2 + 1, :]
        for v in range(NS // 8):
            blk = imp[8 * v:8 * v + 8]
            if 8 * v > j2:
                beats = rv >= blk
            elif 8 * v + 7 < j2:
                beats = rv > blk
            else:
                beats = (rv > blk) | ((rv == blk) & (jj[8 * v:8 * v + 8] > j2))
            ranks[v] = ranks[v] + jnp.where(beats, 1.0, 0.0)
    rank = jnp.concatenate(ranks, axis=0)
    selm_t = jnp.where(rank < SEL_TOPK, 0.0, -1.0)
    if NSP > NS:
        selm_t = jnp.concatenate([selm_t, jnp.zeros((NSP - NS, QB), F32)], axis=0)
    selm = selm_t.T.astype(BF16)

    sl_col = LOG2E * slope
    rel = (lax.broadcasted_iota(I32, (QB, SEL_CHUNK), 1)
           - lax.broadcasted_iota(I32, (QB, SEL_CHUNK), 0))

    def scores(c):
        k0 = pl.multiple_of(c * SEL_CHUNK, SEL_CHUNK)
        mb = _dot_nt(selm, et_bf[pl.ds(k0, SEL_CHUNK), :])
        mb = jnp.where(rel <= t0 - k0, mb, -MASK_BIG)
        return (_dot_nt(q4, ks_bf[pl.ds(k0, SEL_CHUNK), :]) * c2 + sbias[...]
                + jnp.concatenate([mb] * R, axis=0))

    def consume(c, y, carry):
        m, acc = carry
        k0 = pl.multiple_of(c * SEL_CHUNK, SEL_CHUNK)
        off = sl_col * (k0 - t0).astype(F32)
        m_new = jnp.maximum(m, jnp.max(y, axis=1, keepdims=True) + off)
        p = jnp.exp2(y - (m_new - off))
        acc = jnp.exp2(m - m_new) * acc + _dot(p.astype(BF16), vs_aug[pl.ds(k0, SEL_CHUNK), :])
        return m_new, acc

    last_chunk = T // SEL_CHUNK - 1

    def pair(pi, carry):
        c0 = 2 * pi
        y_b[...] = scores(c0 + 1)
        carry = consume(c0, y_a[...], carry)
        y_a[...] = scores(jnp.minimum(c0 + 2, last_chunk))
        return consume(c0 + 1, y_b[...], carry)

    init = (jnp.full((RQ, 1), -MASK_BIG, F32), jnp.zeros((RQ, 2 * HD), F32))
    n_chunks = (t0 + QB + SEL_CHUNK - 1) // SEL_CHUNK
    y_a[...] = scores(0)
    _, acc_s = lax.fori_loop(0, (n_chunks + 1) // 2, pair, init)
    o_s = acc_s[:, 0:HD] * (1.0 / acc_s[:, HD:HD + 1])

    w0 = pl.multiple_of(t0, QB)
    vrow = jnp.where(lax.broadcasted_iota(I32, (1, WIN_KEYS), 1) >= WINDOW - t0, 0.0, -MASK_BIG)
    y = _dot_nt(q4, kw_pad[pl.ds(w0, WIN_KEYS), :]) * c2 + wbias[...] + vrow
    p = jnp.exp2(y - jnp.max(y, axis=1, keepdims=True))
    acc_w = _dot(p.astype(BF16), vw_aug[pl.ds(w0, WIN_KEYS), :])
    o_w = acc_w[:, 0:HD] * (1.0 / acc_w[:, HD:HD + 1])

    smb = sm_ref[0]
    lane = lax.broadcasted_iota(I32, (QB, 128), 1)

    def gate(c):
        cols = []
        for r in range(R):
            idx = (g * R + r) * 3 + c
            cols.append(jnp.sum(jnp.where(lane == idx, smb, 0.0), axis=1, keepdims=True))
        return _sigmoid(jnp.concatenate(cols, axis=0))

    o = gate(0) * o_c + gate(1) * o_s + gate(2) * o_w
    o_ref[0] = jnp.concatenate([o[r * QB:(r + 1) * QB] for r in range(R)], axis=1)


def nsa_prompt(z3, kc, vc, slopes):
    B, T, _ = z3.shape
    G = NSA_KV
    kvb = OFF_KV // HD

    def kvspec(kind):
        return pl.BlockSpec((1, T, HD), lambda b, g, i: (b, 0, kvb + kind * G + g))

    cspec = pl.BlockSpec((1, 1, T // CMP_STRIDE, HD), lambda b, g, i: (b * G + g, 0, 0, 0))
    GW = NSA_GROUP * HD
    assert OFF_Q % GW == 0
    qspec = pl.BlockSpec((1, Q_BLOCK, GW), lambda b, g, i: (b, i, OFF_Q // GW + g))
    return pl.pallas_call(
        functools.partial(_nsa_prompt_body, T=T),
        grid=(B, G, T // Q_BLOCK),
        in_specs=[pl.BlockSpec(memory_space=pltpu.SMEM),
                  qspec, kvspec(2), kvspec(3), kvspec(4), kvspec(5), cspec, cspec,
                  pl.BlockSpec((1, Q_BLOCK, 128), lambda b, g, i: (b, i, OFF_SM // 128))],
        out_specs=pl.BlockSpec((1, Q_BLOCK, GW), lambda b, g, i: (b, i, g)),
        out_shape=jax.ShapeDtypeStruct((B, T, NSA_WIDTH), F32),
        scratch_shapes=[pltpu.VMEM((T, HD), BF16),
                        pltpu.VMEM((T, 2 * HD), BF16),
                        pltpu.VMEM((T, HD), BF16),
                        pltpu.VMEM((T + WINDOW, HD), BF16),
                        pltpu.VMEM((T + WINDOW, 2 * HD), BF16),
                        pltpu.VMEM((NSA_GROUP * Q_BLOCK, WIN_KEYS), F32),
                        pltpu.VMEM((NSA_GROUP * Q_BLOCK, SEL_CHUNK), F32),
                        pltpu.VMEM((NSA_GROUP * Q_BLOCK, SEL_CHUNK), F32),
                        pltpu.VMEM((NSA_GROUP * Q_BLOCK, SEL_CHUNK), F32)],
        compiler_params=_cp("arbitrary", "arbitrary", "arbitrary"),
        name="nsa_prompt",
    )(slopes, z3, z3, z3, z3, z3, kc[:, None], vc[:, None], z3)


M_CHUNK_K = 128


M_HEADS_PER_STEP = 4
M_ROWS_PER_STEP = 512


def _mlstm_body(bif_ref, q_ref, k_ref, v_ref, sm_ref, mo_ref, ng_ref,
                o_ref, c_ref, n_ref, m_ref, c_sc, n_sc, m_sc, *, NT, TB):
    hb = pl.program_id(1)
    t = pl.program_id(2)
    L = M_CHUNK_K
    HB = M_HEADS_PER_STEP
    lane = lax.broadcasted_iota(I32, (L, 128), 1)
    ri = lax.broadcasted_iota(I32, (L, L), 0)
    ci = lax.broadcasted_iota(I32, (L, L), 1)
    eye = ri == ci
    tril = ri >= ci

    @pl.when(t == 0)
    def _():
        c_sc[...] = jnp.zeros_like(c_sc)
        n_sc[...] = jnp.zeros_like(n_sc)
        m_sc[...] = jnp.zeros_like(m_sc)

    def head_chunk(hh, r0):
        h = hb * HB + hh
        cs = slice(hh * HD, (hh + 1) * HD)
        bi = bif_ref[0, h]
        bf = bif_ref[1, h]
        ng = ng_ref[:, cs]
        C = c_sc[hh]
        n = n_sc[hh]
        m = m_sc[hh][:, 0:1]
        q = q_ref[0, pl.ds(r0, L), cs]
        k = k_ref[0, pl.ds(r0, L), cs] * (HD ** -0.5)
        v = v_ref[0, pl.ds(r0, L), cs]
        smb = sm_ref[0, pl.ds(r0, L), :]
        ig_col = jnp.sum(jnp.where(lane == SM_I + h, smb, 0.0), axis=1, keepdims=True) + bi
        fp_col = jnp.sum(jnp.where(lane == SM_F + h, smb, 0.0), axis=1, keepdims=True) + bf
        lf_col = -_softplus(-fp_col)
        lf_row = jnp.sum(jnp.where(eye, lf_col, 0.0), axis=0, keepdims=True)
        ig_row = jnp.sum(jnp.where(eye, ig_col, 0.0), axis=0, keepdims=True)
        b_col = jnp.sum(jnp.where(tril, lf_row, 0.0), axis=1, keepdims=True)
        b_row = jnp.sum(jnp.where(ri <= ci, lf_col, 0.0), axis=0, keepdims=True)
        d = jnp.where(tril, b_col - b_row + ig_row, NEG_INF)
        inter = b_col + m
        m_t = jnp.maximum(inter, jnp.max(d, axis=1, keepdims=True))
        w_intra = jnp.exp(d - m_t)
        w_inter = jnp.exp(inter - m_t)
        qb = q.astype(BF16)
        vb = v.astype(BF16)
        s = _dot_nt(qb, k.astype(BF16)) * w_intra
        num = _dot(s.astype(BF16), vb) + w_inter * _dot(qb, C.astype(BF16))
        den = jnp.sum(s, axis=1, keepdims=True) + w_inter * jnp.sum(q * n, axis=1, keepdims=True)
        hh_ = num * (1.0 / jnp.maximum(jnp.abs(den), jnp.exp(-m_t)))
        m_new = m_t[L - 1:L, :]
        b_last = b_col[L - 1:L, :]
        decay = jnp.exp(b_last + m - m_new)
        w_s = jnp.exp(b_last - b_col + ig_col - m_new)
        kw = k * w_s
        c_sc[hh] = decay * C + _dot(kw.T.astype(BF16), vb)
        n_sc[hh] = decay * n + jnp.sum(kw, axis=0, keepdims=True)
        m_sc[hh] = jnp.broadcast_to(m_new, (1, 128))
        mu = jnp.mean(hh_, axis=1, keepdims=True)
        hc = hh_ - mu
        var = jnp.mean(hc * hc, axis=1, keepdims=True)
        hn = hc * lax.rsqrt(var + LN_EPS) * ng
        o_ref[0, pl.ds(r0, L), cs] = _sigmoid(mo_ref[0, pl.ds(r0, L), cs]) * hn

    def sub(c, carry):
        r0 = pl.multiple_of(c * L, L)
        for hh in range(HB):
            head_chunk(hh, r0)
        return carry

    lax.fori_loop(0, TB // L, sub, 0)

    @pl.when(t == NT - 1)
    def _():
        c_ref[0] = c_sc[...]
        n_ref[0] = n_sc[...]
        m_ref[0] = m_sc[...]


def mlstm_prompt(z3, b_if, norm_g):
    B, T, _ = z3.shape
    H = M_HEADS
    HB = M_HEADS_PER_STEP
    TB = min(M_ROWS_PER_STEP, T)
    W = HB * HD
    assert OFF_MQKV % W == 0 and OFF_MO % W == 0 and T % TB == 0

    def hspec(off):
        return pl.BlockSpec((1, TB, W), lambda b, h, t: (b, t, off // W + h))

    return pl.pallas_call(
        functools.partial(_mlstm_body, NT=T // TB, TB=TB),
        grid=(B, H // HB, T // TB),
        in_specs=[pl.BlockSpec(memory_space=pltpu.SMEM),
                  hspec(OFF_MQKV), hspec(OFF_MQKV + M_WIDTH), hspec(OFF_MQKV + 2 * M_WIDTH),
                  pl.BlockSpec((1, TB, 128), lambda b, h, t: (b, t, OFF_SM // 128)),
                  hspec(OFF_MO),
                  pl.BlockSpec((1, W), lambda b, h, t: (0, h))],
        out_specs=[pl.BlockSpec((1, TB, W), lambda b, h, t: (b, t, h)),
                   pl.BlockSpec((1, HB, HD, HD), lambda b, h, t: (b, h, 0, 0)),
                   pl.BlockSpec((1, HB, 1, HD), lambda b, h, t: (b, h, 0, 0)),
                   pl.BlockSpec((1, HB, 1, 128), lambda b, h, t: (b, h, 0, 0))],
        out_shape=[jax.ShapeDtypeStruct((B, T, M_WIDTH), F32),
                   jax.ShapeDtypeStruct((B, H, HD, HD), F32),
                   jax.ShapeDtypeStruct((B, H, 1, HD), F32),
                   jax.ShapeDtypeStruct((B, H, 1, 128), F32)],
        scratch_shapes=[pltpu.VMEM((HB, HD, HD), F32), pltpu.VMEM((HB, 1, HD), F32),
                        pltpu.VMEM((HB, 1, 128), F32)],
        compiler_params=_cp("arbitrary", "arbitrary", "arbitrary"),
        name="mlstm_prompt",
    )(b_if, z3, z3, z3, z3, z3, norm_g.reshape(1, M_WIDTH))


R_CHUNK = 128


def _rglru_gates(xc, wa, wx, vec):
    xb = xc.astype(BF16)
    r = _sigmoid(_dot(xb, wa) + vec[1:2])
    i = _sigmoid(_dot(xb, wx) + vec[2:3])
    log_a = -LRU_C * r * _softplus(-vec[3:4])
    a = jnp.exp(log_a)
    at = -jnp.tanh(log_a)
    u = jnp.sqrt(2.0 * at) * lax.rsqrt(1.0 + at) * (i * xc)
    return a, u


def _rglru_body(rx_ref, rgt_ref, cw_ref, vec_ref, wa_ref, wx_ref, o_ref, hl_ref, xpad_ref, *, T):
    Tc = R_CHUNK
    xpad_ref[0:8, :] = jnp.zeros((8, 128), F32)
    xpad_ref[8:T + 8, :] = rx_ref[0]
    cw = cw_ref[...]
    vec = vec_ref[...]
    wa = wa_ref[0].astype(BF16)
    wx = wx_ref[0].astype(BF16)
    rowmod = lax.broadcasted_iota(I32, (Tc, 128), 0) & 7

    def chunk(c, h):
        r0 = pl.multiple_of(c * Tc, Tc)
        xc = vec[0:1]
        for j in range(CONV_W):
            xc = xc + xpad_ref[pl.ds(r0 + (8 - (CONV_W - 1)) + j, Tc), :] * cw[j:j + 1]
        a, u = _rglru_gates(xc, wa, wx, vec)
        for sft in (1, 2, 4):
            a1 = pltpu.roll(a, sft, axis=0)
            u1 = pltpu.roll(u, sft, axis=0)
            ok = rowmod >= sft
            u = jnp.where(ok, a * u1 + u, u)
            a = jnp.where(ok, a * a1, a)
        hs = []
        for gi in range(Tc // 8):
            hg = a[gi * 8:(gi + 1) * 8] * h + u[gi * 8:(gi + 1) * 8]
            hs.append(hg)
            h = hg[7:8]
        hf = jnp.concatenate(hs, axis=0)
        o_ref[0, pl.ds(r0, Tc), :] = hf * _gelu(rgt_ref[0, pl.ds(r0, Tc), :])
        return h

    h = lax.fori_loop(0, T // Tc, chunk, jnp.zeros((1, 128), F32))
    hl_ref[0] = h


def rglru_prompt(z3, conv_w, vec, w_a, w_x):
    B, T, _ = z3.shape
    NB = R_BLOCKS
    return pl.pallas_call(
        functools.partial(_rglru_body, T=T),
        grid=(B, NB),
        in_specs=[pl.BlockSpec((1, T, 128), lambda b, n: (b, 0, OFF_RX // 128 + n)),
                  pl.BlockSpec((1, T, 128), lambda b, n: (b, 0, OFF_RGT // 128 + n)),
                  pl.BlockSpec((CONV_W, 128), lambda b, n: (0, n)),
                  pl.BlockSpec((4, 128), lambda b, n: (0, n)),
                  pl.BlockSpec((1, 128, 128), lambda b, n: (n, 0, 0)),
                  pl.BlockSpec((1, 128, 128), lambda b, n: (n, 0, 0))],
        out_specs=[pl.BlockSpec((1, T, 128), lambda b, n: (b, 0, n)),
                   pl.BlockSpec((1, 1, 128), lambda b, n: (b, 0, n))],
        out_shape=[jax.ShapeDtypeStruct((B, T, R_WIDTH), F32),
                   jax.ShapeDtypeStruct((B, 1, R_WIDTH), F32)],
        scratch_shapes=[pltpu.VMEM((T + 8, 128), F32)],
        compiler_params=_cp("parallel", "parallel"),
        name="rglru_prompt",
    )(z3, z3, conv_w, vec, w_a, w_x)


ROWS_PER_PAGE = PAGE_SIZE * 4 * NSA_KV
CHUNKS_PER_PAGE = PAGE_SIZE // CMP_STRIDE
PAGES_PER_STEP = 16


def _sample_cmp_body(pt_ref, slopes_ref, *refs, NP, GRP):
    page_refs = refs[:GRP]
    w1c_ref, cb_ref, w2_ref, q_ref, oc_ref, idx_ref, xs_ref, lt_ref = refs[GRP:]
    p = pl.program_id(1)
    P = NP * PAGE_SIZE
    CPG = GRP * CHUNKS_PER_PAGE
    NCH = NP * CHUNKS_PER_PAGE
    n_sel = P // SEL_BLOCK + 1
    NSP = -(-n_sel // 128) * 128
    scale = HD ** -0.5

    for k in range(GRP):
        for kg in range(2 * NSA_KV):
            for c in range(CMP_STRIDE):
                v = page_refs[k][0, pl.ds(c * 8 + kg, CHUNKS_PER_PAGE, stride=CMP_STRIDE * 8), :]
                xs_ref[kg, k * CHUNKS_PER_PAGE:(k + 1) * CHUNKS_PER_PAGE, c * HD:(c + 1) * HD] = v

    r0 = pl.multiple_of(p * CPG, CPG)
    for kind in range(2):
        x = jnp.concatenate([xs_ref[kind * NSA_KV + g] for g in range(NSA_KV)], axis=0).astype(BF16)
        lt = _dot(x, w1c_ref[kind])
        for g in range(NSA_KV):
            lt_ref[kind * NSA_KV + g, pl.ds(r0, CPG), :] = lt[g * CPG:(g + 1) * CPG]

    @pl.when(p == NP // GRP - 1)
    def _():
        q128 = jnp.concatenate([q_ref[0], jnp.zeros((128 - NSA_HEADS, HD), F32)], axis=0).astype(BF16)
        lane = lax.broadcasted_iota(I32, (1, 128), 1)
        slope_row = jnp.zeros((1, 128), F32)
        for h in range(NSA_HEADS):
            slope_row = jnp.where(lane == h, slopes_ref[h], slope_row)
        n_col = lax.broadcasted_iota(I32, (NCH, 1), 0)
        dist = P - (n_col * CMP_STRIDE + (CMP_BLOCK - 1))
        maskc = (dist >= 0) & (n_col < NCH - 1)
        dist_f = dist.astype(F32)

        def cmp_rows(kind, g):
            lead = lt_ref[kind * NSA_KV + g, :, 0:HD]
            trail = lt_ref[kind * NSA_KV + g, :, HD:2 * HD]
            hid = _gelu(lead + pltpu.roll(trail, NCH - 1, axis=0) + cb_ref[kind, 0:1])
            return _dot(hid.astype(BF16), w2_ref[kind].astype(BF16)).astype(BF16)

        oc = jnp.zeros((128, HD), F32)
        psum2 = jnp.zeros((NCH, 128), F32)
        for g in range(NSA_KV):
            kc = cmp_rows(0, g)
            vc = cmp_rows(1, g)
            s = _dot_nt(kc, q128) * scale - slope_row * dist_f
            s = jnp.where(maskc, s, NEG_INF)
            m = jnp.max(s, axis=0, keepdims=True)
            pt = jnp.where(maskc, jnp.exp(s - m), 0.0)
            l = jnp.sum(pt, axis=0, keepdims=True)
            pt = pt / jnp.where(l > 0.0, l, 1.0)
            in_g = (lane >= g * NSA_GROUP) & (lane < (g + 1) * NSA_GROUP)
            pg = jnp.where(in_g, pt, 0.0)
            oc = oc + _dot(pg.T.astype(BF16), vc)
            psum2 = psum2 + jnp.where(lane == g, jnp.sum(pg, axis=1, keepdims=True), 0.0)
        oc_ref[0] = oc[0:NSA_HEADS]

        p_hi = psum2.astype(BF16)
        p_lo = (psum2 - p_hi.astype(F32)).astype(BF16)
        jo = lax.broadcasted_iota(I32, (NSP, NCH), 0) * SEL_BLOCK
        no = lax.broadcasted_iota(I32, (NSP, NCH), 1)
        ov = ((no * CMP_STRIDE < jo + SEL_BLOCK) & (no * CMP_STRIDE + CMP_BLOCK > jo)
              & (no < NCH - 1)).astype(BF16)
        imp = _dot(ov, p_hi) + _dot(ov, p_lo)
        jcol = lax.broadcasted_iota(I32, (NSP, 1), 0)
        cur = P // SEL_BLOCK
        forced = (jcol == 0) | (jcol == cur) | (jcol == cur - 1)
        imp = jnp.where(forced, imp + FORCE_BONUS, imp)
        imp = jnp.where(jcol * SEL_BLOCK <= P, imp, -1.0)
        imp = jnp.where(jcol < n_sel, imp, -2.0)
        ri = lax.broadcasted_iota(I32, (NSP, NSP), 0)
        ci = lax.broadcasted_iota(I32, (NSP, NSP), 1)
        lane_n = lax.broadcasted_iota(I32, (NSP, 128), 1)
        jcol_f = jcol.astype(F32)
        idx_ref[0] = jnp.zeros((8, 128), I32)
        for g in range(NSA_KV):
            col = jnp.sum(jnp.where(lane_n == g, imp, 0.0), axis=1, keepdims=True)
            rowv = jnp.sum(jnp.where(ri == ci, col, 0.0), axis=0, keepdims=True)
            beats = (rowv > col) | ((rowv == col) & (ci < ri))
            rank = jnp.sum(beats.astype(F32), axis=1, keepdims=True)
            hit = rank == lane_n.astype(F32)
            idx_ref[0, g:g + 1, :] = jnp.sum(jnp.where(hit, jcol_f, 0.0), axis=0,
                                             keepdims=True).astype(I32)


def sample_cmp_select(page_table, pool, layer, n_pool, w1c, cb, w2, q8, slopes):
    DB, NP = page_table.shape
    GRP = min(PAGES_PER_STEP, NP)
    assert NP % GRP == 0
    NCH = NP * CHUNKS_PER_PAGE
    K2 = CMP_STRIDE * HD
    base = layer * n_pool

    def page_spec(k):
        return pl.BlockSpec((1, ROWS_PER_PAGE, HD),
                            lambda b, p, pt: (base + pt[b * NP + p * GRP + k], 0, 0))

    grid_spec = pltpu.PrefetchScalarGridSpec(
        num_scalar_prefetch=1,
        grid=(DB, NP // GRP),
        in_specs=[pl.BlockSpec(memory_space=pltpu.SMEM)] + [page_spec(k) for k in range(GRP)] + [
                  pl.BlockSpec((2, K2, 2 * HD), lambda b, p, pt: (0, 0, 0)),
                  pl.BlockSpec((2, 8, HD), lambda b, p, pt: (0, 0, 0)),
                  pl.BlockSpec((2, HD, HD), lambda b, p, pt: (0, 0, 0)),
                  pl.BlockSpec((1, NSA_HEADS, HD), lambda b, p, pt: (b, 0, 0))],
        out_specs=[pl.BlockSpec((1, NSA_HEADS, HD), lambda b, p, pt: (b, 0, 0)),
                   pl.BlockSpec((1, 8, 128), lambda b, p, pt: (b, 0, 0))],
        scratch_shapes=[pltpu.VMEM((2 * NSA_KV, GRP * CHUNKS_PER_PAGE, K2), F32),
                        pltpu.VMEM((2 * NSA_KV, NCH, 2 * HD), F32)],
    )
    return pl.pallas_call(
        functools.partial(_sample_cmp_body, NP=NP, GRP=GRP),
        grid_spec=grid_spec,
        out_shape=[jax.ShapeDtypeStruct((DB, NSA_HEADS, HD), F32),
                   jax.ShapeDtypeStruct((DB, 8, 128), I32)],
        compiler_params=_cp("arbitrary", "arbitrary"),
        name="sample_cmp_select",
    )(page_table.reshape(-1), slopes, *([pool] * GRP), w1c, cb, w2, q8)


def _sample_sel_body(idx_ref, pt_ref, slopes_ref, *refs, NP, WB):
    page_refs = refs[:SEL_TOPK]
    win_ref, q_ref, new_ref, oc_ref, sm_ref, o_ref = refs[SEL_TOPK:]
    b = pl.program_id(0)
    g = pl.program_id(1)
    P = NP * PAGE_SIZE
    n_past = P // SEL_BLOCK
    scale = HD ** -0.5
    q8 = q_ref[0].astype(BF16)
    rowi = lax.broadcasted_iota(I32, (NSA_HEADS, 1), 0)
    slope = jnp.zeros((NSA_HEADS, 1), F32)
    for h in range(NSA_HEADS):
        slope = jnp.where(rowi == h, slopes_ref[h], slope)
    ibase = (b * NSA_KV + g) * SEL_TOPK

    def vec_dot(krow):
        return jnp.sum(q8.astype(F32) * krow.astype(BF16).astype(F32), axis=1, keepdims=True)

    def add_token(carry, s, valid, vrow):
        m, l, acc = carry
        s = jnp.where(valid, s, NEG_INF)
        m_new = jnp.maximum(m, s)
        alpha = jnp.exp(m - m_new)
        p = jnp.where(valid, jnp.exp(s - m_new), 0.0)
        return (m_new, alpha * l + p,
                alpha * acc + p.astype(BF16).astype(F32) * vrow.astype(BF16).astype(F32))

    carry = (jnp.full((NSA_HEADS, 1), NEG_INF, F32), jnp.zeros((NSA_HEADS, 1), F32),
             jnp.zeros((NSA_HEADS, HD), F32))
    n_new = jnp.int32(0)
    lane = lax.broadcasted_iota(I32, (1, PAGE_SIZE), 1)
    for t in range(SEL_TOPK):
        j = idx_ref[ibase + t]
        n_new = n_new + (j == n_past).astype(I32)
        kb = page_refs[t][0, pl.ds(2 * NSA_KV + g, PAGE_SIZE, stride=8), :].astype(BF16)
        vb = page_refs[t][0, pl.ds(3 * NSA_KV + g, PAGE_SIZE, stride=8), :].astype(BF16)
        kpos = (j // 2) * PAGE_SIZE + lane
        s = _dot_nt(q8, kb) * scale - slope * (P - kpos).astype(F32)
        jv = jnp.full((1, PAGE_SIZE), j, I32)
        carry = _online_update(carry, s, ((kpos // SEL_BLOCK) == jv) & (jv < n_past), vb)

    has_new = jnp.full((NSA_HEADS, 1), n_new, I32) > 0
    k_new = new_ref[0, pl.ds(2 * NSA_KV + g, 1), :]
    v_new = new_ref[0, pl.ds(3 * NSA_KV + g, 1), :]
    _, l_s, acc_s = add_token(carry, vec_dot(k_new) * scale, has_new, v_new)
    o_s = acc_s / l_s

    kw = win_ref[0, pl.ds(g, WB, stride=2 * NSA_KV), :].astype(BF16)
    vw = win_ref[0, pl.ds(NSA_KV + g, WB, stride=2 * NSA_KV), :].astype(BF16)
    d = WB - lax.broadcasted_iota(I32, (1, WB), 1)
    s = _dot_nt(q8, kw) * scale - slope * d.astype(F32)
    mask = d < WINDOW
    s = jnp.where(mask, s, NEG_INF)
    m = jnp.max(s, axis=1, keepdims=True)
    p = jnp.where(mask, jnp.exp(s - m), 0.0)
    carry = (m, jnp.sum(p, axis=1, keepdims=True), _dot(p.astype(BF16), vw))
    kw_new = new_ref[0, pl.ds(4 * NSA_KV + g, 1), :]
    vw_new = new_ref[0, pl.ds(5 * NSA_KV + g, 1), :]
    _, l_w, acc_w = add_token(carry, vec_dot(kw_new) * scale, rowi >= 0, vw_new)
    o_w = acc_w / l_w

    smb = jnp.broadcast_to(sm_ref[0], (NSA_HEADS, 128))
    lane_h = lax.broadcasted_iota(I32, (NSA_HEADS, 128), 1)

    def gate(c):
        return _sigmoid(jnp.sum(jnp.where(lane_h == rowi * 3 + c, smb, 0.0), axis=1, keepdims=True))

    o_ref[0, 0] = gate(0) * oc_ref[0] + gate(1) * o_s + gate(2) * o_w


def sample_sel_win(idx, page_table, pool, win, layer, n_pool, q8, newkv, oc, sm, slopes):
    DB, NP = page_table.shape
    WB = win.shape[1] // (2 * NSA_KV)
    n_past = NP * PAGE_SIZE // SEL_BLOCK
    pbase = layer * n_pool
    wbase = layer * DB
    idx_flat = idx[:, :NSA_KV, :SEL_TOPK].reshape(-1)

    def page_spec(t):
        def page_map(b, g, idx_r, pt_r):
            j = jnp.minimum(idx_r[(b * NSA_KV + g) * SEL_TOPK + t], n_past - 1)
            return (pbase + pt_r[b * NP + j // 2], 0, 0)
        return pl.BlockSpec((1, ROWS_PER_PAGE, HD), page_map)

    grid_spec = pltpu.PrefetchScalarGridSpec(
        num_scalar_prefetch=2,
        grid=(DB, NSA_KV),
        in_specs=[pl.BlockSpec(memory_space=pltpu.SMEM)] + [page_spec(t) for t in range(SEL_TOPK)] + [
                  pl.BlockSpec((1, WB * 2 * NSA_KV, HD), lambda b, g, i_, p_: (wbase + b, 0, 0)),
                  pl.BlockSpec((1, NSA_HEADS, HD), lambda b, g, i_, p_: (b, 0, 0)),
                  pl.BlockSpec((1, 6 * NSA_KV, HD), lambda b, g, i_, p_: (b, 0, 0)),
                  pl.BlockSpec((1, NSA_HEADS, HD), lambda b, g, i_, p_: (b, 0, 0)),
                  pl.BlockSpec((1, 1, 128), lambda b, g, i_, p_: (b, 0, 0))],
        out_specs=pl.BlockSpec((1, 1, NSA_HEADS, HD), lambda b, g, i_, p_: (b, g, 0, 0)),
    )
    return pl.pallas_call(
        functools.partial(_sample_sel_body, NP=NP, WB=WB),
        grid_spec=grid_spec,
        out_shape=jax.ShapeDtypeStruct((DB, NSA_KV, NSA_HEADS, HD), F32),
        compiler_params=_cp("arbitrary", "arbitrary"),
        name="sample_sel_win",
    )(idx_flat, page_table.reshape(-1), slopes, *([pool] * SEL_TOPK), win, q8, newkv, oc, sm)


def _mlstm_step_body(bif_ref, gs_ref, ms_ref, qkv_ref, mo_ref, ng_ref, c_ref, n_ref,
                     o_ref, co_ref, no_ref, mo_out_ref):
    b = pl.program_id(0)
    H = M_HEADS
    ri = lax.broadcasted_iota(I32, (HD, HD), 0)
    ci = lax.broadcasted_iota(I32, (HD, HD), 1)
    for h in range(H):
        ig = jnp.full((1, HD), gs_ref[b, h] + bif_ref[0, h], F32)
        fp = jnp.full((1, HD), gs_ref[b, H + h] + bif_ref[1, h], F32)
        m = jnp.full((1, HD), ms_ref[b, h], F32)
        lf = -_softplus(-fp)
        inter = lf + m
        m_t = jnp.maximum(inter, ig)
        w_intra = jnp.exp(ig - m_t)
        w_inter = jnp.exp(inter - m_t)
        q = qkv_ref[0, h:h + 1, :]
        k = qkv_ref[0, H + h:H + h + 1, :] * (HD ** -0.5)
        v = qkv_ref[0, 2 * H + h:2 * H + h + 1, :]
        C = c_ref[0, h]
        n = n_ref[0, h:h + 1, :]
        s = jnp.sum(q * k, axis=1, keepdims=True) * w_intra
        qC = _dot(jnp.broadcast_to(q, (8, HD)).astype(BF16), C.astype(BF16))[0:1]
        num = s * v + w_inter * qC
        den = s + w_inter * jnp.sum(q * n, axis=1, keepdims=True)
        hh = num / jnp.maximum(jnp.abs(den), jnp.exp(-m_t))
        k_col = jnp.sum(jnp.where(ri == ci, jnp.broadcast_to(k, (HD, HD)), 0.0), axis=1, keepdims=True)
        co_ref[0, h] = w_inter * C + (k_col * w_intra) * v
        no_ref[0, h:h + 1, :] = w_inter * n + w_intra * k
        mo_out_ref[0, h:h + 1, :] = m_t
        mu = jnp.mean(hh, axis=1, keepdims=True)
        hc = hh - mu
        var = jnp.mean(hc * hc, axis=1, keepdims=True)
        hn = hc * lax.rsqrt(var + LN_EPS) * ng_ref[h:h + 1, :]
        o_ref[0, h:h + 1, :] = _sigmoid(mo_ref[0, h:h + 1, :]) * hn


def mlstm_step(b_if, gates, m_state, qkv, mo, norm_g, C, n):
    DB = qkv.shape[0]
    H = M_HEADS
    smem = pl.BlockSpec(memory_space=pltpu.SMEM)
    row = pl.BlockSpec((1, H, HD), lambda b: (b, 0, 0))
    cspec = pl.BlockSpec((1, H, HD, HD), lambda b: (b, 0, 0, 0))
    return pl.pallas_call(
        _mlstm_step_body,
        grid=(DB,),
        in_specs=[smem, smem, smem, pl.BlockSpec((1, 3 * H, HD), lambda b: (b, 0, 0)), row,
                  pl.BlockSpec((H, HD), lambda b: (0, 0)), cspec, row],
        out_specs=[row, cspec, row, row],
        out_shape=[jax.ShapeDtypeStruct((DB, H, HD), F32), jax.ShapeDtypeStruct((DB, H, HD, HD), F32),
                   jax.ShapeDtypeStruct((DB, H, HD), F32), jax.ShapeDtypeStruct((DB, H, HD), F32)],
        compiler_params=_cp("arbitrary"),
        name="mlstm_step",
    )(b_if, gates, m_state, qkv, mo, norm_g.reshape(H, HD), C, n)


def _rglru_step_body(rx_ref, rgt_ref, buf_ref, h_ref, cw_ref, vec_ref, wa_ref, wx_ref, o_ref, ho_ref):
    cw = cw_ref[...]
    vec = vec_ref[...]
    xc = vec[0:1] + rx_ref[...] * cw[CONV_W - 1:CONV_W]
    for j in range(CONV_W - 1):
        xc = xc + buf_ref[j] * cw[j:j + 1]
    a, u = _rglru_gates(xc, wa_ref[0].astype(BF16), wx_ref[0].astype(BF16), vec)
    h = a * h_ref[...] + u
    ho_ref[...] = h
    o_ref[...] = h * _gelu(rgt_ref[...])


def rglru_step(rx, rgt, buf_t, h0, conv_w, vec, w_a, w_x):
    DB = rx.shape[0]
    col = pl.BlockSpec((DB, 128), lambda n: (0, n))
    return pl.pallas_call(
        _rglru_step_body,
        grid=(R_BLOCKS,),
        in_specs=[col, col, pl.BlockSpec((CONV_W - 1, DB, 128), lambda n: (0, 0, n)), col,
                  pl.BlockSpec((CONV_W, 128), lambda n: (0, n)),
                  pl.BlockSpec((4, 128), lambda n: (0, n)),
                  pl.BlockSpec((1, 128, 128), lambda n: (n, 0, 0)),
                  pl.BlockSpec((1, 128, 128), lambda n: (n, 0, 0))],
        out_specs=[col, col],
        out_shape=[jax.ShapeDtypeStruct((DB, R_WIDTH), F32), jax.ShapeDtypeStruct((DB, R_WIDTH), F32)],
        compiler_params=_cp("arbitrary"),
        name="rglru_step",
    )(rx, rgt, buf_t, h0, conv_w, vec, w_a, w_x)


def pack_w_in(w):
    pad = jnp.zeros(w.shape[:2] + (N_AL - 14888,), w.dtype)
    parts = [w[..., 8744:14888], w[..., 0:1024], w[..., 2584:5656], w[..., 5672:6696], w[..., 6696:7720],
             w[..., 7720:8744], w[..., 1024:2560], w[..., 2560:2584], w[..., 5656:5672], pad]
    return jnp.concatenate(parts, axis=-1).astype(BF16)


def prompt_layer(xp, lw, gw, layer, kv_buf, depth):
    B, T, D = xp.shape
    x2 = xp.reshape(B * T, D)
    M = B * T
    z = mm(x2, gw["w_in"], layer, tm=min(1024, M), tn=1152, out_dtype=F32)
    z3 = z.reshape(B, T, N_AL)
    kv_buf = kv_rows(z, kv_buf, layer, depth)
    kc, vc, cb = compress_from_z(z3, lw["phi_w1"], lw["pe"], lw["phi_b1"], lw["phi_w2"])
    o_a = nsa_prompt(z3, kc, vc, lw["slopes"])
    o_b, Cp, np_, mp = mlstm_prompt(z3, lw["b_if"], lw["norm_g"])
    o_c, hp = rglru_prompt(z3, lw["conv_w"], lw["rg_vec"], lw["w_a"], lw["w_x"])
    merged = merge(o_a.reshape(B * T, -1), o_b.reshape(B * T, -1), o_c.reshape(B * T, -1), z,
                   gw["w_branch"], layer, tm=min(512, M), tn=1024, out_dtype=BF16)
    h = mm_res_ln(merged, gw["w_out"], layer, x2, lw["ln_g"][0:1], lw["ln_b"][0:1], tm=min(256, M), tk=D)
    f1 = mm(h, gw["mlp_w1"], layer, tm=min(1024, M), tn=1024, out_dtype=BF16, act="relu2")
    x_new = mm_res_ln(f1, gw["mlp_w2"], layer, h, lw["ln_g"][1:2], lw["ln_b"][1:2], tm=min(512, M), tk=1024)
    n_win = min(WINDOW, T)
    wkv = z3[:, T - n_win:, OFF_KV + 4 * NSA_KV * HD:OFF_KV + 6 * NSA_KV * HD]
    states = (wkv.reshape(B, n_win, 2, NSA_KV, HD), Cp, np_[:, :, 0], mp[:, :, 0, 0], hp[:, 0],
              z3[:, T - (CONV_W - 1):, OFF_RX:OFF_RX + R_WIDTH])
    return x_new.reshape(B, T, D), states, cb, kv_buf


def sample_layer(xs, lw, gw, cb, layer, page_table, pool, n_pool, win, win_l, C0, n0, m0, h0, conv0):
    DB, D = xs.shape
    z = mm(xs, gw["w_in"], layer, tm=DB, tn=1152, out_dtype=F32)
    q8 = z[:, OFF_Q:OFF_Q + NSA_WIDTH].reshape(DB, NSA_HEADS, HD)
    newkv = z[:, OFF_KV:OFF_KV + 6 * NSA_KV * HD].reshape(DB, 6 * NSA_KV, HD)
    sm = z[:, OFF_SM:OFF_SM + 128].reshape(DB, 1, 128)
    oc, idx = sample_cmp_select(page_table, pool, layer, n_pool, lw["w1c"], cb, lw["phi_w2"], q8,
                                lw["slopes"])
    osw = sample_sel_win(idx, page_table, pool, win, layer, n_pool, q8, newkv, oc, sm, lw["slopes"])
    o_a = jnp.concatenate([osw[:, g, g * NSA_GROUP:(g + 1) * NSA_GROUP] for g in range(NSA_KV)],
                          axis=1).reshape(DB, NSA_WIDTH)
    qkv = z[:, OFF_MQKV:OFF_MQKV + 3 * M_WIDTH].reshape(DB, 3 * M_HEADS, HD)
    mo = z[:, OFF_MO:OFF_MO + M_WIDTH].reshape(DB, M_HEADS, HD)
    gates = z[:, OFF_SM + SM_I:OFF_SM + SM_I + 2 * M_HEADS]
    o_b, Cs, ns, ms = mlstm_step(lw["b_if"], gates, m0, qkv, mo, lw["norm_g"], C0, n0)
    rx = z[:, OFF_RX:OFF_RX + R_WIDTH]
    o_c, hs = rglru_step(rx, z[:, OFF_RGT:OFF_RGT + R_WIDTH], conv0.transpose(1, 0, 2), h0,
                         lw["conv_w"], lw["rg_vec"], lw["w_a"], lw["w_x"])
    merged = merge(o_a, o_b.reshape(DB, M_WIDTH), o_c, z, gw["w_branch"], layer, tm=DB, tn=1024,
                   out_dtype=F32)
    h = mm_res_ln(merged, gw["w_out"], layer, xs, lw["ln_g"][0:1], lw["ln_b"][0:1], tm=DB, tk=D)
    f1 = mm(h, gw["mlp_w1"], layer, tm=DB, tn=1024, out_dtype=F32, act="relu2")
    x_new = mm_res_ln(f1, gw["mlp_w2"], layer, h, lw["ln_g"][1:2], lw["ln_b"][1:2], tm=DB, tk=1024)
    kvn = newkv.reshape(DB, 1, 6, NSA_KV, HD)
    Wb = win_l.shape[1]
    keep = min(WINDOW, Wb + 1)
    win_new = jnp.concatenate([win_l, kvn[:, :, 4:6]], axis=1)[:, Wb + 1 - keep:]
    conv_new = jnp.concatenate([conv0[:, 1:], rx[:, None]], axis=1)
    states = (kvn[:, :, :4], win_new, Cs, ns, ms[:, :, 0], hs, conv_new)
    return x_new, states


def kernel(x_prompt, x_sample, cache_nsa_kv, cache_win_kv, state_mlstm_C, state_mlstm_n, state_mlstm_m,
           state_rglru_h, state_rglru_conv, page_table, w_in, nsa_pe, nsa_phi_w1, nsa_phi_b1, nsa_phi_w2,
           mlstm_b_if, mlstm_norm_g, rg_conv_w, rg_conv_b, rg_w_a, rg_b_a, rg_w_x, rg_b_x, rg_lambda,
           w_branch, w_out, ln_g, ln_b, mlp_w1, mlp_w2):
    DB, Tn, D = x_sample.shape
    assert Tn == 1 and D == D_MODEL
    depth, n_pool = cache_nsa_kv.shape[:2]
    pool = cache_nsa_kv.reshape(depth * n_pool, ROWS_PER_PAGE, HD)
    Wb = cache_win_kv.shape[2]
    win = cache_win_kv.reshape(depth * DB, Wb * 2 * NSA_KV, HD)
    slopes = alibi_slopes()
    half = CMP_STRIDE * HD
    B, T = x_prompt.shape[:2]
    xp = x_prompt
    xs = x_sample.reshape(DB, D)
    gw = {"w_in": pack_w_in(w_in), "w_branch": w_branch.astype(BF16), "w_out": w_out.astype(BF16),
          "mlp_w1": mlp_w1.astype(BF16), "mlp_w2": mlp_w2.astype(BF16)}
    new_p = [[] for _ in range(6)]
    new_s = [[] for _ in range(7)]
    kv_buf = None
    for l in range(depth):
        lw = {
            "pe": nsa_pe[l], "phi_w1": nsa_phi_w1[l], "phi_b1": nsa_phi_b1[l], "phi_w2": nsa_phi_w2[l],
            "w1c": jnp.concatenate([nsa_phi_w1[l][:, :half], nsa_phi_w1[l][:, half:]], axis=2).astype(BF16),
            "slopes": slopes, "b_if": mlstm_b_if[l], "norm_g": mlstm_norm_g[l],
            "conv_w": rg_conv_w[l],
            "rg_vec": jnp.stack([rg_conv_b[l], rg_b_a[l], rg_b_x[l], rg_lambda[l]]),
            "w_a": rg_w_a[l], "w_x": rg_w_x[l],
            "ln_g": ln_g[l], "ln_b": ln_b[l],
        }
        xp, st_p, cb, kv_buf = prompt_layer(xp, lw, gw, l, kv_buf, depth)
        xs, st_s = sample_layer(xs, lw, gw, cb, l, page_table, pool, n_pool, win, cache_win_kv[l],
                                state_mlstm_C[l], state_mlstm_n[l], state_mlstm_m[l],
                                state_rglru_h[l], state_rglru_conv[l])
        for lst, val in zip(new_p, st_p):
            lst.append(val)
        for lst, val in zip(new_s, st_s):
            lst.append(val)
    P = [jnp.stack(a) for a in new_p]
    S = [jnp.stack(a) for a in new_s]
    nsa_kv_p = kv_buf.reshape(depth, B, T, 4, NSA_KV, HD)
    return (xp, xs.reshape(DB, Tn, D), nsa_kv_p, S[0], P[0], S[1], P[1], S[2], P[2], S[3], P[3], S[4],
            P[4], S[5], P[5], S[6])
```

```python
import functools

import jax
import jax.numpy as jnp
from jax import lax
from jax.experimental import pallas as pl
from jax.experimental.pallas import tpu as pltpu

F32 = jnp.float32
BF16 = jnp.bfloat16
I32 = jnp.int32

D_MODEL = 2048
DEPTH = 4
PAGE_SIZE = 128
HD = 128
NSA_HEADS = 8
NSA_KV = 2
NSA_GROUP = 4
NSA_WIDTH = NSA_HEADS * HD
CMP_BLOCK = 32
CMP_STRIDE = 16
SEL_BLOCK = 64
SEL_TOPK = 16
WINDOW = 512
Q_BLOCK = 128
FORCE_BONUS = 1.0e4
M_HEADS = 8
M_WIDTH = M_HEADS * HD
R_WIDTH = 1024
R_BLOCKS = 8
CONV_W = 4
LRU_C = 8.0
D_FF = 4 * D_MODEL
DEEPNORM_ALPHA = (2 * DEPTH) ** 0.25
LN_EPS = 1e-5
NEG_INF = -1e30

OFF_MG = 0
OFF_Q = 6144
OFF_MQKV = 7168
OFF_MO = 10240
OFF_RX = 11264
OFF_RGT = 12288
OFF_KV = 13312
OFF_SM = 14848
N_AL = 15360
SM_I = 24
SM_F = 32

IN_TM, IN_TN = 1024, 1536
MERGE_TM, MERGE_TN = 256, 2048
OUT_TM = 256
UP_TM, UP_TN = 1024, 1024
DOWN_TM, DOWN_TK = 512, 2048

VMEM_LIMIT = 52 * 1024 * 1024


def _cp(*sem):
    return pltpu.CompilerParams(dimension_semantics=sem, vmem_limit_bytes=VMEM_LIMIT)


def _dot(a, b):
    return jnp.dot(a, b, preferred_element_type=F32)


def _dot_nt(a, b):
    return lax.dot_general(a, b, (((1,), (1,)), ((), ())), preferred_element_type=F32)


def _gelu(x):
    return 0.5 * x * (1.0 + jnp.tanh(0.7978845608028654 * (x + 0.044715 * (x * x * x))))


def _sigmoid(x):
    return 0.5 * jnp.tanh(0.5 * x) + 0.5


def _softplus(x):
    return jnp.maximum(x, 0.0) + jnp.log1p(jnp.exp(-jnp.abs(x)))


def _mm_body(a_ref, b_ref, o_ref, *, act):
    acc = _dot(a_ref[...].astype(BF16), b_ref[...])
    if act == "relu2":
        acc = jnp.square(jnp.maximum(acc, 0.0))
    o_ref[...] = acc.astype(o_ref.dtype)


def mm(a, b, layer, *, tm, tn, out_dtype, act=None):
    M, K = a.shape
    N = b.shape[2]
    assert M % tm == 0 and N % tn == 0
    return pl.pallas_call(
        functools.partial(_mm_body, act=act),
        grid=(M // tm, N // tn),
        in_specs=[pl.BlockSpec((tm, K), lambda i, j: (i, 0)),
                  pl.BlockSpec((None, K, tn), lambda i, j: (layer, 0, j))],
        out_specs=pl.BlockSpec((tm, tn), lambda i, j: (i, j)),
        out_shape=jax.ShapeDtypeStruct((M, N), out_dtype),
        compiler_params=_cp("parallel", "arbitrary"),
        name="mm",
    )(a, b)


def _mm_ln_body(a_ref, b_ref, x_ref, g_ref, bb_ref, o_ref, acc_ref, *, nk):
    k = pl.program_id(1)

    if nk > 1:
        @pl.when(k == 0)
        def _():
            acc_ref[...] = _dot(a_ref[...].astype(BF16), b_ref[...])

        @pl.when((k > 0) & (k < nk - 1))
        def _():
            acc_ref[...] += _dot(a_ref[...].astype(BF16), b_ref[...])

    @pl.when(k == nk - 1)
    def _():
        y = DEEPNORM_ALPHA * x_ref[...] + _dot(a_ref[...].astype(BF16), b_ref[...])
        if nk > 1:
            y = y + acc_ref[...]
        mu = jnp.mean(y, axis=-1, keepdims=True)
        yc = y - mu
        var = jnp.mean(yc * yc, axis=-1, keepdims=True)
        o_ref[...] = yc * lax.rsqrt(var + LN_EPS) * g_ref[...] + bb_ref[...]


def mm_res_ln(a, b, layer, x, g, bb, *, tm, tk):
    M, K = a.shape
    N = b.shape[2]
    nk = K // tk
    assert M % tm == 0 and K % tk == 0
    return pl.pallas_call(
        functools.partial(_mm_ln_body, nk=nk),
        grid=(M // tm, nk),
        in_specs=[pl.BlockSpec((tm, tk), lambda i, k: (i, k)),
                  pl.BlockSpec((None, tk, N), lambda i, k: (layer, k, 0)),
                  pl.BlockSpec((tm, N), lambda i, k: (i, 0)),
                  pl.BlockSpec((1, N), lambda i, k: (0, 0)),
                  pl.BlockSpec((1, N), lambda i, k: (0, 0))],
        out_specs=pl.BlockSpec((tm, N), lambda i, k: (i, 0)),
        out_shape=jax.ShapeDtypeStruct((M, N), F32),
        scratch_shapes=[pltpu.VMEM((tm, N), F32)],
        compiler_params=_cp("parallel", "arbitrary"),
        name="mm_res_ln",
    )(a, b, x, g, bb)


def _merge_body(oa_ref, ob_ref, oc_ref, g0_ref, g1_ref, g2_ref, w_ref, o_ref):
    acc = _sigmoid(g0_ref[...]) * _dot(oa_ref[...].astype(BF16), w_ref[0])
    acc += _sigmoid(g1_ref[...]) * _dot(ob_ref[...].astype(BF16), w_ref[1])
    acc += _sigmoid(g2_ref[...]) * _dot(oc_ref[...].astype(BF16), w_ref[2])
    o_ref[...] = acc.astype(o_ref.dtype)


def merge(o_a, o_b, o_c, z, w_branch, layer, *, tm, tn, out_dtype):
    M, W = o_a.shape
    N = w_branch.shape[3]
    assert OFF_MG % tn == 0 and N % tn == 0 and M % tm == 0
    gb = OFF_MG // tn
    nb = N // tn
    br = pl.BlockSpec((tm, W), lambda i, j: (i, 0))

    def gspec(k):
        return pl.BlockSpec((tm, tn), lambda i, j: (i, gb + k * nb + j))

    return pl.pallas_call(
        _merge_body,
        grid=(M // tm, nb),
        in_specs=[br, br, br, gspec(0), gspec(1), gspec(2),
                  pl.BlockSpec((None, 3, W, tn), lambda i, j: (layer, 0, 0, j))],
        out_specs=pl.BlockSpec((tm, tn), lambda i, j: (i, j)),
        out_shape=jax.ShapeDtypeStruct((M, N), out_dtype),
        compiler_params=_cp("parallel", "arbitrary"),
        name="merge",
    )(o_a, o_b, o_c, z, z, z, w_branch)


def _kv_rows_body(z_ref, *rest):
    o_ref = rest[-1]
    tm = z_ref.shape[0]
    for kg in range(4 * NSA_KV):
        o_ref[pl.ds(kg, tm, stride=4 * NSA_KV), :] = z_ref[:, kg * HD:(kg + 1) * HD]


def kv_rows(z, buf, layer, depth):
    M = z.shape[0]
    tm = min(512, M)
    nb = M // tm
    W = 4 * NSA_KV * HD
    assert OFF_KV % W == 0 and M % tm == 0
    in_specs = [pl.BlockSpec((tm, W), lambda i: (i, OFF_KV // W))]
    args = [z]
    aliases = {}
    if buf is not None:
        in_specs.append(pl.BlockSpec(memory_space=pl.ANY))
        args.append(buf)
        aliases = {1: 0}
    return pl.pallas_call(
        _kv_rows_body,
        grid=(nb,),
        in_specs=in_specs,
        out_specs=pl.BlockSpec((tm * 4 * NSA_KV, HD), lambda i: (layer * nb + i, 0)),
        out_shape=jax.ShapeDtypeStruct((depth * M * 4 * NSA_KV, HD), F32),
        input_output_aliases=aliases,
        compiler_params=_cp("arbitrary"),
        name="kv_rows",
    )(*args)


def _compress_body(x_ref, w1_ref, pe_ref, b1_ref, w2_ref, o_ref, cb_ref):
    half = CMP_STRIDE * HD
    x = x_ref[0, 0].astype(BF16)
    w1 = w1_ref[0].astype(BF16)
    lead = _dot(x, w1[:half])
    trail = _dot(x, w1[half:])
    n = x.shape[0]
    pe8 = jnp.broadcast_to(pe_ref[0], (8, 2 * half)).astype(BF16)
    cb = _dot(pe8, w1) + b1_ref[0]
    trail_next = pltpu.roll(trail, n - 1, axis=0)
    hid = _gelu(lead + trail_next + cb[0:1])
    o_ref[0, 0] = _dot(hid.astype(BF16), w2_ref[0].astype(BF16))
    cb_ref[0] = cb


def compress_prompt(x, w1, pe, b1, w2):
    _, S, n, K = x.shape
    return pl.pallas_call(
        _compress_body,
        grid=(2, S),
        in_specs=[pl.BlockSpec((1, 1, n, K), lambda k, s: (k, s, 0, 0)),
                  pl.BlockSpec((1, 2 * K, HD), lambda k, s: (k, 0, 0)),
                  pl.BlockSpec((1, 1, 2 * K), lambda k, s: (k, 0, 0)),
                  pl.BlockSpec((1, 1, HD), lambda k, s: (k, 0, 0)),
                  pl.BlockSpec((1, HD, HD), lambda k, s: (k, 0, 0))],
        out_specs=[pl.BlockSpec((1, 1, n, HD), lambda k, s: (k, s, 0, 0)),
                   pl.BlockSpec((1, 8, HD), lambda k, s: (k, 0, 0))],
        out_shape=[jax.ShapeDtypeStruct((2, S, n, HD), F32),
                   jax.ShapeDtypeStruct((2, 8, HD), F32)],
        compiler_params=_cp("arbitrary", "arbitrary"),
        name="compress_prompt",
    )(x, w1, pe, b1, w2)


def compress_from_z(z3, w1, pe, b1, w2):
    B, T, _ = z3.shape
    G = NSA_KV
    n = T // CMP_STRIDE
    x = z3[:, :, OFF_KV:OFF_KV + 2 * G * HD].reshape(B, n, CMP_STRIDE, 2, G, HD)
    x = x.transpose(3, 0, 4, 1, 2, 5).reshape(2, B * G, n, CMP_STRIDE * HD)
    out, cb = compress_prompt(x, w1, pe.reshape(2, 1, CMP_BLOCK * HD), b1.reshape(2, 1, HD), w2)
    return out[0], out[1], cb


def alibi_slopes():
    return jnp.asarray([2.0 ** (-8.0 * (h + 1) / NSA_HEADS) for h in range(NSA_HEADS)], F32)


SEL_CHUNK = 512


def _online_update(carry, s, mask, v):
    m, l, acc = carry
    s = jnp.where(mask, s, NEG_INF)
    m_new = jnp.maximum(m, jnp.max(s, axis=1, keepdims=True))
    alpha = jnp.exp(m - m_new)
    p = jnp.where(mask, jnp.exp(s - m_new), 0.0)
    l = alpha * l + jnp.sum(p, axis=1, keepdims=True)
    acc = alpha * acc + _dot(p.astype(BF16), v)
    return m_new, l, acc


MASK_BIG = 1.0e30
LOG2E = 1.4426950408889634
WIN_KEYS = WINDOW + Q_BLOCK


def _nsa_prompt_body(slopes_ref, q_ref, ks_ref, vs_ref, kw_ref, vw_ref, kc_ref, vc_ref,
                     sm_ref, o_ref, ks_bf, vs_aug, et_bf, kw_pad, vw_aug, wbias, sbias, y_a, y_b,
                     clist, *, T):
    assert T % (2 * SEL_CHUNK) == 0
    g = pl.program_id(1)
    i = pl.program_id(2)
    QB = Q_BLOCK
    R = NSA_GROUP
    RQ = R * QB
    NCP = T // CMP_STRIDE
    NC = NCP - 1
    NS = T // SEL_BLOCK
    NSP = 128
    assert NS <= NSP
    t0 = i * QB
    scale = HD ** -0.5
    c2 = scale * LOG2E

    @pl.when(i == 0)
    def _():
        lane = lax.broadcasted_iota(I32, (T, HD), 1)
        ones_col = jnp.where(lane == 0, 1.0, 0.0).astype(BF16)
        ks_bf[...] = ks_ref[0].astype(BF16)
        vs_aug[:, 0:HD] = vs_ref[0].astype(BF16)
        vs_aug[:, HD:2 * HD] = ones_col
        kk = lax.broadcasted_iota(I32, (T, HD), 0)
        et_bf[...] = jnp.where(kk // SEL_BLOCK == lane, MASK_BIG, 0.0).astype(BF16)
        kw_pad[0:WINDOW, :] = jnp.zeros((WINDOW, HD), BF16)
        kw_pad[WINDOW:WINDOW + T, :] = kw_ref[0].astype(BF16)
        vw_aug[0:WINDOW, :] = jnp.zeros((WINDOW, 2 * HD), BF16)
        vw_aug[WINDOW:WINDOW + T, 0:HD] = vw_ref[0].astype(BF16)
        vw_aug[WINDOW:WINDOW + T, HD:2 * HD] = ones_col
        wd = (lax.broadcasted_iota(I32, (QB, WIN_KEYS), 0) + WINDOW
              - lax.broadcasted_iota(I32, (QB, WIN_KEYS), 1))
        band = (wd >= 0) & (wd < WINDOW)
        wdf = wd.astype(F32)
        lf = lax.broadcasted_iota(I32, (QB, SEL_CHUNK), 1).astype(F32)
        for r in range(R):
            sl = slopes_ref[g * R + r]
            wbias[r * QB:(r + 1) * QB, :] = jnp.where(band, (-LOG2E * sl) * wdf, -MASK_BIG)
            sbias[r * QB:(r + 1) * QB, :] = (LOG2E * sl) * lf

    qb = q_ref[0]
    q4 = jnp.concatenate([qb[:, r * HD:(r + 1) * HD] for r in range(R)], axis=0).astype(BF16)
    row = lax.broadcasted_iota(I32, (RQ, 1), 0)
    qpos = t0 + (row & (QB - 1))
    slope = jnp.concatenate(
        [jnp.full((QB, 1), slopes_ref[g * R + r], F32) for r in range(R)], axis=0)

    kc = kc_ref[0, 0].astype(BF16)
    vc = vc_ref[0, 0].astype(BF16)
    n_idx = lax.broadcasted_iota(I32, (1, NCP), 1)
    dist = qpos - (n_idx * CMP_STRIDE + (CMP_BLOCK - 1))
    mask = (dist >= 0) & (n_idx < NC)
    s = _dot_nt(q4, kc) * scale - slope * dist.astype(F32)
    s = jnp.where(mask, s, NEG_INF)
    m = jnp.max(s, axis=1, keepdims=True)
    p = jnp.where(mask, jnp.exp(s - m), 0.0)
    l = jnp.sum(p, axis=1, keepdims=True)
    p = p * (1.0 / jnp.where(l > 0.0, l, 1.0))
    o_c = _dot(p.astype(BF16), vc)

    psum = p[0:QB]
    for r in range(1, R):
        psum = psum + p[r * QB:(r + 1) * QB]
    p_hi = psum.astype(BF16)
    p_lo = (psum - p_hi.astype(F32)).astype(BF16)
    jo = lax.broadcasted_iota(I32, (NS, NCP), 0) * SEL_BLOCK
    no = lax.broadcasted_iota(I32, (NS, NCP), 1)
    ov = ((no * CMP_STRIDE < jo + SEL_BLOCK) & (no * CMP_STRIDE + CMP_BLOCK > jo)
          & (no < NC)).astype(BF16)
    imp = _dot_nt(ov, p_hi) + _dot_nt(ov, p_lo)
    jj = lax.broadcasted_iota(I32, (NS, QB), 0)
    qp = t0 + lax.broadcasted_iota(I32, (NS, QB), 1)
    cur = qp // SEL_BLOCK
    forced = (jj == 0) | (jj == cur) | (jj == cur - 1)
    imp = jnp.where(forced, imp + FORCE_BONUS, imp)
    imp = jnp.where(jj * SEL_BLOCK <= qp, imp, -1.0)
    ranks = [jnp.zeros((8, QB), F32) for _ in range(NS // 8)]
    for j2 in range(NS):
        rv = imp[j2:j2 + 1, :]
        for v in range(NS // 8):
            blk = imp[8 * v:8 * v + 8]
            if 8 * v > j2:
                beats = rv >= blk
            elif 8 * v + 7 < j2:
                beats = rv > blk
            else:
                beats = (rv > blk) | ((rv == blk) & (jj[8 * v:8 * v + 8] > j2))
            ranks[v] = ranks[v] + jnp.where(beats, 1.0, 0.0)
    rank = jnp.concatenate(ranks, axis=0)
    selm_t = jnp.where(rank < SEL_TOPK, 0.0, -1.0)
    if NSP > NS:
        selm_t = jnp.concatenate([selm_t, jnp.zeros((NSP - NS, QB), F32)], axis=0)
    selm = selm_t.T.astype(BF16)

    sl_col = LOG2E * slope
    rel = (lax.broadcasted_iota(I32, (QB, SEL_CHUNK), 1)
           - lax.broadcasted_iota(I32, (QB, SEL_CHUNK), 0))

    def scores(c):
        k0 = pl.multiple_of(c * SEL_CHUNK, SEL_CHUNK)
        mb = _dot_nt(selm, et_bf[pl.ds(k0, SEL_CHUNK), :])
        mb = jnp.where(rel <= t0 - k0, mb, -MASK_BIG)
        return (_dot_nt(q4, ks_bf[pl.ds(k0, SEL_CHUNK), :]) * c2 + sbias[...]
                + jnp.concatenate([mb] * R, axis=0))

    def consume(c, y, carry):
        m, acc = carry
        k0 = pl.multiple_of(c * SEL_CHUNK, SEL_CHUNK)
        off = sl_col * (k0 - t0).astype(F32)
        m_new = jnp.maximum(m, jnp.max(y, axis=1, keepdims=True) + off)
        p = jnp.exp2(y - (m_new - off))
        acc = jnp.exp2(m - m_new) * acc + _dot(p.astype(BF16), vs_aug[pl.ds(k0, SEL_CHUNK), :])
        return m_new, acc

    n_chunks = (t0 + QB + SEL_CHUNK - 1) // SEL_CHUNK
    bpc = SEL_CHUNK // SEL_BLOCK
    n_act = jnp.int32(0)
    dummy = jnp.int32(T // SEL_CHUNK - 1)
    for c in reversed(range(T // SEL_CHUNK)):
        act = (jnp.sum(selm_t[c * bpc:(c + 1) * bpc] + 1.0) > 0.0) & (c < n_chunks)
        dummy = jnp.where(act, dummy, c)
    for c in range(T // SEL_CHUNK):
        act = (jnp.sum(selm_t[c * bpc:(c + 1) * bpc] + 1.0) > 0.0) & (c < n_chunks)
        clist[n_act] = c
        n_act = n_act + act.astype(I32)
    clist[n_act] = dummy

    def pair(pi, carry):
        k = 2 * pi
        y_b[...] = scores(clist[k + 1])
        carry = consume(clist[k], y_a[...], carry)
        y_a[...] = scores(clist[jnp.minimum(k + 2, n_act)])
        return consume(clist[k + 1], y_b[...], carry)

    init = (jnp.full((RQ, 1), -MASK_BIG, F32), jnp.zeros((RQ, 2 * HD), F32))
    y_a[...] = scores(clist[0])
    _, acc_s = lax.fori_loop(0, (n_act + 1) // 2, pair, init)
    o_s = acc_s[:, 0:HD] * (1.0 / acc_s[:, HD:HD + 1])

    w0 = pl.multiple_of(t0, QB)
    vrow = jnp.where(lax.broadcasted_iota(I32, (1, WIN_KEYS), 1) >= WINDOW - t0, 0.0, -MASK_BIG)
    y = _dot_nt(q4, kw_pad[pl.ds(w0, WIN_KEYS), :]) * c2 + wbias[...] + vrow
    p = jnp.exp2(y - jnp.max(y, axis=1, keepdims=True))
    acc_w = _dot(p.astype(BF16), vw_aug[pl.ds(w0, WIN_KEYS), :])
    o_w = acc_w[:, 0:HD] * (1.0 / acc_w[:, HD:HD + 1])

    smb = sm_ref[0]
    lane = lax.broadcasted_iota(I32, (QB, 128), 1)

    def gate(c):
        cols = []
        for r in range(R):
            idx = (g * R + r) * 3 + c
            cols.append(jnp.sum(jnp.where(lane == idx, smb, 0.0), axis=1, keepdims=True))
        return _sigmoid(jnp.concatenate(cols, axis=0))

    o = gate(0) * o_c + gate(1) * o_s + gate(2) * o_w
    o_ref[0] = jnp.concatenate([o[r * QB:(r + 1) * QB] for r in range(R)], axis=1).astype(o_ref.dtype)


def nsa_prompt(z3, kc, vc, slopes):
    B, T, _ = z3.shape
    G = NSA_KV
    kvb = OFF_KV // HD

    def kvspec(kind):
        return pl.BlockSpec((1, T, HD), lambda b, g, i: (b, 0, kvb + kind * G + g))

    cspec = pl.BlockSpec((1, 1, T // CMP_STRIDE, HD), lambda b, g, i: (b * G + g, 0, 0, 0))
    GW = NSA_GROUP * HD
    assert OFF_Q % GW == 0
    qspec = pl.BlockSpec((1, Q_BLOCK, GW), lambda b, g, i: (b, i, OFF_Q // GW + g))
    return pl.pallas_call(
        functools.partial(_nsa_prompt_body, T=T),
        grid=(B, G, T // Q_BLOCK),
        in_specs=[pl.BlockSpec(memory_space=pltpu.SMEM),
                  qspec, kvspec(2), kvspec(3), kvspec(4), kvspec(5), cspec, cspec,
                  pl.BlockSpec((1, Q_BLOCK, 128), lambda b, g, i: (b, i, OFF_SM // 128))],
        out_specs=pl.BlockSpec((1, Q_BLOCK, GW), lambda b, g, i: (b, i, g)),
        out_shape=jax.ShapeDtypeStruct((B, T, NSA_WIDTH), BF16),
        scratch_shapes=[pltpu.VMEM((T, HD), BF16),
                        pltpu.VMEM((T, 2 * HD), BF16),
                        pltpu.VMEM((T, HD), BF16),
                        pltpu.VMEM((T + WINDOW, HD), BF16),
                        pltpu.VMEM((T + WINDOW, 2 * HD), BF16),
                        pltpu.VMEM((NSA_GROUP * Q_BLOCK, WIN_KEYS), F32),
                        pltpu.VMEM((NSA_GROUP * Q_BLOCK, SEL_CHUNK), F32),
                        pltpu.VMEM((NSA_GROUP * Q_BLOCK, SEL_CHUNK), F32),
                        pltpu.VMEM((NSA_GROUP * Q_BLOCK, SEL_CHUNK), F32),
                        pltpu.SMEM((T // SEL_CHUNK + 8,), I32)],
        compiler_params=_cp("arbitrary", "arbitrary", "arbitrary"),
        name="nsa_prompt",
    )(slopes, z3, z3, z3, z3, z3, kc[:, None], vc[:, None], z3)


M_CHUNK_K = 128


M_HEADS_PER_STEP = 4
M_ROWS_PER_STEP = 512


def _mlstm_body(bif_ref, q_ref, k_ref, v_ref, sm_ref, mo_ref, ng_ref,
                o_ref, c_ref, n_ref, m_ref, c_sc, n_sc, m_sc, *, NT, TB):
    hb = pl.program_id(1)
    t = pl.program_id(2)
    L = M_CHUNK_K
    HB = M_HEADS_PER_STEP
    lane = lax.broadcasted_iota(I32, (L, 128), 1)
    ri = lax.broadcasted_iota(I32, (L, L), 0)
    ci = lax.broadcasted_iota(I32, (L, L), 1)
    eye = ri == ci
    tril = ri >= ci

    @pl.when(t == 0)
    def _():
        c_sc[...] = jnp.zeros_like(c_sc)
        n_sc[...] = jnp.zeros_like(n_sc)
        m_sc[...] = jnp.zeros_like(m_sc)

    def head_chunk(hh, r0):
        h = hb * HB + hh
        cs = slice(hh * HD, (hh + 1) * HD)
        bi = bif_ref[0, h]
        bf = bif_ref[1, h]
        ng = ng_ref[:, cs]
        C = c_sc[hh]
        n = n_sc[hh]
        m = m_sc[hh][:, 0:1]
        q = q_ref[0, pl.ds(r0, L), cs]
        k = k_ref[0, pl.ds(r0, L), cs] * (HD ** -0.5)
        v = v_ref[0, pl.ds(r0, L), cs]
        smb = sm_ref[0, pl.ds(r0, L), :]
        ig_col = jnp.sum(jnp.where(lane == SM_I + h, smb, 0.0), axis=1, keepdims=True) + bi
        fp_col = jnp.sum(jnp.where(lane == SM_F + h, smb, 0.0), axis=1, keepdims=True) + bf
        lf_col = -_softplus(-fp_col)
        lf_row = jnp.sum(jnp.where(eye, lf_col, 0.0), axis=0, keepdims=True)
        ig_row = jnp.sum(jnp.where(eye, ig_col, 0.0), axis=0, keepdims=True)
        b_col = jnp.sum(jnp.where(tril, lf_row, 0.0), axis=1, keepdims=True)
        b_row = jnp.sum(jnp.where(ri <= ci, lf_col, 0.0), axis=0, keepdims=True)
        d = jnp.where(tril, b_col - b_row + ig_row, NEG_INF)
        inter = b_col + m
        m_t = jnp.maximum(inter, jnp.max(d, axis=1, keepdims=True))
        w_intra = jnp.exp(d - m_t)
        w_inter = jnp.exp(inter - m_t)
        qb = q.astype(BF16)
        vb = v.astype(BF16)
        s = _dot_nt(qb, k.astype(BF16)) * w_intra
        num = _dot(s.astype(BF16), vb) + w_inter * _dot(qb, C.astype(BF16))
        den = jnp.sum(s, axis=1, keepdims=True) + w_inter * jnp.sum(q * n, axis=1, keepdims=True)
        hh_ = num * (1.0 / jnp.maximum(jnp.abs(den), jnp.exp(-m_t)))
        m_new = m_t[L - 1:L, :]
        b_last = b_col[L - 1:L, :]
        decay = jnp.exp(b_last + m - m_new)
        w_s = jnp.exp(b_last - b_col + ig_col - m_new)
        kw = k * w_s
        c_sc[hh] = decay * C + _dot(kw.T.astype(BF16), vb)
        n_sc[hh] = decay * n + jnp.sum(kw, axis=0, keepdims=True)
        m_sc[hh] = jnp.broadcast_to(m_new, (1, 128))
        mu = jnp.mean(hh_, axis=1, keepdims=True)
        hc = hh_ - mu
        var = jnp.mean(hc * hc, axis=1, keepdims=True)
        hn = hc * lax.rsqrt(var + LN_EPS) * ng
        o_ref[0, pl.ds(r0, L), cs] = (_sigmoid(mo_ref[0, pl.ds(r0, L), cs]) * hn).astype(o_ref.dtype)

    def sub(c, carry):
        r0 = pl.multiple_of(c * L, L)
        for hh in range(HB):
            head_chunk(hh, r0)
        return carry

    lax.fori_loop(0, TB // L, sub, 0)

    @pl.when(t == NT - 1)
    def _():
        c_ref[0] = c_sc[...]
        n_ref[0] = n_sc[...]
        m_ref[0] = m_sc[...]


def mlstm_prompt(z3, b_if, norm_g):
    B, T, _ = z3.shape
    H = M_HEADS
    HB = M_HEADS_PER_STEP
    TB = min(M_ROWS_PER_STEP, T)
    W = HB * HD
    assert OFF_MQKV % W == 0 and OFF_MO % W == 0 and T % TB == 0

    def hspec(off):
        return pl.BlockSpec((1, TB, W), lambda b, h, t: (b, t, off // W + h))

    return pl.pallas_call(
        functools.partial(_mlstm_body, NT=T // TB, TB=TB),
        grid=(B, H // HB, T // TB),
        in_specs=[pl.BlockSpec(memory_space=pltpu.SMEM),
                  hspec(OFF_MQKV), hspec(OFF_MQKV + M_WIDTH), hspec(OFF_MQKV + 2 * M_WIDTH),
                  pl.BlockSpec((1, TB, 128), lambda b, h, t: (b, t, OFF_SM // 128)),
                  hspec(OFF_MO),
                  pl.BlockSpec((1, W), lambda b, h, t: (0, h))],
        out_specs=[pl.BlockSpec((1, TB, W), lambda b, h, t: (b, t, h)),
                   pl.BlockSpec((1, HB, HD, HD), lambda b, h, t: (b, h, 0, 0)),
                   pl.BlockSpec((1, HB, 1, HD), lambda b, h, t: (b, h, 0, 0)),
                   pl.BlockSpec((1, HB, 1, 128), lambda b, h, t: (b, h, 0, 0))],
        out_shape=[jax.ShapeDtypeStruct((B, T, M_WIDTH), BF16),
                   jax.ShapeDtypeStruct((B, H, HD, HD), F32),
                   jax.ShapeDtypeStruct((B, H, 1, HD), F32),
                   jax.ShapeDtypeStruct((B, H, 1, 128), F32)],
        scratch_shapes=[pltpu.VMEM((HB, HD, HD), F32), pltpu.VMEM((HB, 1, HD), F32),
                        pltpu.VMEM((HB, 1, 128), F32)],
        compiler_params=_cp("arbitrary", "arbitrary", "arbitrary"),
        name="mlstm_prompt",
    )(b_if, z3, z3, z3, z3, z3, norm_g.reshape(1, M_WIDTH))


R_CHUNK = 128


def _rglru_gates(xc, wa, wx, vec):
    xb = xc.astype(BF16)
    r = _sigmoid(_dot(xb, wa) + vec[1:2])
    i = _sigmoid(_dot(xb, wx) + vec[2:3])
    log_a = -LRU_C * r * _softplus(-vec[3:4])
    a = jnp.exp(log_a)
    at = -jnp.tanh(log_a)
    u = jnp.sqrt(2.0 * at) * lax.rsqrt(1.0 + at) * (i * xc)
    return a, u


def _rglru_body(rx_ref, rgt_ref, cw_ref, vec_ref, wa_ref, wx_ref, o_ref, hl_ref, xpad_ref, *, T):
    Tc = R_CHUNK
    xpad_ref[0:8, :] = jnp.zeros((8, 128), F32)
    xpad_ref[8:T + 8, :] = rx_ref[0]
    cw = cw_ref[...]
    vec = vec_ref[...]
    wa = wa_ref[0].astype(BF16)
    wx = wx_ref[0].astype(BF16)
    rowmod = lax.broadcasted_iota(I32, (Tc, 128), 0) & 7

    def chunk(c, h):
        r0 = pl.multiple_of(c * Tc, Tc)
        xc = vec[0:1]
        for j in range(CONV_W):
            xc = xc + xpad_ref[pl.ds(r0 + (8 - (CONV_W - 1)) + j, Tc), :] * cw[j:j + 1]
        a, u = _rglru_gates(xc, wa, wx, vec)
        for sft in (1, 2, 4):
            a1 = pltpu.roll(a, sft, axis=0)
            u1 = pltpu.roll(u, sft, axis=0)
            ok = rowmod >= sft
            u = jnp.where(ok, a * u1 + u, u)
            a = jnp.where(ok, a * a1, a)
        hs = []
        for gi in range(Tc // 8):
            hg = a[gi * 8:(gi + 1) * 8] * h + u[gi * 8:(gi + 1) * 8]
            hs.append(hg)
            h = hg[7:8]
        hf = jnp.concatenate(hs, axis=0)
        o_ref[0, pl.ds(r0, Tc), :] = (hf * _gelu(rgt_ref[0, pl.ds(r0, Tc), :])).astype(o_ref.dtype)
        return h

    h = lax.fori_loop(0, T // Tc, chunk, jnp.zeros((1, 128), F32))
    hl_ref[0] = h


def rglru_prompt(z3, conv_w, vec, w_a, w_x):
    B, T, _ = z3.shape
    NB = R_BLOCKS
    return pl.pallas_call(
        functools.partial(_rglru_body, T=T),
        grid=(B, NB),
        in_specs=[pl.BlockSpec((1, T, 128), lambda b, n: (b, 0, OFF_RX // 128 + n)),
                  pl.BlockSpec((1, T, 128), lambda b, n: (b, 0, OFF_RGT // 128 + n)),
                  pl.BlockSpec((CONV_W, 128), lambda b, n: (0, n)),
                  pl.BlockSpec((4, 128), lambda b, n: (0, n)),
                  pl.BlockSpec((1, 128, 128), lambda b, n: (n, 0, 0)),
                  pl.BlockSpec((1, 128, 128), lambda b, n: (n, 0, 0))],
        out_specs=[pl.BlockSpec((1, T, 128), lambda b, n: (b, 0, n)),
                   pl.BlockSpec((1, 1, 128), lambda b, n: (b, 0, n))],
        out_shape=[jax.ShapeDtypeStruct((B, T, R_WIDTH), BF16),
                   jax.ShapeDtypeStruct((B, 1, R_WIDTH), F32)],
        scratch_shapes=[pltpu.VMEM((T + 8, 128), F32)],
        compiler_params=_cp("parallel", "parallel"),
        name="rglru_prompt",
    )(z3, z3, conv_w, vec, w_a, w_x)


ROWS_PER_PAGE = PAGE_SIZE * 4 * NSA_KV
CHUNKS_PER_PAGE = PAGE_SIZE // CMP_STRIDE
PAGES_PER_STEP = 16


def _sample_cmp_body(pt_ref, slopes_ref, *refs, NP, GRP):
    page_refs = refs[:GRP]
    w1c_ref, cb_ref, w2_ref, q_ref, oc_ref, idx_ref, xs_ref, lt_ref = refs[GRP:]
    p = pl.program_id(1)
    P = NP * PAGE_SIZE
    CPG = GRP * CHUNKS_PER_PAGE
    NCH = NP * CHUNKS_PER_PAGE
    n_sel = P // SEL_BLOCK + 1
    NSP = -(-n_sel // 128) * 128
    scale = HD ** -0.5

    for k in range(GRP):
        for kg in range(2 * NSA_KV):
            for c in range(CMP_STRIDE):
                v = page_refs[k][0, pl.ds(c * 8 + kg, CHUNKS_PER_PAGE, stride=CMP_STRIDE * 8), :]
                xs_ref[kg, k * CHUNKS_PER_PAGE:(k + 1) * CHUNKS_PER_PAGE, c * HD:(c + 1) * HD] = v

    r0 = pl.multiple_of(p * CPG, CPG)
    for kind in range(2):
        x = jnp.concatenate([xs_ref[kind * NSA_KV + g] for g in range(NSA_KV)], axis=0).astype(BF16)
        lt = _dot(x, w1c_ref[kind])
        for g in range(NSA_KV):
            lt_ref[kind * NSA_KV + g, pl.ds(r0, CPG), :] = lt[g * CPG:(g + 1) * CPG]

    @pl.when(p == NP // GRP - 1)
    def _():
        q128 = jnp.concatenate([q_ref[0], jnp.zeros((128 - NSA_HEADS, HD), F32)], axis=0).astype(BF16)
        lane = lax.broadcasted_iota(I32, (1, 128), 1)
        slope_row = jnp.zeros((1, 128), F32)
        for h in range(NSA_HEADS):
            slope_row = jnp.where(lane == h, slopes_ref[h], slope_row)
        n_col = lax.broadcasted_iota(I32, (NCH, 1), 0)
        dist = P - (n_col * CMP_STRIDE + (CMP_BLOCK - 1))
        maskc = (dist >= 0) & (n_col < NCH - 1)
        dist_f = dist.astype(F32)

        def cmp_rows(kind, g):
            lead = lt_ref[kind * NSA_KV + g, :, 0:HD]
            trail = lt_ref[kind * NSA_KV + g, :, HD:2 * HD]
            hid = _gelu(lead + pltpu.roll(trail, NCH - 1, axis=0) + cb_ref[kind, 0:1])
            return _dot(hid.astype(BF16), w2_ref[kind].astype(BF16)).astype(BF16)

        oc = jnp.zeros((128, HD), F32)
        psum2 = jnp.zeros((NCH, 128), F32)
        for g in range(NSA_KV):
            kc = cmp_rows(0, g)
            vc = cmp_rows(1, g)
            s = _dot_nt(kc, q128) * scale - slope_row * dist_f
            s = jnp.where(maskc, s, NEG_INF)
            m = jnp.max(s, axis=0, keepdims=True)
            pt = jnp.where(maskc, jnp.exp(s - m), 0.0)
            l = jnp.sum(pt, axis=0, keepdims=True)
            pt = pt / jnp.where(l > 0.0, l, 1.0)
            in_g = (lane >= g * NSA_GROUP) & (lane < (g + 1) * NSA_GROUP)
            pg = jnp.where(in_g, pt, 0.0)
            oc = oc + _dot(pg.T.astype(BF16), vc)
            psum2 = psum2 + jnp.where(lane == g, jnp.sum(pg, axis=1, keepdims=True), 0.0)
        oc_ref[0] = oc[0:NSA_HEADS]

        p_hi = psum2.astype(BF16)
        p_lo = (psum2 - p_hi.astype(F32)).astype(BF16)
        jo = lax.broadcasted_iota(I32, (NSP, NCH), 0) * SEL_BLOCK
        no = lax.broadcasted_iota(I32, (NSP, NCH), 1)
        ov = ((no * CMP_STRIDE < jo + SEL_BLOCK) & (no * CMP_STRIDE + CMP_BLOCK > jo)
              & (no < NCH - 1)).astype(BF16)
        imp = _dot(ov, p_hi) + _dot(ov, p_lo)
        jcol = lax.broadcasted_iota(I32, (NSP, 1), 0)
        cur = P // SEL_BLOCK
        forced = (jcol == 0) | (jcol == cur) | (jcol == cur - 1)
        imp = jnp.where(forced, imp + FORCE_BONUS, imp)
        imp = jnp.where(jcol * SEL_BLOCK <= P, imp, -1.0)
        imp = jnp.where(jcol < n_sel, imp, -2.0)
        ri = lax.broadcasted_iota(I32, (NSP, NSP), 0)
        ci = lax.broadcasted_iota(I32, (NSP, NSP), 1)
        lane_n = lax.broadcasted_iota(I32, (NSP, 128), 1)
        jcol_f = jcol.astype(F32)
        idx_ref[0] = jnp.zeros((8, 128), I32)
        for g in range(NSA_KV):
            col = jnp.sum(jnp.where(lane_n == g, imp, 0.0), axis=1, keepdims=True)
            rowv = jnp.sum(jnp.where(ri == ci, col, 0.0), axis=0, keepdims=True)
            beats = (rowv > col) | ((rowv == col) & (ci < ri))
            rank = jnp.sum(beats.astype(F32), axis=1, keepdims=True)
            hit = rank == lane_n.astype(F32)
            idx_ref[0, g:g + 1, :] = jnp.sum(jnp.where(hit, jcol_f, 0.0), axis=0,
                                             keepdims=True).astype(I32)


def sample_cmp_select(page_table, pool, layer, n_pool, w1c, cb, w2, q8, slopes):
    DB, NP = page_table.shape
    GRP = min(PAGES_PER_STEP, NP)
    assert NP % GRP == 0
    NCH = NP * CHUNKS_PER_PAGE
    K2 = CMP_STRIDE * HD
    base = layer * n_pool

    def page_spec(k):
        return pl.BlockSpec((1, ROWS_PER_PAGE, HD),
                            lambda b, p, pt: (base + pt[b * NP + p * GRP + k], 0, 0))

    grid_spec = pltpu.PrefetchScalarGridSpec(
        num_scalar_prefetch=1,
        grid=(DB, NP // GRP),
        in_specs=[pl.BlockSpec(memory_space=pltpu.SMEM)] + [page_spec(k) for k in range(GRP)] + [
                  pl.BlockSpec((2, K2, 2 * HD), lambda b, p, pt: (0, 0, 0)),
                  pl.BlockSpec((2, 8, HD), lambda b, p, pt: (0, 0, 0)),
                  pl.BlockSpec((2, HD, HD), lambda b, p, pt: (0, 0, 0)),
                  pl.BlockSpec((1, NSA_HEADS, HD), lambda b, p, pt: (b, 0, 0))],
        out_specs=[pl.BlockSpec((1, NSA_HEADS, HD), lambda b, p, pt: (b, 0, 0)),
                   pl.BlockSpec((1, 8, 128), lambda b, p, pt: (b, 0, 0))],
        scratch_shapes=[pltpu.VMEM((2 * NSA_KV, GRP * CHUNKS_PER_PAGE, K2), F32),
                        pltpu.VMEM((2 * NSA_KV, NCH, 2 * HD), F32)],
    )
    return pl.pallas_call(
        functools.partial(_sample_cmp_body, NP=NP, GRP=GRP),
        grid_spec=grid_spec,
        out_shape=[jax.ShapeDtypeStruct((DB, NSA_HEADS, HD), F32),
                   jax.ShapeDtypeStruct((DB, 8, 128), I32)],
        compiler_params=_cp("arbitrary", "arbitrary"),
        name="sample_cmp_select",
    )(page_table.reshape(-1), slopes, *([pool] * GRP), w1c, cb, w2, q8)


def _sample_sel_body(idx_ref, pt_ref, slopes_ref, *refs, NP, WB):
    page_refs = refs[:SEL_TOPK]
    win_ref, q_ref, new_ref, oc_ref, sm_ref, o_ref = refs[SEL_TOPK:]
    b = pl.program_id(0)
    g = pl.program_id(1)
    P = NP * PAGE_SIZE
    n_past = P // SEL_BLOCK
    scale = HD ** -0.5
    q8 = q_ref[0].astype(BF16)
    rowi = lax.broadcasted_iota(I32, (NSA_HEADS, 1), 0)
    slope = jnp.zeros((NSA_HEADS, 1), F32)
    for h in range(NSA_HEADS):
        slope = jnp.where(rowi == h, slopes_ref[h], slope)
    ibase = (b * NSA_KV + g) * SEL_TOPK

    def vec_dot(krow):
        return jnp.sum(q8.astype(F32) * krow.astype(BF16).astype(F32), axis=1, keepdims=True)

    def add_token(carry, s, valid, vrow):
        m, l, acc = carry
        s = jnp.where(valid, s, NEG_INF)
        m_new = jnp.maximum(m, s)
        alpha = jnp.exp(m - m_new)
        p = jnp.where(valid, jnp.exp(s - m_new), 0.0)
        return (m_new, alpha * l + p,
                alpha * acc + p.astype(BF16).astype(F32) * vrow.astype(BF16).astype(F32))

    carry = (jnp.full((NSA_HEADS, 1), NEG_INF, F32), jnp.zeros((NSA_HEADS, 1), F32),
             jnp.zeros((NSA_HEADS, HD), F32))
    n_new = jnp.int32(0)
    lane = lax.broadcasted_iota(I32, (1, PAGE_SIZE), 1)
    for t in range(SEL_TOPK):
        j = idx_ref[ibase + t]
        n_new = n_new + (j == n_past).astype(I32)
        kb = page_refs[t][0, pl.ds(2 * NSA_KV + g, PAGE_SIZE, stride=8), :].astype(BF16)
        vb = page_refs[t][0, pl.ds(3 * NSA_KV + g, PAGE_SIZE, stride=8), :].astype(BF16)
        kpos = (j // 2) * PAGE_SIZE + lane
        s = _dot_nt(q8, kb) * scale - slope * (P - kpos).astype(F32)
        jv = jnp.full((1, PAGE_SIZE), j, I32)
        carry = _online_update(carry, s, ((kpos // SEL_BLOCK) == jv) & (jv < n_past), vb)

    has_new = jnp.full((NSA_HEADS, 1), n_new, I32) > 0
    k_new = new_ref[0, pl.ds(2 * NSA_KV + g, 1), :]
    v_new = new_ref[0, pl.ds(3 * NSA_KV + g, 1), :]
    _, l_s, acc_s = add_token(carry, vec_dot(k_new) * scale, has_new, v_new)
    o_s = acc_s / l_s

    kw = win_ref[0, pl.ds(g, WB, stride=2 * NSA_KV), :].astype(BF16)
    vw = win_ref[0, pl.ds(NSA_KV + g, WB, stride=2 * NSA_KV), :].astype(BF16)
    d = WB - lax.broadcasted_iota(I32, (1, WB), 1)
    s = _dot_nt(q8, kw) * scale - slope * d.astype(F32)
    mask = d < WINDOW
    s = jnp.where(mask, s, NEG_INF)
    m = jnp.max(s, axis=1, keepdims=True)
    p = jnp.where(mask, jnp.exp(s - m), 0.0)
    carry = (m, jnp.sum(p, axis=1, keepdims=True), _dot(p.astype(BF16), vw))
    kw_new = new_ref[0, pl.ds(4 * NSA_KV + g, 1), :]
    vw_new = new_ref[0, pl.ds(5 * NSA_KV + g, 1), :]
    _, l_w, acc_w = add_token(carry, vec_dot(kw_new) * scale, rowi >= 0, vw_new)
    o_w = acc_w / l_w

    smb = jnp.broadcast_to(sm_ref[0], (NSA_HEADS, 128))
    lane_h = lax.broadcasted_iota(I32, (NSA_HEADS, 128), 1)

    def gate(c):
        return _sigmoid(jnp.sum(jnp.where(lane_h == rowi * 3 + c, smb, 0.0), axis=1, keepdims=True))

    o_ref[0, 0] = gate(0) * oc_ref[0] + gate(1) * o_s + gate(2) * o_w


def sample_sel_win(idx, page_table, pool, win, layer, n_pool, q8, newkv, oc, sm, slopes):
    DB, NP = page_table.shape
    WB = win.shape[1] // (2 * NSA_KV)
    n_past = NP * PAGE_SIZE // SEL_BLOCK
    pbase = layer * n_pool
    wbase = layer * DB
    idx_flat = idx[:, :NSA_KV, :SEL_TOPK].reshape(-1)

    def page_spec(t):
        def page_map(b, g, idx_r, pt_r):
            j = jnp.minimum(idx_r[(b * NSA_KV + g) * SEL_TOPK + t], n_past - 1)
            return (pbase + pt_r[b * NP + j // 2], 0, 0)
        return pl.BlockSpec((1, ROWS_PER_PAGE, HD), page_map)

    grid_spec = pltpu.PrefetchScalarGridSpec(
        num_scalar_prefetch=2,
        grid=(DB, NSA_KV),
        in_specs=[pl.BlockSpec(memory_space=pltpu.SMEM)] + [page_spec(t) for t in range(SEL_TOPK)] + [
                  pl.BlockSpec((1, WB * 2 * NSA_KV, HD), lambda b, g, i_, p_: (wbase + b, 0, 0)),
                  pl.BlockSpec((1, NSA_HEADS, HD), lambda b, g, i_, p_: (b, 0, 0)),
                  pl.BlockSpec((1, 6 * NSA_KV, HD), lambda b, g, i_, p_: (b, 0, 0)),
                  pl.BlockSpec((1, NSA_HEADS, HD), lambda b, g, i_, p_: (b, 0, 0)),
                  pl.BlockSpec((1, 1, 128), lambda b, g, i_, p_: (b, 0, 0))],
        out_specs=pl.BlockSpec((1, 1, NSA_HEADS, HD), lambda b, g, i_, p_: (b, g, 0, 0)),
    )
    return pl.pallas_call(
        functools.partial(_sample_sel_body, NP=NP, WB=WB),
        grid_spec=grid_spec,
        out_shape=jax.ShapeDtypeStruct((DB, NSA_KV, NSA_HEADS, HD), F32),
        compiler_params=_cp("arbitrary", "arbitrary"),
        name="sample_sel_win",
    )(idx_flat, page_table.reshape(-1), slopes, *([pool] * SEL_TOPK), win, q8, newkv, oc, sm)


def _mlstm_step_body(bif_ref, gs_ref, ms_ref, qkv_ref, mo_ref, ng_ref, c_ref, n_ref,
                     o_ref, co_ref, no_ref, mo_out_ref):
    b = pl.program_id(0)
    H = M_HEADS
    ri = lax.broadcasted_iota(I32, (HD, HD), 0)
    ci = lax.broadcasted_iota(I32, (HD, HD), 1)
    for h in range(H):
        ig = jnp.full((1, HD), gs_ref[b, h] + bif_ref[0, h], F32)
        fp = jnp.full((1, HD), gs_ref[b, H + h] + bif_ref[1, h], F32)
        m = jnp.full((1, HD), ms_ref[b, h], F32)
        lf = -_softplus(-fp)
        inter = lf + m
        m_t = jnp.maximum(inter, ig)
        w_intra = jnp.exp(ig - m_t)
        w_inter = jnp.exp(inter - m_t)
        q = qkv_ref[0, h:h + 1, :]
        k = qkv_ref[0, H + h:H + h + 1, :] * (HD ** -0.5)
        v = qkv_ref[0, 2 * H + h:2 * H + h + 1, :]
        C = c_ref[0, h]
        n = n_ref[0, h:h + 1, :]
        s = jnp.sum(q * k, axis=1, keepdims=True) * w_intra
        qC = _dot(jnp.broadcast_to(q, (8, HD)).astype(BF16), C.astype(BF16))[0:1]
        num = s * v + w_inter * qC
        den = s + w_inter * jnp.sum(q * n, axis=1, keepdims=True)
        hh = num / jnp.maximum(jnp.abs(den), jnp.exp(-m_t))
        k_col = jnp.sum(jnp.where(ri == ci, jnp.broadcast_to(k, (HD, HD)), 0.0), axis=1, keepdims=True)
        co_ref[0, h] = w_inter * C + (k_col * w_intra) * v
        no_ref[0, h:h + 1, :] = w_inter * n + w_intra * k
        mo_out_ref[0, h:h + 1, :] = m_t
        mu = jnp.mean(hh, axis=1, keepdims=True)
        hc = hh - mu
        var = jnp.mean(hc * hc, axis=1, keepdims=True)
        hn = hc * lax.rsqrt(var + LN_EPS) * ng_ref[h:h + 1, :]
        o_ref[0, h:h + 1, :] = _sigmoid(mo_ref[0, h:h + 1, :]) * hn


def mlstm_step(b_if, gates, m_state, qkv, mo, norm_g, C, n):
    DB = qkv.shape[0]
    H = M_HEADS
    smem = pl.BlockSpec(memory_space=pltpu.SMEM)
    row = pl.BlockSpec((1, H, HD), lambda b: (b, 0, 0))
    cspec = pl.BlockSpec((1, H, HD, HD), lambda b: (b, 0, 0, 0))
    return pl.pallas_call(
        _mlstm_step_body,
        grid=(DB,),
        in_specs=[smem, smem, smem, pl.BlockSpec((1, 3 * H, HD), lambda b: (b, 0, 0)), row,
                  pl.BlockSpec((H, HD), lambda b: (0, 0)), cspec, row],
        out_specs=[row, cspec, row, row],
        out_shape=[jax.ShapeDtypeStruct((DB, H, HD), F32), jax.ShapeDtypeStruct((DB, H, HD, HD), F32),
                   jax.ShapeDtypeStruct((DB, H, HD), F32), jax.ShapeDtypeStruct((DB, H, HD), F32)],
        compiler_params=_cp("arbitrary"),
        name="mlstm_step",
    )(b_if, gates, m_state, qkv, mo, norm_g.reshape(H, HD), C, n)


def _rglru_step_body(rx_ref, rgt_ref, buf_ref, h_ref, cw_ref, vec_ref, wa_ref, wx_ref, o_ref, ho_ref):
    cw = cw_ref[...]
    vec = vec_ref[...]
    xc = vec[0:1] + rx_ref[...] * cw[CONV_W - 1:CONV_W]
    for j in range(CONV_W - 1):
        xc = xc + buf_ref[j] * cw[j:j + 1]
    a, u = _rglru_gates(xc, wa_ref[0].astype(BF16), wx_ref[0].astype(BF16), vec)
    h = a * h_ref[...] + u
    ho_ref[...] = h
    o_ref[...] = h * _gelu(rgt_ref[...])


def rglru_step(rx, rgt, buf_t, h0, conv_w, vec, w_a, w_x):
    DB = rx.shape[0]
    col = pl.BlockSpec((DB, 128), lambda n: (0, n))
    return pl.pallas_call(
        _rglru_step_body,
        grid=(R_BLOCKS,),
        in_specs=[col, col, pl.BlockSpec((CONV_W - 1, DB, 128), lambda n: (0, 0, n)), col,
                  pl.BlockSpec((CONV_W, 128), lambda n: (0, n)),
                  pl.BlockSpec((4, 128), lambda n: (0, n)),
                  pl.BlockSpec((1, 128, 128), lambda n: (n, 0, 0)),
                  pl.BlockSpec((1, 128, 128), lambda n: (n, 0, 0))],
        out_specs=[col, col],
        out_shape=[jax.ShapeDtypeStruct((DB, R_WIDTH), F32), jax.ShapeDtypeStruct((DB, R_WIDTH), F32)],
        compiler_params=_cp("arbitrary"),
        name="rglru_step",
    )(rx, rgt, buf_t, h0, conv_w, vec, w_a, w_x)


def pack_w_in(w):
    pad = jnp.zeros(w.shape[:2] + (N_AL - 14888,), w.dtype)
    parts = [w[..., 8744:14888], w[..., 0:1024], w[..., 2584:5656], w[..., 5672:6696], w[..., 6696:7720],
             w[..., 7720:8744], w[..., 1024:2560], w[..., 2560:2584], w[..., 5656:5672], pad]
    return jnp.concatenate(parts, axis=-1).astype(BF16)


def prompt_layer(xp, lw, gw, layer, kv_buf, depth):
    B, T, D = xp.shape
    x2 = xp.reshape(B * T, D)
    M = B * T
    z = mm(x2, gw["w_in"], layer, tm=min(IN_TM, M), tn=IN_TN, out_dtype=F32)
    z3 = z.reshape(B, T, N_AL)
    kv_buf = kv_rows(z, kv_buf, layer, depth)
    kc, vc, cb = compress_from_z(z3, lw["phi_w1"], lw["pe"], lw["phi_b1"], lw["phi_w2"])
    o_a = nsa_prompt(z3, kc, vc, lw["slopes"])
    o_b, Cp, np_, mp = mlstm_prompt(z3, lw["b_if"], lw["norm_g"])
    o_c, hp = rglru_prompt(z3, lw["conv_w"], lw["rg_vec"], lw["w_a"], lw["w_x"])
    merged = merge(o_a.reshape(B * T, -1), o_b.reshape(B * T, -1), o_c.reshape(B * T, -1), z,
                   gw["w_branch"], layer, tm=min(MERGE_TM, M), tn=MERGE_TN, out_dtype=BF16)
    h = mm_res_ln(merged, gw["w_out"], layer, x2, lw["ln_g"][0:1], lw["ln_b"][0:1], tm=min(OUT_TM, M), tk=D)
    f1 = mm(h, gw["mlp_w1"], layer, tm=min(UP_TM, M), tn=UP_TN, out_dtype=BF16, act="relu2")
    x_new = mm_res_ln(f1, gw["mlp_w2"], layer, h, lw["ln_g"][1:2], lw["ln_b"][1:2], tm=min(DOWN_TM, M),
                      tk=DOWN_TK)
    n_win = min(WINDOW, T)
    wkv = z3[:, T - n_win:, OFF_KV + 4 * NSA_KV * HD:OFF_KV + 6 * NSA_KV * HD]
    states = (wkv.reshape(B, n_win, 2, NSA_KV, HD), Cp, np_[:, :, 0], mp[:, :, 0, 0], hp[:, 0],
              z3[:, T - (CONV_W - 1):, OFF_RX:OFF_RX + R_WIDTH])
    return x_new.reshape(B, T, D), states, cb, kv_buf


def sample_layer(xs, lw, gw, cb, layer, page_table, pool, n_pool, win, win_l, C0, n0, m0, h0, conv0):
    DB, D = xs.shape
    z = mm(xs, gw["w_in"], layer, tm=DB, tn=IN_TN, out_dtype=F32)
    q8 = z[:, OFF_Q:OFF_Q + NSA_WIDTH].reshape(DB, NSA_HEADS, HD)
    newkv = z[:, OFF_KV:OFF_KV + 6 * NSA_KV * HD].reshape(DB, 6 * NSA_KV, HD)
    sm = z[:, OFF_SM:OFF_SM + 128].reshape(DB, 1, 128)
    oc, idx = sample_cmp_select(page_table, pool, layer, n_pool, lw["w1c"], cb, lw["phi_w2"], q8,
                                lw["slopes"])
    osw = sample_sel_win(idx, page_table, pool, win, layer, n_pool, q8, newkv, oc, sm, lw["slopes"])
    o_a = jnp.concatenate([osw[:, g, g * NSA_GROUP:(g + 1) * NSA_GROUP] for g in range(NSA_KV)],
                          axis=1).reshape(DB, NSA_WIDTH)
    qkv = z[:, OFF_MQKV:OFF_MQKV + 3 * M_WIDTH].reshape(DB, 3 * M_HEADS, HD)
    mo = z[:, OFF_MO:OFF_MO + M_WIDTH].reshape(DB, M_HEADS, HD)
    gates = z[:, OFF_SM + SM_I:OFF_SM + SM_I + 2 * M_HEADS]
    o_b, Cs, ns, ms = mlstm_step(lw["b_if"], gates, m0, qkv, mo, lw["norm_g"], C0, n0)
    rx = z[:, OFF_RX:OFF_RX + R_WIDTH]
    o_c, hs = rglru_step(rx, z[:, OFF_RGT:OFF_RGT + R_WIDTH], conv0.transpose(1, 0, 2), h0,
                         lw["conv_w"], lw["rg_vec"], lw["w_a"], lw["w_x"])
    merged = merge(o_a, o_b.reshape(DB, M_WIDTH), o_c, z, gw["w_branch"], layer, tm=DB, tn=MERGE_TN,
                   out_dtype=F32)
    h = mm_res_ln(merged, gw["w_out"], layer, xs, lw["ln_g"][0:1], lw["ln_b"][0:1], tm=DB, tk=D)
    f1 = mm(h, gw["mlp_w1"], layer, tm=DB, tn=UP_TN, out_dtype=F32, act="relu2")
    x_new = mm_res_ln(f1, gw["mlp_w2"], layer, h, lw["ln_g"][1:2], lw["ln_b"][1:2], tm=DB, tk=DOWN_TK)
    kvn = newkv.reshape(DB, 1, 6, NSA_KV, HD)
    Wb = win_l.shape[1]
    keep = min(WINDOW, Wb + 1)
    win_new = jnp.concatenate([win_l, kvn[:, :, 4:6]], axis=1)[:, Wb + 1 - keep:]
    conv_new = jnp.concatenate([conv0[:, 1:], rx[:, None]], axis=1)
    states = (kvn[:, :, :4], win_new, Cs, ns, ms[:, :, 0], hs, conv_new)
    return x_new, states


def kernel(x_prompt, x_sample, cache_nsa_kv, cache_win_kv, state_mlstm_C, state_mlstm_n, state_mlstm_m,
           state_rglru_h, state_rglru_conv, page_table, w_in, nsa_pe, nsa_phi_w1, nsa_phi_b1, nsa_phi_w2,
           mlstm_b_if, mlstm_norm_g, rg_conv_w, rg_conv_b, rg_w_a, rg_b_a, rg_w_x, rg_b_x, rg_lambda,
           w_branch, w_out, ln_g, ln_b, mlp_w1, mlp_w2):
    DB, Tn, D = x_sample.shape
    assert Tn == 1 and D == D_MODEL
    depth, n_pool = cache_nsa_kv.shape[:2]
    pool = cache_nsa_kv.reshape(depth * n_pool, ROWS_PER_PAGE, HD)
    Wb = cache_win_kv.shape[2]
    win = cache_win_kv.reshape(depth * DB, Wb * 2 * NSA_KV, HD)
    slopes = alibi_slopes()
    half = CMP_STRIDE * HD
    B, T = x_prompt.shape[:2]
    xp = x_prompt
    xs = x_sample.reshape(DB, D)
    gw = {"w_in": pack_w_in(w_in), "w_branch": w_branch.astype(BF16), "w_out": w_out.astype(BF16),
          "mlp_w1": mlp_w1.astype(BF16), "mlp_w2": mlp_w2.astype(BF16)}
    new_p = [[] for _ in range(6)]
    new_s = [[] for _ in range(7)]
    kv_buf = None
    for l in range(depth):
        lw = {
            "pe": nsa_pe[l], "phi_w1": nsa_phi_w1[l], "phi_b1": nsa_phi_b1[l], "phi_w2": nsa_phi_w2[l],
            "w1c": jnp.concatenate([nsa_phi_w1[l][:, :half], nsa_phi_w1[l][:, half:]], axis=2).astype(BF16),
            "slopes": slopes, "b_if": mlstm_b_if[l], "norm_g": mlstm_norm_g[l],
            "conv_w": rg_conv_w[l],
            "rg_vec": jnp.stack([rg_conv_b[l], rg_b_a[l], rg_b_x[l], rg_lambda[l]]),
            "w_a": rg_w_a[l], "w_x": rg_w_x[l],
            "ln_g": ln_g[l], "ln_b": ln_b[l],
        }
        xp, st_p, cb, kv_buf = prompt_layer(xp, lw, gw, l, kv_buf, depth)
        xs, st_s = sample_layer(xs, lw, gw, cb, l, page_table, pool, n_pool, win, cache_win_kv[l],
                                state_mlstm_C[l], state_mlstm_n[l], state_mlstm_m[l],
                                state_rglru_h[l], state_rglru_conv[l])
        for lst, val in zip(new_p, st_p):
            lst.append(val)
        for lst, val in zip(new_s, st_s):
            lst.append(val)
    P = [jnp.stack(a) for a in new_p]
    S = [jnp.stack(a) for a in new_s]
    nsa_kv_p = kv_buf.reshape(depth, B, T, 4, NSA_KV, HD)
    return (xp, xs.reshape(DB, Tn, D), nsa_kv_p, S[0], P[0], S[1], P[1], S[2], P[2], S[3], P[3], S[4],
            P[4], S[5], P[5], S[6])
```

```python
import functools

import jax
import jax.numpy as jnp
from jax import lax
from jax.experimental import pallas as pl
from jax.experimental.pallas import tpu as pltpu

F32 = jnp.float32
BF16 = jnp.bfloat16
I32 = jnp.int32

D_MODEL = 2048
DEPTH = 4
PAGE_SIZE = 128
HD = 128
NSA_HEADS = 8
NSA_KV = 2
NSA_GROUP = 4
NSA_WIDTH = NSA_HEADS * HD
CMP_BLOCK = 32
CMP_STRIDE = 16
SEL_BLOCK = 64
SEL_TOPK = 16
WINDOW = 512
Q_BLOCK = 128
FORCE_BONUS = 1.0e4
M_HEADS = 8
M_WIDTH = M_HEADS * HD
R_WIDTH = 1024
R_BLOCKS = 8
CONV_W = 4
LRU_C = 8.0
D_FF = 4 * D_MODEL
DEEPNORM_ALPHA = (2 * DEPTH) ** 0.25
LN_EPS = 1e-5
NEG_INF = -1e30

OFF_MG = 0
OFF_Q = 6144
OFF_MQKV = 7168
OFF_MO = 10240
OFF_RX = 11264
OFF_RGT = 12288
OFF_KV = 13312
OFF_SM = 14848
N_AL = 15360
SM_I = 24
SM_F = 32

IN_TM, IN_TN = 1024, 1536
MERGE_TM, MERGE_TN = 256, 2048
OUT_TM = 256
UP_TM, UP_TN = 1024, 1024
DOWN_TM, DOWN_TK = 512, 2048

VMEM_LIMIT = 52 * 1024 * 1024


def _cp(*sem):
    return pltpu.CompilerParams(dimension_semantics=sem, vmem_limit_bytes=VMEM_LIMIT)


def _dot(a, b):
    return jnp.dot(a, b, preferred_element_type=F32)


def _dot_nt(a, b):
    return lax.dot_general(a, b, (((1,), (1,)), ((), ())), preferred_element_type=F32)


def _gelu(x):
    return 0.5 * x * (1.0 + jnp.tanh(0.7978845608028654 * (x + 0.044715 * (x * x * x))))


def _sigmoid(x):
    return 0.5 * jnp.tanh(0.5 * x) + 0.5


def _softplus(x):
    return jnp.maximum(x, 0.0) + jnp.log1p(jnp.exp(-jnp.abs(x)))


def _mm_body(a_ref, b_ref, o_ref, *, act):
    acc = _dot(a_ref[...].astype(BF16), b_ref[...])
    if act == "relu2":
        acc = jnp.square(jnp.maximum(acc, 0.0))
    o_ref[...] = acc.astype(o_ref.dtype)


def mm(a, b, layer, *, tm, tn, out_dtype, act=None):
    M, K = a.shape
    N = b.shape[2]
    assert M % tm == 0 and N % tn == 0
    return pl.pallas_call(
        functools.partial(_mm_body, act=act),
        grid=(M // tm, N // tn),
        in_specs=[pl.BlockSpec((tm, K), lambda i, j: (i, 0)),
                  pl.BlockSpec((None, K, tn), lambda i, j: (layer, 0, j))],
        out_specs=pl.BlockSpec((tm, tn), lambda i, j: (i, j)),
        out_shape=jax.ShapeDtypeStruct((M, N), out_dtype),
        compiler_params=_cp("parallel", "arbitrary"),
        name="mm",
    )(a, b)


def _mm_ln_body(a_ref, b_ref, x_ref, g_ref, bb_ref, o_ref, acc_a, acc_b, *, nk, nblk):
    i = pl.program_id(0)
    k = pl.program_id(1)
    even = (i % 2) == 0

    def prod():
        return _dot(a_ref[...].astype(BF16), b_ref[...])

    def finish(acc):
        y = DEEPNORM_ALPHA * x_ref[...] + acc[...]
        mu = jnp.mean(y, axis=-1, keepdims=True)
        yc = y - mu
        var = jnp.mean(yc * yc, axis=-1, keepdims=True)
        o_ref[...] = yc * lax.rsqrt(var + LN_EPS) * g_ref[...] + bb_ref[...]

    @pl.when((k == 0) & (i == 0))
    def _():
        acc_a[...] = prod()

    @pl.when((k == 0) & (i > 0) & (i < nblk) & even)
    def _():
        finish(acc_b)
        acc_a[...] = prod()

    @pl.when((k == 0) & (i > 0) & (i < nblk) & jnp.logical_not(even))
    def _():
        finish(acc_a)
        acc_b[...] = prod()

    @pl.when((k == 0) & (i == nblk))
    def _():
        finish(acc_a if (nblk - 1) % 2 == 0 else acc_b)

    if nk > 1:
        @pl.when((k > 0) & (i < nblk) & even)
        def _():
            acc_a[...] += prod()

        @pl.when((k > 0) & (i < nblk) & jnp.logical_not(even))
        def _():
            acc_b[...] += prod()


def mm_res_ln(a, b, layer, x, g, bb, *, tm, tk):
    M, K = a.shape
    N = b.shape[2]
    nk = K // tk
    nblk = M // tm
    assert M % tm == 0 and K % tk == 0

    def a_map(i, k):
        last = i == nblk
        return (jnp.where(last, nblk - 1, i), jnp.where(last, nk - 1, k))

    def b_map(i, k):
        return (layer, jnp.where(i == nblk, nk - 1, k), 0)

    def lag_map(i, k):
        return (jnp.maximum(i - 1, 0), 0)

    return pl.pallas_call(
        functools.partial(_mm_ln_body, nk=nk, nblk=nblk),
        grid=(nblk + 1, nk),
        in_specs=[pl.BlockSpec((tm, tk), a_map),
                  pl.BlockSpec((None, tk, N), b_map),
                  pl.BlockSpec((tm, N), lag_map),
                  pl.BlockSpec((1, N), lambda i, k: (0, 0)),
                  pl.BlockSpec((1, N), lambda i, k: (0, 0))],
        out_specs=pl.BlockSpec((tm, N), lag_map),
        out_shape=jax.ShapeDtypeStruct((M, N), F32),
        scratch_shapes=[pltpu.VMEM((tm, N), F32), pltpu.VMEM((tm, N), F32)],
        compiler_params=_cp("arbitrary", "arbitrary"),
        name="mm_res_ln",
    )(a, b, x, g, bb)


def _merge_body(oa_ref, ob_ref, oc_ref, g0_ref, g1_ref, g2_ref, w_ref, o_ref):
    acc = _sigmoid(g0_ref[...]) * _dot(oa_ref[...].astype(BF16), w_ref[0])
    acc += _sigmoid(g1_ref[...]) * _dot(ob_ref[...].astype(BF16), w_ref[1])
    acc += _sigmoid(g2_ref[...]) * _dot(oc_ref[...].astype(BF16), w_ref[2])
    o_ref[...] = acc.astype(o_ref.dtype)


def merge(o_a, o_b, o_c, z, w_branch, layer, *, tm, tn, out_dtype):
    M, W = o_a.shape
    N = w_branch.shape[3]
    assert OFF_MG % tn == 0 and N % tn == 0 and M % tm == 0
    gb = OFF_MG // tn
    nb = N // tn
    br = pl.BlockSpec((tm, W), lambda i, j: (i, 0))

    def gspec(k):
        return pl.BlockSpec((tm, tn), lambda i, j: (i, gb + k * nb + j))

    return pl.pallas_call(
        _merge_body,
        grid=(M // tm, nb),
        in_specs=[br, br, br, gspec(0), gspec(1), gspec(2),
                  pl.BlockSpec((None, 3, W, tn), lambda i, j: (layer, 0, 0, j))],
        out_specs=pl.BlockSpec((tm, tn), lambda i, j: (i, j)),
        out_shape=jax.ShapeDtypeStruct((M, N), out_dtype),
        compiler_params=_cp("parallel", "arbitrary"),
        name="merge",
    )(o_a, o_b, o_c, z, z, z, w_branch)


def _kv_rows_body(z_ref, *rest):
    o_ref = rest[-1]
    tm = z_ref.shape[0]
    for kg in range(4 * NSA_KV):
        o_ref[pl.ds(kg, tm, stride=4 * NSA_KV), :] = z_ref[:, kg * HD:(kg + 1) * HD]


def kv_rows(z, buf, layer, depth):
    M = z.shape[0]
    tm = min(512, M)
    nb = M // tm
    W = 4 * NSA_KV * HD
    assert OFF_KV % W == 0 and M % tm == 0
    in_specs = [pl.BlockSpec((tm, W), lambda i: (i, OFF_KV // W))]
    args = [z]
    aliases = {}
    if buf is not None:
        in_specs.append(pl.BlockSpec(memory_space=pl.ANY))
        args.append(buf)
        aliases = {1: 0}
    return pl.pallas_call(
        _kv_rows_body,
        grid=(nb,),
        in_specs=in_specs,
        out_specs=pl.BlockSpec((tm * 4 * NSA_KV, HD), lambda i: (layer * nb + i, 0)),
        out_shape=jax.ShapeDtypeStruct((depth * M * 4 * NSA_KV, HD), F32),
        input_output_aliases=aliases,
        compiler_params=_cp("arbitrary"),
        name="kv_rows",
    )(*args)


def _compress_body(z_ref, w1_ref, pe_ref, b1_ref, w2_ref, o_ref, cb_ref, *, n):
    half = CMP_STRIDE * HD
    x = jnp.concatenate([z_ref[0, pl.ds(c, n, stride=CMP_STRIDE), :] for c in range(CMP_STRIDE)],
                        axis=1).astype(BF16)
    w1 = w1_ref[0].astype(BF16)
    lead = _dot(x, w1[:half])
    trail = _dot(x, w1[half:])
    pe8 = jnp.broadcast_to(pe_ref[0], (8, 2 * half)).astype(BF16)
    cb = _dot(pe8, w1) + b1_ref[0]
    trail_next = pltpu.roll(trail, n - 1, axis=0)
    hid = _gelu(lead + trail_next + cb[0:1])
    o_ref[0, 0] = _dot(hid.astype(BF16), w2_ref[0].astype(BF16))
    cb_ref[0] = cb


def compress_from_z(z3, w1, pe, b1, w2):
    B, T, _ = z3.shape
    G = NSA_KV
    S = B * G
    n = T // CMP_STRIDE
    K = CMP_STRIDE * HD
    kvb = OFF_KV // HD
    out, cb = pl.pallas_call(
        functools.partial(_compress_body, n=n),
        grid=(2, S),
        in_specs=[pl.BlockSpec((1, T, HD), lambda k, s: (s // G, 0, kvb + k * G + s % G)),
                  pl.BlockSpec((1, 2 * K, HD), lambda k, s: (k, 0, 0)),
                  pl.BlockSpec((1, 1, 2 * K), lambda k, s: (k, 0, 0)),
                  pl.BlockSpec((1, 1, HD), lambda k, s: (k, 0, 0)),
                  pl.BlockSpec((1, HD, HD), lambda k, s: (k, 0, 0))],
        out_specs=[pl.BlockSpec((1, 1, n, HD), lambda k, s: (k, s, 0, 0)),
                   pl.BlockSpec((1, 8, HD), lambda k, s: (k, 0, 0))],
        out_shape=[jax.ShapeDtypeStruct((2, S, n, HD), F32),
                   jax.ShapeDtypeStruct((2, 8, HD), F32)],
        compiler_params=_cp("arbitrary", "arbitrary"),
        name="compress_prompt",
    )(z3, w1, pe.reshape(2, 1, CMP_BLOCK * HD), b1.reshape(2, 1, HD), w2)
    return out[0], out[1], cb


def alibi_slopes():
    return jnp.asarray([2.0 ** (-8.0 * (h + 1) / NSA_HEADS) for h in range(NSA_HEADS)], F32)


SEL_CHUNK = 512


def _online_update(carry, s, mask, v):
    m, l, acc = carry
    s = jnp.where(mask, s, NEG_INF)
    m_new = jnp.maximum(m, jnp.max(s, axis=1, keepdims=True))
    alpha = jnp.exp(m - m_new)
    p = jnp.where(mask, jnp.exp(s - m_new), 0.0)
    l = alpha * l + jnp.sum(p, axis=1, keepdims=True)
    acc = alpha * acc + _dot(p.astype(BF16), v)
    return m_new, l, acc


MASK_BIG = 1.0e30
LOG2E = 1.4426950408889634
WIN_KEYS = WINDOW + Q_BLOCK


def _nsa_prompt_body(slopes_ref, q_ref, ks_ref, vs_ref, kw_ref, vw_ref, kc_ref, vc_ref,
                     sm_ref, o_ref, ks_bf, vs_aug, et_bf, kw_pad, vw_aug, wbias, sbias, y_a, y_b,
                     clist, *, T):
    assert T % (2 * SEL_CHUNK) == 0
    g = pl.program_id(1)
    i = pl.program_id(2)
    QB = Q_BLOCK
    R = NSA_GROUP
    RQ = R * QB
    NCP = T // CMP_STRIDE
    NC = NCP - 1
    NS = T // SEL_BLOCK
    NSP = 128
    assert NS <= NSP
    t0 = i * QB
    scale = HD ** -0.5
    c2 = scale * LOG2E

    @pl.when(i == 0)
    def _():
        lane = lax.broadcasted_iota(I32, (T, HD), 1)
        ones_col = jnp.where(lane == 0, 1.0, 0.0).astype(BF16)
        ks_bf[...] = ks_ref[0].astype(BF16)
        vs_aug[:, 0:HD] = vs_ref[0].astype(BF16)
        vs_aug[:, HD:2 * HD] = ones_col
        kk = lax.broadcasted_iota(I32, (T, HD), 0)
        et_bf[...] = jnp.where(kk // SEL_BLOCK == lane, MASK_BIG, 0.0).astype(BF16)
        kw_pad[0:WINDOW, :] = jnp.zeros((WINDOW, HD), BF16)
        kw_pad[WINDOW:WINDOW + T, :] = kw_ref[0].astype(BF16)
        vw_aug[0:WINDOW, :] = jnp.zeros((WINDOW, 2 * HD), BF16)
        vw_aug[WINDOW:WINDOW + T, 0:HD] = vw_ref[0].astype(BF16)
        vw_aug[WINDOW:WINDOW + T, HD:2 * HD] = ones_col
        wd = (lax.broadcasted_iota(I32, (QB, WIN_KEYS), 0) + WINDOW
              - lax.broadcasted_iota(I32, (QB, WIN_KEYS), 1))
        band = (wd >= 0) & (wd < WINDOW)
        wdf = wd.astype(F32)
        lf = lax.broadcasted_iota(I32, (QB, SEL_CHUNK), 1).astype(F32)
        for r in range(R):
            sl = slopes_ref[g * R + r]
            wbias[r * QB:(r + 1) * QB, :] = jnp.where(band, (-LOG2E * sl) * wdf, -MASK_BIG)
            sbias[r * QB:(r + 1) * QB, :] = (LOG2E * sl) * lf

    qb = q_ref[0]
    q4 = jnp.concatenate([qb[:, r * HD:(r + 1) * HD] for r in range(R)], axis=0).astype(BF16)
    row = lax.broadcasted_iota(I32, (RQ, 1), 0)
    qpos = t0 + (row & (QB - 1))
    slope = jnp.concatenate(
        [jnp.full((QB, 1), slopes_ref[g * R + r], F32) for r in range(R)], axis=0)

    kc = kc_ref[0, 0].astype(BF16)
    vc = vc_ref[0, 0].astype(BF16)
    n_idx = lax.broadcasted_iota(I32, (1, NCP), 1)
    dist = qpos - (n_idx * CMP_STRIDE + (CMP_BLOCK - 1))
    mask = (dist >= 0) & (n_idx < NC)
    s = _dot_nt(q4, kc) * scale - slope * dist.astype(F32)
    s = jnp.where(mask, s, NEG_INF)
    m = jnp.max(s, axis=1, keepdims=True)
    p = jnp.where(mask, jnp.exp(s - m), 0.0)
    l = jnp.sum(p, axis=1, keepdims=True)
    p = p * (1.0 / jnp.where(l > 0.0, l, 1.0))
    o_c = _dot(p.astype(BF16), vc)

    psum = p[0:QB]
    for r in range(1, R):
        psum = psum + p[r * QB:(r + 1) * QB]
    p_hi = psum.astype(BF16)
    p_lo = (psum - p_hi.astype(F32)).astype(BF16)
    jo = lax.broadcasted_iota(I32, (NS, NCP), 0) * SEL_BLOCK
    no = lax.broadcasted_iota(I32, (NS, NCP), 1)
    ov = ((no * CMP_STRIDE < jo + SEL_BLOCK) & (no * CMP_STRIDE + CMP_BLOCK > jo)
          & (no < NC)).astype(BF16)
    imp = _dot_nt(ov, p_hi) + _dot_nt(ov, p_lo)
    jj = lax.broadcasted_iota(I32, (NS, QB), 0)
    qp = t0 + lax.broadcasted_iota(I32, (NS, QB), 1)
    cur = qp // SEL_BLOCK
    forced = (jj == 0) | (jj == cur) | (jj == cur - 1)
    imp = jnp.where(forced, imp + FORCE_BONUS, imp)
    imp = jnp.where(jj * SEL_BLOCK <= qp, imp, -1.0)
    ranks = [jnp.zeros((8, QB), F32) for _ in range(NS // 8)]
    for j2 in range(NS):
        rv = imp[j2:j2 + 1, :]
        for v in range(NS // 8):
            blk = imp[8 * v:8 * v + 8]
            if 8 * v > j2:
                beats = rv >= blk
            elif 8 * v + 7 < j2:
                beats = rv > blk
            else:
                beats = (rv > blk) | ((rv == blk) & (jj[8 * v:8 * v + 8] > j2))
            ranks[v] = ranks[v] + jnp.where(beats, 1.0, 0.0)
    rank = jnp.concatenate(ranks, axis=0)
    selm_t = jnp.where(rank < SEL_TOPK, 0.0, -1.0)
    if NSP > NS:
        selm_t = jnp.concatenate([selm_t, jnp.zeros((NSP - NS, QB), F32)], axis=0)
    selm = selm_t.T.astype(BF16)

    sl_col = LOG2E * slope
    rel = (lax.broadcasted_iota(I32, (QB, SEL_CHUNK), 1)
           - lax.broadcasted_iota(I32, (QB, SEL_CHUNK), 0))

    def scores(c):
        k0 = pl.multiple_of(c * SEL_CHUNK, SEL_CHUNK)
        mb = _dot_nt(selm, et_bf[pl.ds(k0, SEL_CHUNK), :])
        mb = jnp.where(rel <= t0 - k0, mb, -MASK_BIG)
        return (_dot_nt(q4, ks_bf[pl.ds(k0, SEL_CHUNK), :]) * c2 + sbias[...]
                + jnp.concatenate([mb] * R, axis=0))

    def consume(c, y, carry):
        m, acc = carry
        k0 = pl.multiple_of(c * SEL_CHUNK, SEL_CHUNK)
        off = sl_col * (k0 - t0).astype(F32)
        m_new = jnp.maximum(m, jnp.max(y, axis=1, keepdims=True) + off)
        p = jnp.exp2(y - (m_new - off))
        acc = jnp.exp2(m - m_new) * acc + _dot(p.astype(BF16), vs_aug[pl.ds(k0, SEL_CHUNK), :])
        return m_new, acc

    n_chunks = (t0 + QB + SEL_CHUNK - 1) // SEL_CHUNK
    bpc = SEL_CHUNK // SEL_BLOCK
    n_act = jnp.int32(0)
    dummy = jnp.int32(T // SEL_CHUNK - 1)
    for c in reversed(range(T // SEL_CHUNK)):
        act = (jnp.sum(selm_t[c * bpc:(c + 1) * bpc] + 1.0) > 0.0) & (c < n_chunks)
        dummy = jnp.where(act, dummy, c)
    for c in range(T // SEL_CHUNK):
        act = (jnp.sum(selm_t[c * bpc:(c + 1) * bpc] + 1.0) > 0.0) & (c < n_chunks)
        clist[n_act] = c
        n_act = n_act + act.astype(I32)
    clist[n_act] = dummy

    def pair(pi, carry):
        k = 2 * pi
        y_b[...] = scores(clist[k + 1])
        carry = consume(clist[k], y_a[...], carry)
        y_a[...] = scores(clist[jnp.minimum(k + 2, n_act)])
        return consume(clist[k + 1], y_b[...], carry)

    init = (jnp.full((RQ, 1), -MASK_BIG, F32), jnp.zeros((RQ, 2 * HD), F32))
    y_a[...] = scores(clist[0])
    _, acc_s = lax.fori_loop(0, (n_act + 1) // 2, pair, init)
    o_s = acc_s[:, 0:HD] * (1.0 / acc_s[:, HD:HD + 1])

    w0 = pl.multiple_of(t0, QB)
    vrow = jnp.where(lax.broadcasted_iota(I32, (1, WIN_KEYS), 1) >= WINDOW - t0, 0.0, -MASK_BIG)
    y = _dot_nt(q4, kw_pad[pl.ds(w0, WIN_KEYS), :]) * c2 + wbias[...] + vrow
    p = jnp.exp2(y - jnp.max(y, axis=1, keepdims=True))
    acc_w = _dot(p.astype(BF16), vw_aug[pl.ds(w0, WIN_KEYS), :])
    o_w = acc_w[:, 0:HD] * (1.0 / acc_w[:, HD:HD + 1])

    smb = sm_ref[0]
    lane = lax.broadcasted_iota(I32, (QB, 128), 1)

    def gate(c):
        cols = []
        for r in range(R):
            idx = (g * R + r) * 3 + c
            cols.append(jnp.sum(jnp.where(lane == idx, smb, 0.0), axis=1, keepdims=True))
        return _sigmoid(jnp.concatenate(cols, axis=0))

    o = gate(0) * o_c + gate(1) * o_s + gate(2) * o_w
    o_ref[0] = jnp.concatenate([o[r * QB:(r + 1) * QB] for r in range(R)], axis=1).astype(o_ref.dtype)


def nsa_prompt(z3, kc, vc, slopes):
    B, T, _ = z3.shape
    G = NSA_KV
    kvb = OFF_KV // HD

    def kvspec(kind):
        return pl.BlockSpec((1, T, HD), lambda b, g, i: (b, 0, kvb + kind * G + g))

    cspec = pl.BlockSpec((1, 1, T // CMP_STRIDE, HD), lambda b, g, i: (b * G + g, 0, 0, 0))
    GW = NSA_GROUP * HD
    assert OFF_Q % GW == 0
    qspec = pl.BlockSpec((1, Q_BLOCK, GW), lambda b, g, i: (b, i, OFF_Q // GW + g))
    return pl.pallas_call(
        functools.partial(_nsa_prompt_body, T=T),
        grid=(B, G, T // Q_BLOCK),
        in_specs=[pl.BlockSpec(memory_space=pltpu.SMEM),
                  qspec, kvspec(2), kvspec(3), kvspec(4), kvspec(5), cspec, cspec,
                  pl.BlockSpec((1, Q_BLOCK, 128), lambda b, g, i: (b, i, OFF_SM // 128))],
        out_specs=pl.BlockSpec((1, Q_BLOCK, GW), lambda b, g, i: (b, i, g)),
        out_shape=jax.ShapeDtypeStruct((B, T, NSA_WIDTH), BF16),
        scratch_shapes=[pltpu.VMEM((T, HD), BF16),
                        pltpu.VMEM((T, 2 * HD), BF16),
                        pltpu.VMEM((T, HD), BF16),
                        pltpu.VMEM((T + WINDOW, HD), BF16),
                        pltpu.VMEM((T + WINDOW, 2 * HD), BF16),
                        pltpu.VMEM((NSA_GROUP * Q_BLOCK, WIN_KEYS), F32),
                        pltpu.VMEM((NSA_GROUP * Q_BLOCK, SEL_CHUNK), F32),
                        pltpu.VMEM((NSA_GROUP * Q_BLOCK, SEL_CHUNK), F32),
                        pltpu.VMEM((NSA_GROUP * Q_BLOCK, SEL_CHUNK), F32),
                        pltpu.SMEM((T // SEL_CHUNK + 8,), I32)],
        compiler_params=_cp("arbitrary", "arbitrary", "arbitrary"),
        name="nsa_prompt",
    )(slopes, z3, z3, z3, z3, z3, kc[:, None], vc[:, None], z3)


M_CHUNK_K = 128


M_HEADS_PER_STEP = 4
M_ROWS_PER_STEP = 512


def _mlstm_body(bif_ref, q_ref, k_ref, v_ref, sm_ref, mo_ref, ng_ref,
                o_ref, c_ref, n_ref, m_ref, c_sc, n_sc, m_sc, *, NT, TB):
    hb = pl.program_id(1)
    t = pl.program_id(2)
    L = M_CHUNK_K
    HB = M_HEADS_PER_STEP
    lane = lax.broadcasted_iota(I32, (L, 128), 1)
    ri = lax.broadcasted_iota(I32, (L, L), 0)
    ci = lax.broadcasted_iota(I32, (L, L), 1)
    eye = ri == ci
    tril = ri >= ci

    @pl.when(t == 0)
    def _():
        c_sc[...] = jnp.zeros_like(c_sc)
        n_sc[...] = jnp.zeros_like(n_sc)
        m_sc[...] = jnp.zeros_like(m_sc)

    def head_chunk(hh, r0):
        h = hb * HB + hh
        cs = slice(hh * HD, (hh + 1) * HD)
        bi = bif_ref[0, h]
        bf = bif_ref[1, h]
        ng = ng_ref[:, cs]
        C = c_sc[hh]
        n = n_sc[hh]
        m = m_sc[hh][:, 0:1]
        q = q_ref[0, pl.ds(r0, L), cs]
        k = k_ref[0, pl.ds(r0, L), cs] * (HD ** -0.5)
        v = v_ref[0, pl.ds(r0, L), cs]
        smb = sm_ref[0, pl.ds(r0, L), :]
        ig_col = jnp.sum(jnp.where(lane == SM_I + h, smb, 0.0), axis=1, keepdims=True) + bi
        fp_col = jnp.sum(jnp.where(lane == SM_F + h, smb, 0.0), axis=1, keepdims=True) + bf
        lf_col = -_softplus(-fp_col)
        lf_row = jnp.sum(jnp.where(eye, lf_col, 0.0), axis=0, keepdims=True)
        ig_row = jnp.sum(jnp.where(eye, ig_col, 0.0), axis=0, keepdims=True)
        b_col = jnp.sum(jnp.where(tril, lf_row, 0.0), axis=1, keepdims=True)
        b_row = jnp.sum(jnp.where(ri <= ci, lf_col, 0.0), axis=0, keepdims=True)
        d = jnp.where(tril, b_col - b_row + ig_row, NEG_INF)
        inter = b_col + m
        m_t = jnp.maximum(inter, jnp.max(d, axis=1, keepdims=True))
        w_intra = jnp.exp(d - m_t)
        w_inter = jnp.exp(inter - m_t)
        qb = q.astype(BF16)
        vb = v.astype(BF16)
        s = _dot_nt(qb, k.astype(BF16)) * w_intra
        num = _dot(s.astype(BF16), vb) + w_inter * _dot(qb, C.astype(BF16))
        den = jnp.sum(s, axis=1, keepdims=True) + w_inter * jnp.sum(q * n, axis=1, keepdims=True)
        hh_ = num * (1.0 / jnp.maximum(jnp.abs(den), jnp.exp(-m_t)))
        m_new = m_t[L - 1:L, :]
        b_last = b_col[L - 1:L, :]
        decay = jnp.exp(b_last + m - m_new)
        w_s = jnp.exp(b_last - b_col + ig_col - m_new)
        kw = k * w_s
        c_sc[hh] = decay * C + _dot(kw.T.astype(BF16), vb)
        n_sc[hh] = decay * n + jnp.sum(kw, axis=0, keepdims=True)
        m_sc[hh] = jnp.broadcast_to(m_new, (1, 128))
        mu = jnp.mean(hh_, axis=1, keepdims=True)
        hc = hh_ - mu
        var = jnp.mean(hc * hc, axis=1, keepdims=True)
        hn = hc * lax.rsqrt(var + LN_EPS) * ng
        o_ref[0, pl.ds(r0, L), cs] = (_sigmoid(mo_ref[0, pl.ds(r0, L), cs]) * hn).astype(o_ref.dtype)

    def sub(c, carry):
        r0 = pl.multiple_of(c * L, L)
        for hh in range(HB):
            head_chunk(hh, r0)
        return carry

    lax.fori_loop(0, TB // L, sub, 0)

    @pl.when(t == NT - 1)
    def _():
        c_ref[0] = c_sc[...]
        n_ref[0] = n_sc[...]
        m_ref[0] = m_sc[...]


def mlstm_prompt(z3, b_if, norm_g):
    B, T, _ = z3.shape
    H = M_HEADS
    HB = M_HEADS_PER_STEP
    TB = min(M_ROWS_PER_STEP, T)
    W = HB * HD
    assert OFF_MQKV % W == 0 and OFF_MO % W == 0 and T % TB == 0

    def hspec(off):
        return pl.BlockSpec((1, TB, W), lambda b, h, t: (b, t, off // W + h))

    return pl.pallas_call(
        functools.partial(_mlstm_body, NT=T // TB, TB=TB),
        grid=(B, H // HB, T // TB),
        in_specs=[pl.BlockSpec(memory_space=pltpu.SMEM),
                  hspec(OFF_MQKV), hspec(OFF_MQKV + M_WIDTH), hspec(OFF_MQKV + 2 * M_WIDTH),
                  pl.BlockSpec((1, TB, 128), lambda b, h, t: (b, t, OFF_SM // 128)),
                  hspec(OFF_MO),
                  pl.BlockSpec((1, W), lambda b, h, t: (0, h))],
        out_specs=[pl.BlockSpec((1, TB, W), lambda b, h, t: (b, t, h)),
                   pl.BlockSpec((1, HB, HD, HD), lambda b, h, t: (b, h, 0, 0)),
                   pl.BlockSpec((1, HB, 1, HD), lambda b, h, t: (b, h, 0, 0)),
                   pl.BlockSpec((1, HB, 1, 128), lambda b, h, t: (b, h, 0, 0))],
        out_shape=[jax.ShapeDtypeStruct((B, T, M_WIDTH), BF16),
                   jax.ShapeDtypeStruct((B, H, HD, HD), F32),
                   jax.ShapeDtypeStruct((B, H, 1, HD), F32),
                   jax.ShapeDtypeStruct((B, H, 1, 128), F32)],
        scratch_shapes=[pltpu.VMEM((HB, HD, HD), F32), pltpu.VMEM((HB, 1, HD), F32),
                        pltpu.VMEM((HB, 1, 128), F32)],
        compiler_params=_cp("arbitrary", "arbitrary", "arbitrary"),
        name="mlstm_prompt",
    )(b_if, z3, z3, z3, z3, z3, norm_g.reshape(1, M_WIDTH))


R_CHUNK = 128


def _rglru_gates(xc, was, wxs, vec):
    xb = xc.astype(BF16)
    nb = len(was)
    ya = jnp.concatenate([_dot(xb[:, j * 128:(j + 1) * 128], was[j]) for j in range(nb)], axis=1)
    yx = jnp.concatenate([_dot(xb[:, j * 128:(j + 1) * 128], wxs[j]) for j in range(nb)], axis=1)
    r = _sigmoid(ya + vec[1:2])
    i = _sigmoid(yx + vec[2:3])
    log_a = -LRU_C * r * _softplus(-vec[3:4])
    a = jnp.exp(log_a)
    at = -jnp.tanh(log_a)
    u = jnp.sqrt(2.0 * at) * lax.rsqrt(1.0 + at) * (i * xc)
    return a, u


R_BLOCKS_PER_STEP = 2


def _rglru_body(rx_ref, rgt_ref, cw_ref, vec_ref, wa_ref, wx_ref, o_ref, hl_ref, xpad_ref, *, T):
    Tc = R_CHUNK
    W = rx_ref.shape[2]
    xpad_ref[0:8, :] = jnp.zeros((8, W), F32)
    xpad_ref[8:T + 8, :] = rx_ref[0]
    cw = cw_ref[...]
    vec = vec_ref[...]
    was = [wa_ref[j].astype(BF16) for j in range(W // 128)]
    wxs = [wx_ref[j].astype(BF16) for j in range(W // 128)]
    rowmod = lax.broadcasted_iota(I32, (Tc, W), 0) & 7

    def chunk(c, h):
        r0 = pl.multiple_of(c * Tc, Tc)
        win = xpad_ref[pl.ds(r0, Tc + 8), :]
        xc = vec[0:1]
        for j in range(CONV_W):
            off = 8 - (CONV_W - 1) + j
            xc = xc + win[off:off + Tc] * cw[j:j + 1]
        a, u = _rglru_gates(xc, was, wxs, vec)
        for sft in (1, 2, 4):
            a1 = pltpu.roll(a, sft, axis=0)
            u1 = pltpu.roll(u, sft, axis=0)
            ok = rowmod >= sft
            u = jnp.where(ok, a * u1 + u, u)
            a = jnp.where(ok, a * a1, a)
        hs = []
        for gi in range(Tc // 8):
            hg = a[gi * 8:(gi + 1) * 8] * h + u[gi * 8:(gi + 1) * 8]
            hs.append(hg)
            h = hg[7:8]
        hf = jnp.concatenate(hs, axis=0)
        o_ref[0, pl.ds(r0, Tc), :] = (hf * _gelu(rgt_ref[0, pl.ds(r0, Tc), :])).astype(o_ref.dtype)
        return h

    h = lax.fori_loop(0, T // Tc, chunk, jnp.zeros((1, W), F32))
    hl_ref[0] = h


def rglru_prompt(z3, conv_w, vec, w_a, w_x):
    B, T, _ = z3.shape
    RB = R_BLOCKS_PER_STEP
    W = RB * 128
    assert OFF_RX % W == 0 and OFF_RGT % W == 0 and R_BLOCKS % RB == 0
    return pl.pallas_call(
        functools.partial(_rglru_body, T=T),
        grid=(B, R_BLOCKS // RB),
        in_specs=[pl.BlockSpec((1, T, W), lambda b, n: (b, 0, OFF_RX // W + n)),
                  pl.BlockSpec((1, T, W), lambda b, n: (b, 0, OFF_RGT // W + n)),
                  pl.BlockSpec((CONV_W, W), lambda b, n: (0, n)),
                  pl.BlockSpec((4, W), lambda b, n: (0, n)),
                  pl.BlockSpec((RB, 128, 128), lambda b, n: (n, 0, 0)),
                  pl.BlockSpec((RB, 128, 128), lambda b, n: (n, 0, 0))],
        out_specs=[pl.BlockSpec((1, T, W), lambda b, n: (b, 0, n)),
                   pl.BlockSpec((1, 1, W), lambda b, n: (b, 0, n))],
        out_shape=[jax.ShapeDtypeStruct((B, T, R_WIDTH), BF16),
                   jax.ShapeDtypeStruct((B, 1, R_WIDTH), F32)],
        scratch_shapes=[pltpu.VMEM((T + 8, W), F32)],
        compiler_params=_cp("arbitrary", "arbitrary"),
        name="rglru_prompt",
    )(z3, z3, conv_w, vec, w_a, w_x)


ROWS_PER_PAGE = PAGE_SIZE * 4 * NSA_KV
CHUNKS_PER_PAGE = PAGE_SIZE // CMP_STRIDE
PAGES_PER_STEP = 16


def _sample_cmp_body(pt_ref, slopes_ref, *refs, NP, GRP):
    page_refs = refs[:GRP]
    w1c_ref, cb_ref, w2_ref, q_ref, oc_ref, idx_ref, xs_ref, lt_ref = refs[GRP:]
    p = pl.program_id(1)
    P = NP * PAGE_SIZE
    CPG = GRP * CHUNKS_PER_PAGE
    NCH = NP * CHUNKS_PER_PAGE
    n_sel = P // SEL_BLOCK + 1
    NSP = -(-n_sel // 128) * 128
    scale = HD ** -0.5

    for k in range(GRP):
        for kg in range(2 * NSA_KV):
            for c in range(CMP_STRIDE):
                v = page_refs[k][0, pl.ds(c * 8 + kg, CHUNKS_PER_PAGE, stride=CMP_STRIDE * 8), :]
                xs_ref[kg, k * CHUNKS_PER_PAGE:(k + 1) * CHUNKS_PER_PAGE, c * HD:(c + 1) * HD] = v

    r0 = pl.multiple_of(p * CPG, CPG)
    for kind in range(2):
        x = jnp.concatenate([xs_ref[kind * NSA_KV + g] for g in range(NSA_KV)], axis=0).astype(BF16)
        lt = _dot(x, w1c_ref[kind])
        for g in range(NSA_KV):
            lt_ref[kind * NSA_KV + g, pl.ds(r0, CPG), :] = lt[g * CPG:(g + 1) * CPG]

    @pl.when(p == NP // GRP - 1)
    def _():
        q128 = jnp.concatenate([q_ref[0], jnp.zeros((128 - NSA_HEADS, HD), F32)], axis=0).astype(BF16)
        lane = lax.broadcasted_iota(I32, (1, 128), 1)
        slope_row = jnp.zeros((1, 128), F32)
        for h in range(NSA_HEADS):
            slope_row = jnp.where(lane == h, slopes_ref[h], slope_row)
        n_col = lax.broadcasted_iota(I32, (NCH, 1), 0)
        dist = P - (n_col * CMP_STRIDE + (CMP_BLOCK - 1))
        maskc = (dist >= 0) & (n_col < NCH - 1)
        dist_f = dist.astype(F32)

        def cmp_rows(kind, g):
            lead = lt_ref[kind * NSA_KV + g, :, 0:HD]
            trail = lt_ref[kind * NSA_KV + g, :, HD:2 * HD]
            hid = _gelu(lead + pltpu.roll(trail, NCH - 1, axis=0) + cb_ref[kind, 0:1])
            return _dot(hid.astype(BF16), w2_ref[kind].astype(BF16)).astype(BF16)

        oc = jnp.zeros((128, HD), F32)
        psum2 = jnp.zeros((NCH, 128), F32)
        for g in range(NSA_KV):
            kc = cmp_rows(0, g)
            vc = cmp_rows(1, g)
            s = _dot_nt(kc, q128) * scale - slope_row * dist_f
            s = jnp.where(maskc, s, NEG_INF)
            m = jnp.max(s, axis=0, keepdims=True)
            pt = jnp.where(maskc, jnp.exp(s - m), 0.0)
            l = jnp.sum(pt, axis=0, keepdims=True)
            pt = pt / jnp.where(l > 0.0, l, 1.0)
            in_g = (lane >= g * NSA_GROUP) & (lane < (g + 1) * NSA_GROUP)
            pg = jnp.where(in_g, pt, 0.0)
            oc = oc + _dot(pg.T.astype(BF16), vc)
            psum2 = psum2 + jnp.where(lane == g, jnp.sum(pg, axis=1, keepdims=True), 0.0)
        oc_ref[0] = oc[0:NSA_HEADS]

        p_hi = psum2.astype(BF16)
        p_lo = (psum2 - p_hi.astype(F32)).astype(BF16)
        jo = lax.broadcasted_iota(I32, (NSP, NCH), 0) * SEL_BLOCK
        no = lax.broadcasted_iota(I32, (NSP, NCH), 1)
        ov = ((no * CMP_STRIDE < jo + SEL_BLOCK) & (no * CMP_STRIDE + CMP_BLOCK > jo)
              & (no < NCH - 1)).astype(BF16)
        imp = _dot(ov, p_hi) + _dot(ov, p_lo)
        jcol = lax.broadcasted_iota(I32, (NSP, 1), 0)
        cur = P // SEL_BLOCK
        forced = (jcol == 0) | (jcol == cur) | (jcol == cur - 1)
        imp = jnp.where(forced, imp + FORCE_BONUS, imp)
        imp = jnp.where(jcol * SEL_BLOCK <= P, imp, -1.0)
        imp = jnp.where(jcol < n_sel, imp, -2.0)
        ri = lax.broadcasted_iota(I32, (NSP, NSP), 0)
        ci = lax.broadcasted_iota(I32, (NSP, NSP), 1)
        lane_n = lax.broadcasted_iota(I32, (NSP, 128), 1)
        jcol_f = jcol.astype(F32)
        idx_ref[0] = jnp.zeros((8, 128), I32)
        for g in range(NSA_KV):
            col = jnp.sum(jnp.where(lane_n == g, imp, 0.0), axis=1, keepdims=True)
            rowv = jnp.sum(jnp.where(ri == ci, col, 0.0), axis=0, keepdims=True)
            beats = (rowv > col) | ((rowv == col) & (ci < ri))
            rank = jnp.sum(beats.astype(F32), axis=1, keepdims=True)
            hit = rank == lane_n.astype(F32)
            idx_ref[0, g:g + 1, :] = jnp.sum(jnp.where(hit, jcol_f, 0.0), axis=0,
                                             keepdims=True).astype(I32)


def sample_cmp_select(page_table, pool, layer, n_pool, w1c, cb, w2, q8, slopes):
    DB, NP = page_table.shape
    GRP = min(PAGES_PER_STEP, NP)
    assert NP % GRP == 0
    NCH = NP * CHUNKS_PER_PAGE
    K2 = CMP_STRIDE * HD
    base = layer * n_pool

    def page_spec(k):
        return pl.BlockSpec((1, ROWS_PER_PAGE, HD),
                            lambda b, p, pt: (base + pt[b * NP + p * GRP + k], 0, 0))

    grid_spec = pltpu.PrefetchScalarGridSpec(
        num_scalar_prefetch=1,
        grid=(DB, NP // GRP),
        in_specs=[pl.BlockSpec(memory_space=pltpu.SMEM)] + [page_spec(k) for k in range(GRP)] + [
                  pl.BlockSpec((2, K2, 2 * HD), lambda b, p, pt: (0, 0, 0)),
                  pl.BlockSpec((2, 8, HD), lambda b, p, pt: (0, 0, 0)),
                  pl.BlockSpec((2, HD, HD), lambda b, p, pt: (0, 0, 0)),
                  pl.BlockSpec((1, NSA_HEADS, HD), lambda b, p, pt: (b, 0, 0))],
        out_specs=[pl.BlockSpec((1, NSA_HEADS, HD), lambda b, p, pt: (b, 0, 0)),
                   pl.BlockSpec((1, 8, 128), lambda b, p, pt: (b, 0, 0))],
        scratch_shapes=[pltpu.VMEM((2 * NSA_KV, GRP * CHUNKS_PER_PAGE, K2), F32),
                        pltpu.VMEM((2 * NSA_KV, NCH, 2 * HD), F32)],
    )
    return pl.pallas_call(
        functools.partial(_sample_cmp_body, NP=NP, GRP=GRP),
        grid_spec=grid_spec,
        out_shape=[jax.ShapeDtypeStruct((DB, NSA_HEADS, HD), F32),
                   jax.ShapeDtypeStruct((DB, 8, 128), I32)],
        compiler_params=_cp("arbitrary", "arbitrary"),
        name="sample_cmp_select",
    )(page_table.reshape(-1), slopes, *([pool] * GRP), w1c, cb, w2, q8)


def _sample_sel_body(idx_ref, pt_ref, slopes_ref, *refs, NP, WB):
    page_refs = refs[:SEL_TOPK]
    win_ref, q_ref, new_ref, oc_ref, sm_ref, o_ref = refs[SEL_TOPK:]
    b = pl.program_id(0)
    g = pl.program_id(1)
    P = NP * PAGE_SIZE
    n_past = P // SEL_BLOCK
    scale = HD ** -0.5
    q8 = q_ref[0].astype(BF16)
    rowi = lax.broadcasted_iota(I32, (NSA_HEADS, 1), 0)
    slope = jnp.zeros((NSA_HEADS, 1), F32)
    for h in range(NSA_HEADS):
        slope = jnp.where(rowi == h, slopes_ref[h], slope)
    ibase = (b * NSA_KV + g) * SEL_TOPK

    def vec_dot(krow):
        return jnp.sum(q8.astype(F32) * krow.astype(BF16).astype(F32), axis=1, keepdims=True)

    def add_token(carry, s, valid, vrow):
        m, l, acc = carry
        s = jnp.where(valid, s, NEG_INF)
        m_new = jnp.maximum(m, s)
        alpha = jnp.exp(m - m_new)
        p = jnp.where(valid, jnp.exp(s - m_new), 0.0)
        return (m_new, alpha * l + p,
                alpha * acc + p.astype(BF16).astype(F32) * vrow.astype(BF16).astype(F32))

    carry = (jnp.full((NSA_HEADS, 1), NEG_INF, F32), jnp.zeros((NSA_HEADS, 1), F32),
             jnp.zeros((NSA_HEADS, HD), F32))
    n_new = jnp.int32(0)
    lane = lax.broadcasted_iota(I32, (1, PAGE_SIZE), 1)
    for t in range(SEL_TOPK):
        j = idx_ref[ibase + t]
        n_new = n_new + (j == n_past).astype(I32)
        kb = page_refs[t][0, pl.ds(2 * NSA_KV + g, PAGE_SIZE, stride=8), :].astype(BF16)
        vb = page_refs[t][0, pl.ds(3 * NSA_KV + g, PAGE_SIZE, stride=8), :].astype(BF16)
        kpos = (j // 2) * PAGE_SIZE + lane
        s = _dot_nt(q8, kb) * scale - slope * (P - kpos).astype(F32)
        jv = jnp.full((1, PAGE_SIZE), j, I32)
        carry = _online_update(carry, s, ((kpos // SEL_BLOCK) == jv) & (jv < n_past), vb)

    has_new = jnp.full((NSA_HEADS, 1), n_new, I32) > 0
    k_new = new_ref[0, pl.ds(2 * NSA_KV + g, 1), :]
    v_new = new_ref[0, pl.ds(3 * NSA_KV + g, 1), :]
    _, l_s, acc_s = add_token(carry, vec_dot(k_new) * scale, has_new, v_new)
    o_s = acc_s / l_s

    kw = win_ref[0, pl.ds(g, WB, stride=2 * NSA_KV), :].astype(BF16)
    vw = win_ref[0, pl.ds(NSA_KV + g, WB, stride=2 * NSA_KV), :].astype(BF16)
    d = WB - lax.broadcasted_iota(I32, (1, WB), 1)
    s = _dot_nt(q8, kw) * scale - slope * d.astype(F32)
    mask = d < WINDOW
    s = jnp.where(mask, s, NEG_INF)
    m = jnp.max(s, axis=1, keepdims=True)
    p = jnp.where(mask, jnp.exp(s - m), 0.0)
    carry = (m, jnp.sum(p, axis=1, keepdims=True), _dot(p.astype(BF16), vw))
    kw_new = new_ref[0, pl.ds(4 * NSA_KV + g, 1), :]
    vw_new = new_ref[0, pl.ds(5 * NSA_KV + g, 1), :]
    _, l_w, acc_w = add_token(carry, vec_dot(kw_new) * scale, rowi >= 0, vw_new)
    o_w = acc_w / l_w

    smb = jnp.broadcast_to(sm_ref[0], (NSA_HEADS, 128))
    lane_h = lax.broadcasted_iota(I32, (NSA_HEADS, 128), 1)

    def gate(c):
        return _sigmoid(jnp.sum(jnp.where(lane_h == rowi * 3 + c, smb, 0.0), axis=1, keepdims=True))

    o_ref[0, 0] = gate(0) * oc_ref[0] + gate(1) * o_s + gate(2) * o_w


def sample_sel_win(idx, page_table, pool, win, layer, n_pool, q8, newkv, oc, sm, slopes):
    DB, NP = page_table.shape
    WB = win.shape[1] // (2 * NSA_KV)
    n_past = NP * PAGE_SIZE // SEL_BLOCK
    pbase = layer * n_pool
    wbase = layer * DB
    idx_flat = idx[:, :NSA_KV, :SEL_TOPK].reshape(-1)

    def page_spec(t):
        def page_map(b, g, idx_r, pt_r):
            j = jnp.minimum(idx_r[(b * NSA_KV + g) * SEL_TOPK + t], n_past - 1)
            return (pbase + pt_r[b * NP + j // 2], 0, 0)
        return pl.BlockSpec((1, ROWS_PER_PAGE, HD), page_map)

    grid_spec = pltpu.PrefetchScalarGridSpec(
        num_scalar_prefetch=2,
        grid=(DB, NSA_KV),
        in_specs=[pl.BlockSpec(memory_space=pltpu.SMEM)] + [page_spec(t) for t in range(SEL_TOPK)] + [
                  pl.BlockSpec((1, WB * 2 * NSA_KV, HD), lambda b, g, i_, p_: (wbase + b, 0, 0)),
                  pl.BlockSpec((1, NSA_HEADS, HD), lambda b, g, i_, p_: (b, 0, 0)),
                  pl.BlockSpec((1, 6 * NSA_KV, HD), lambda b, g, i_, p_: (b, 0, 0)),
                  pl.BlockSpec((1, NSA_HEADS, HD), lambda b, g, i_, p_: (b, 0, 0)),
                  pl.BlockSpec((1, 1, 128), lambda b, g, i_, p_: (b, 0, 0))],
        out_specs=pl.BlockSpec((1, 1, NSA_HEADS, HD), lambda b, g, i_, p_: (b, g, 0, 0)),
    )
    return pl.pallas_call(
        functools.partial(_sample_sel_body, NP=NP, WB=WB),
        grid_spec=grid_spec,
        out_shape=jax.ShapeDtypeStruct((DB, NSA_KV, NSA_HEADS, HD), F32),
        compiler_params=_cp("arbitrary", "arbitrary"),
        name="sample_sel_win",
    )(idx_flat, page_table.reshape(-1), slopes, *([pool] * SEL_TOPK), win, q8, newkv, oc, sm)


def _mlstm_step_body(bif_ref, gs_ref, ms_ref, qkv_ref, mo_ref, ng_ref, c_ref, n_ref,
                     o_ref, co_ref, no_ref, mo_out_ref):
    b = pl.program_id(0)
    H = M_HEADS
    ri = lax.broadcasted_iota(I32, (HD, HD), 0)
    ci = lax.broadcasted_iota(I32, (HD, HD), 1)
    for h in range(H):
        ig = jnp.full((1, HD), gs_ref[b, h] + bif_ref[0, h], F32)
        fp = jnp.full((1, HD), gs_ref[b, H + h] + bif_ref[1, h], F32)
        m = jnp.full((1, HD), ms_ref[b, h], F32)
        lf = -_softplus(-fp)
        inter = lf + m
        m_t = jnp.maximum(inter, ig)
        w_intra = jnp.exp(ig - m_t)
        w_inter = jnp.exp(inter - m_t)
        q = qkv_ref[0, h:h + 1, :]
        k = qkv_ref[0, H + h:H + h + 1, :] * (HD ** -0.5)
        v = qkv_ref[0, 2 * H + h:2 * H + h + 1, :]
        C = c_ref[0, h]
        n = n_ref[0, h:h + 1, :]
        s = jnp.sum(q * k, axis=1, keepdims=True) * w_intra
        qC = _dot(jnp.broadcast_to(q, (8, HD)).astype(BF16), C.astype(BF16))[0:1]
        num = s * v + w_inter * qC
        den = s + w_inter * jnp.sum(q * n, axis=1, keepdims=True)
        hh = num / jnp.maximum(jnp.abs(den), jnp.exp(-m_t))
        k_col = jnp.sum(jnp.where(ri == ci, jnp.broadcast_to(k, (HD, HD)), 0.0), axis=1, keepdims=True)
        co_ref[0, h] = w_inter * C + (k_col * w_intra) * v
        no_ref[0, h:h + 1, :] = w_inter * n + w_intra * k
        mo_out_ref[0, h:h + 1, :] = m_t
        mu = jnp.mean(hh, axis=1, keepdims=True)
        hc = hh - mu
        var = jnp.mean(hc * hc, axis=1, keepdims=True)
        hn = hc * lax.rsqrt(var + LN_EPS) * ng_ref[h:h + 1, :]
        o_ref[0, h:h + 1, :] = _sigmoid(mo_ref[0, h:h + 1, :]) * hn


def mlstm_step(b_if, gates, m_state, qkv, mo, norm_g, C, n):
    DB = qkv.shape[0]
    H = M_HEADS
    smem = pl.BlockSpec(memory_space=pltpu.SMEM)
    row = pl.BlockSpec((1, H, HD), lambda b: (b, 0, 0))
    cspec = pl.BlockSpec((1, H, HD, HD), lambda b: (b, 0, 0, 0))
    return pl.pallas_call(
        _mlstm_step_body,
        grid=(DB,),
        in_specs=[smem, smem, smem, pl.BlockSpec((1, 3 * H, HD), lambda b: (b, 0, 0)), row,
                  pl.BlockSpec((H, HD), lambda b: (0, 0)), cspec, row],
        out_specs=[row, cspec, row, row],
        out_shape=[jax.ShapeDtypeStruct((DB, H, HD), F32), jax.ShapeDtypeStruct((DB, H, HD, HD), F32),
                   jax.ShapeDtypeStruct((DB, H, HD), F32), jax.ShapeDtypeStruct((DB, H, HD), F32)],
        compiler_params=_cp("arbitrary"),
        name="mlstm_step",
    )(b_if, gates, m_state, qkv, mo, norm_g.reshape(H, HD), C, n)


def _rglru_step_body(rx_ref, rgt_ref, buf_ref, h_ref, cw_ref, vec_ref, wa_ref, wx_ref, o_ref, ho_ref):
    cw = cw_ref[...]
    vec = vec_ref[...]
    xc = vec[0:1] + rx_ref[...] * cw[CONV_W - 1:CONV_W]
    for j in range(CONV_W - 1):
        xc = xc + buf_ref[j] * cw[j:j + 1]
    a, u = _rglru_gates(xc, [wa_ref[0].astype(BF16)], [wx_ref[0].astype(BF16)], vec)
    h = a * h_ref[...] + u
    ho_ref[...] = h
    o_ref[...] = h * _gelu(rgt_ref[...])


def rglru_step(rx, rgt, buf_t, h0, conv_w, vec, w_a, w_x):
    DB = rx.shape[0]
    col = pl.BlockSpec((DB, 128), lambda n: (0, n))
    return pl.pallas_call(
        _rglru_step_body,
        grid=(R_BLOCKS,),
        in_specs=[col, col, pl.BlockSpec((CONV_W - 1, DB, 128), lambda n: (0, 0, n)), col,
                  pl.BlockSpec((CONV_W, 128), lambda n: (0, n)),
                  pl.BlockSpec((4, 128), lambda n: (0, n)),
                  pl.BlockSpec((1, 128, 128), lambda n: (n, 0, 0)),
                  pl.BlockSpec((1, 128, 128), lambda n: (n, 0, 0))],
        out_specs=[col, col],
        out_shape=[jax.ShapeDtypeStruct((DB, R_WIDTH), F32), jax.ShapeDtypeStruct((DB, R_WIDTH), F32)],
        compiler_params=_cp("arbitrary"),
        name="rglru_step",
    )(rx, rgt, buf_t, h0, conv_w, vec, w_a, w_x)


def pack_w_in(w):
    pad = jnp.zeros(w.shape[:2] + (N_AL - 14888,), w.dtype)
    parts = [w[..., 8744:14888], w[..., 0:1024], w[..., 2584:5656], w[..., 5672:6696], w[..., 6696:7720],
             w[..., 7720:8744], w[..., 1024:2560], w[..., 2560:2584], w[..., 5656:5672], pad]
    return jnp.concatenate(parts, axis=-1).astype(BF16)


def prompt_layer(xp, lw, gw, layer, kv_buf, depth):
    B, T, D = xp.shape
    x2 = xp.reshape(B * T, D)
    M = B * T
    z = mm(x2, gw["w_in"], layer, tm=min(IN_TM, M), tn=IN_TN, out_dtype=F32)
    z3 = z.reshape(B, T, N_AL)
    kv_buf = kv_rows(z, kv_buf, layer, depth)
    kc, vc, cb = compress_from_z(z3, lw["phi_w1"], lw["pe"], lw["phi_b1"], lw["phi_w2"])
    o_a = nsa_prompt(z3, kc, vc, lw["slopes"])
    o_b, Cp, np_, mp = mlstm_prompt(z3, lw["b_if"], lw["norm_g"])
    o_c, hp = rglru_prompt(z3, lw["conv_w"], lw["rg_vec"], lw["w_a"], lw["w_x"])
    merged = merge(o_a.reshape(B * T, -1), o_b.reshape(B * T, -1), o_c.reshape(B * T, -1), z,
                   gw["w_branch"], layer, tm=min(MERGE_TM, M), tn=MERGE_TN, out_dtype=BF16)
    h = mm_res_ln(merged, gw["w_out"], layer, x2, lw["ln_g"][0:1], lw["ln_b"][0:1], tm=min(OUT_TM, M), tk=D)
    f1 = mm(h, gw["mlp_w1"], layer, tm=min(UP_TM, M), tn=UP_TN, out_dtype=BF16, act="relu2")
    x_new = mm_res_ln(f1, gw["mlp_w2"], layer, h, lw["ln_g"][1:2], lw["ln_b"][1:2], tm=min(DOWN_TM, M),
                      tk=DOWN_TK)
    n_win = min(WINDOW, T)
    wkv = z3[:, T - n_win:, OFF_KV + 4 * NSA_KV * HD:OFF_KV + 6 * NSA_KV * HD]
    states = (wkv.reshape(B, n_win, 2, NSA_KV, HD), Cp, np_[:, :, 0], mp[:, :, 0, 0], hp[:, 0],
              z3[:, T - (CONV_W - 1):, OFF_RX:OFF_RX + R_WIDTH])
    return x_new.reshape(B, T, D), states, cb, kv_buf


def sample_layer(xs, lw, gw, cb, layer, page_table, pool, n_pool, win, win_l, C0, n0, m0, h0, conv0):
    DB, D = xs.shape
    z = mm(xs, gw["w_in"], layer, tm=DB, tn=IN_TN, out_dtype=F32)
    q8 = z[:, OFF_Q:OFF_Q + NSA_WIDTH].reshape(DB, NSA_HEADS, HD)
    newkv = z[:, OFF_KV:OFF_KV + 6 * NSA_KV * HD].reshape(DB, 6 * NSA_KV, HD)
    sm = z[:, OFF_SM:OFF_SM + 128].reshape(DB, 1, 128)
    oc, idx = sample_cmp_select(page_table, pool, layer, n_pool, lw["w1c"], cb, lw["phi_w2"], q8,
                                lw["slopes"])
    osw = sample_sel_win(idx, page_table, pool, win, layer, n_pool, q8, newkv, oc, sm, lw["slopes"])
    o_a = jnp.concatenate([osw[:, g, g * NSA_GROUP:(g + 1) * NSA_GROUP] for g in range(NSA_KV)],
                          axis=1).reshape(DB, NSA_WIDTH)
    qkv = z[:, OFF_MQKV:OFF_MQKV + 3 * M_WIDTH].reshape(DB, 3 * M_HEADS, HD)
    mo = z[:, OFF_MO:OFF_MO + M_WIDTH].reshape(DB, M_HEADS, HD)
    gates = z[:, OFF_SM + SM_I:OFF_SM + SM_I + 2 * M_HEADS]
    o_b, Cs, ns, ms = mlstm_step(lw["b_if"], gates, m0, qkv, mo, lw["norm_g"], C0, n0)
    rx = z[:, OFF_RX:OFF_RX + R_WIDTH]
    o_c, hs = rglru_step(rx, z[:, OFF_RGT:OFF_RGT + R_WIDTH], conv0.transpose(1, 0, 2), h0,
                         lw["conv_w"], lw["rg_vec"], lw["w_a"], lw["w_x"])
    merged = merge(o_a, o_b.reshape(DB, M_WIDTH), o_c, z, gw["w_branch"], layer, tm=DB, tn=MERGE_TN,
                   out_dtype=F32)
    h = mm_res_ln(merged, gw["w_out"], layer, xs, lw["ln_g"][0:1], lw["ln_b"][0:1], tm=DB, tk=D)
    f1 = mm(h, gw["mlp_w1"], layer, tm=DB, tn=UP_TN, out_dtype=F32, act="relu2")
    x_new = mm_res_ln(f1, gw["mlp_w2"], layer, h, lw["ln_g"][1:2], lw["ln_b"][1:2], tm=DB, tk=DOWN_TK)
    kvn = newkv.reshape(DB, 1, 6, NSA_KV, HD)
    Wb = win_l.shape[1]
    keep = min(WINDOW, Wb + 1)
    win_new = jnp.concatenate([win_l, kvn[:, :, 4:6]], axis=1)[:, Wb + 1 - keep:]
    conv_new = jnp.concatenate([conv0[:, 1:], rx[:, None]], axis=1)
    states = (kvn[:, :, :4], win_new, Cs, ns, ms[:, :, 0], hs, conv_new)
    return x_new, states


def kernel(x_prompt, x_sample, cache_nsa_kv, cache_win_kv, state_mlstm_C, state_mlstm_n, state_mlstm_m,
           state_rglru_h, state_rglru_conv, page_table, w_in, nsa_pe, nsa_phi_w1, nsa_phi_b1, nsa_phi_w2,
           mlstm_b_if, mlstm_norm_g, rg_conv_w, rg_conv_b, rg_w_a, rg_b_a, rg_w_x, rg_b_x, rg_lambda,
           w_branch, w_out, ln_g, ln_b, mlp_w1, mlp_w2):
    DB, Tn, D = x_sample.shape
    assert Tn == 1 and D == D_MODEL
    depth, n_pool = cache_nsa_kv.shape[:2]
    pool = cache_nsa_kv.reshape(depth * n_pool, ROWS_PER_PAGE, HD)
    Wb = cache_win_kv.shape[2]
    win = cache_win_kv.reshape(depth * DB, Wb * 2 * NSA_KV, HD)
    slopes = alibi_slopes()
    half = CMP_STRIDE * HD
    B, T = x_prompt.shape[:2]
    xp = x_prompt
    xs = x_sample.reshape(DB, D)
    gw = {"w_in": pack_w_in(w_in), "w_branch": w_branch.astype(BF16), "w_out": w_out.astype(BF16),
          "mlp_w1": mlp_w1.astype(BF16), "mlp_w2": mlp_w2.astype(BF16)}
    new_p = [[] for _ in range(6)]
    new_s = [[] for _ in range(7)]
    kv_buf = None
    for l in range(depth):
        lw = {
            "pe": nsa_pe[l], "phi_w1": nsa_phi_w1[l], "phi_b1": nsa_phi_b1[l], "phi_w2": nsa_phi_w2[l],
            "w1c": jnp.concatenate([nsa_phi_w1[l][:, :half], nsa_phi_w1[l][:, half:]], axis=2).astype(BF16),
            "slopes": slopes, "b_if": mlstm_b_if[l], "norm_g": mlstm_norm_g[l],
            "conv_w": rg_conv_w[l],
            "rg_vec": jnp.stack([rg_conv_b[l], rg_b_a[l], rg_b_x[l], rg_lambda[l]]),
            "w_a": rg_w_a[l], "w_x": rg_w_x[l],
            "ln_g": ln_g[l], "ln_b": ln_b[l],
        }
        xp, st_p, cb, kv_buf = prompt_layer(xp, lw, gw, l, kv_buf, depth)
        xs, st_s = sample_layer(xs, lw, gw, cb, l, page_table, pool, n_pool, win, cache_win_kv[l],
                                state_mlstm_C[l], state_mlstm_n[l], state_mlstm_m[l],
                                state_rglru_h[l], state_rglru_conv[l])
        for lst, val in zip(new_p, st_p):
            lst.append(val)
        for lst, val in zip(new_s, st_s):
            lst.append(val)
    P = [jnp.stack(a) for a in new_p]
    S = [jnp.stack(a) for a in new_s]
    nsa_kv_p = kv_buf.reshape(depth, B, T, 4, NSA_KV, HD)
    return (xp, xs.reshape(DB, Tn, D), nsa_kv_p, S[0], P[0], S[1], P[1], S[2], P[2], S[3], P[3], S[4],
            P[4], S[5], P[5], S[6])
```

```python
import functools

import jax
import jax.numpy as jnp
from jax import lax
from jax.experimental import pallas as pl
from jax.experimental.pallas import tpu as pltpu

F32 = jnp.float32
BF16 = jnp.bfloat16
I32 = jnp.int32

D_MODEL = 2048
DEPTH = 4
PAGE_SIZE = 128
HD = 128
NSA_HEADS = 8
NSA_KV = 2
NSA_GROUP = 4
NSA_WIDTH = NSA_HEADS * HD
CMP_BLOCK = 32
CMP_STRIDE = 16
SEL_BLOCK = 64
SEL_TOPK = 16
WINDOW = 512
Q_BLOCK = 128
FORCE_BONUS = 1.0e4
M_HEADS = 8
M_WIDTH = M_HEADS * HD
R_WIDTH = 1024
R_BLOCKS = 8
CONV_W = 4
LRU_C = 8.0
D_FF = 4 * D_MODEL
DEEPNORM_ALPHA = (2 * DEPTH) ** 0.25
LN_EPS = 1e-5
NEG_INF = -1e30

OFF_MG = 0
OFF_Q = 6144
OFF_MQKV = 7168
OFF_MO = 10240
OFF_RX = 11264
OFF_RGT = 12288
OFF_KV = 13312
OFF_SM = 14848
N_AL = 15360
SM_I = 24
SM_F = 32

IN_TM, IN_TN = 1024, 1536
MERGE_TM, MERGE_TN = 512, 2048
OUT_TM = 512
UP_TM, UP_TN = 1024, 1024
DOWN_TM, DOWN_TK = 512, 2048

VMEM_LIMIT = 52 * 1024 * 1024


def _cp(*sem):
    return pltpu.CompilerParams(dimension_semantics=sem, vmem_limit_bytes=VMEM_LIMIT)


def _dot(a, b):
    return jnp.dot(a, b, preferred_element_type=F32)


def _dot_nt(a, b):
    return lax.dot_general(a, b, (((1,), (1,)), ((), ())), preferred_element_type=F32)


def _gelu(x):
    return 0.5 * x * (1.0 + jnp.tanh(0.7978845608028654 * (x + 0.044715 * (x * x * x))))


def _sigmoid(x):
    return 0.5 * jnp.tanh(0.5 * x) + 0.5


def _softplus(x):
    return jnp.maximum(x, 0.0) + jnp.log1p(jnp.exp(-jnp.abs(x)))


def _mm_body(a_ref, b_ref, o_ref, *, act):
    acc = _dot(a_ref[...].astype(BF16), b_ref[...])
    if act == "relu2":
        acc = jnp.square(jnp.maximum(acc, 0.0))
    o_ref[...] = acc.astype(o_ref.dtype)


def mm(a, b, layer, *, tm, tn, out_dtype, act=None):
    M, K = a.shape
    N = b.shape[2]
    assert M % tm == 0 and N % tn == 0
    return pl.pallas_call(
        functools.partial(_mm_body, act=act),
        grid=(M // tm, N // tn),
        in_specs=[pl.BlockSpec((tm, K), lambda i, j: (i, 0)),
                  pl.BlockSpec((None, K, tn), lambda i, j: (layer, 0, j))],
        out_specs=pl.BlockSpec((tm, tn), lambda i, j: (i, j)),
        out_shape=jax.ShapeDtypeStruct((M, N), out_dtype),
        compiler_params=_cp("parallel", "arbitrary"),
        name="mm",
    )(a, b)


def _mm_ln_body(a_ref, b_ref, x_ref, g_ref, bb_ref, o_ref, acc_ref, *, nk):
    k = pl.program_id(1)

    if nk > 1:
        @pl.when(k == 0)
        def _():
            acc_ref[...] = _dot(a_ref[...].astype(BF16), b_ref[...])

        @pl.when((k > 0) & (k < nk - 1))
        def _():
            acc_ref[...] += _dot(a_ref[...].astype(BF16), b_ref[...])

    @pl.when(k == nk - 1)
    def _():
        y = DEEPNORM_ALPHA * x_ref[...] + _dot(a_ref[...].astype(BF16), b_ref[...])
        if nk > 1:
            y = y + acc_ref[...]
        mu = jnp.mean(y, axis=-1, keepdims=True)
        yc = y - mu
        var = jnp.mean(yc * yc, axis=-1, keepdims=True)
        o_ref[...] = yc * lax.rsqrt(var + LN_EPS) * g_ref[...] + bb_ref[...]


def mm_res_ln(a, b, layer, x, g, bb, *, tm, tk):
    M, K = a.shape
    N = b.shape[2]
    nk = K // tk
    assert M % tm == 0 and K % tk == 0
    return pl.pallas_call(
        functools.partial(_mm_ln_body, nk=nk),
        grid=(M // tm, nk),
        in_specs=[pl.BlockSpec((tm, tk), lambda i, k: (i, k)),
                  pl.BlockSpec((None, tk, N), lambda i, k: (layer, k, 0)),
                  pl.BlockSpec((tm, N), lambda i, k: (i, 0)),
                  pl.BlockSpec((1, N), lambda i, k: (0, 0)),
                  pl.BlockSpec((1, N), lambda i, k: (0, 0))],
        out_specs=pl.BlockSpec((tm, N), lambda i, k: (i, 0)),
        out_shape=jax.ShapeDtypeStruct((M, N), F32),
        scratch_shapes=[pltpu.VMEM((tm, N), F32)],
        compiler_params=_cp("parallel", "arbitrary"),
        name="mm_res_ln",
    )(a, b, x, g, bb)


def _merge_body(oa_ref, ob_ref, oc_ref, g0_ref, g1_ref, g2_ref, w_ref, o_ref):
    acc = _sigmoid(g0_ref[...]) * _dot(oa_ref[...].astype(BF16), w_ref[0])
    acc += _sigmoid(g1_ref[...]) * _dot(ob_ref[...].astype(BF16), w_ref[1])
    acc += _sigmoid(g2_ref[...]) * _dot(oc_ref[...].astype(BF16), w_ref[2])
    o_ref[...] = acc.astype(o_ref.dtype)


def merge(o_a, o_b, o_c, z, w_branch, layer, *, tm, tn, out_dtype):
    M, W = o_a.shape
    N = w_branch.shape[3]
    assert OFF_MG % tn == 0 and N % tn == 0 and M % tm == 0
    gb = OFF_MG // tn
    nb = N // tn
    br = pl.BlockSpec((tm, W), lambda i, j: (i, 0))

    def gspec(k):
        return pl.BlockSpec((tm, tn), lambda i, j: (i, gb + k * nb + j))

    return pl.pallas_call(
        _merge_body,
        grid=(M // tm, nb),
        in_specs=[br, br, br, gspec(0), gspec(1), gspec(2),
                  pl.BlockSpec((None, 3, W, tn), lambda i, j: (layer, 0, 0, j),
                               pipeline_mode=pl.Buffered(1) if nb == 1 else None)],
        out_specs=pl.BlockSpec((tm, tn), lambda i, j: (i, j)),
        out_shape=jax.ShapeDtypeStruct((M, N), out_dtype),
        compiler_params=_cp("parallel", "arbitrary"),
        name="merge",
    )(o_a, o_b, o_c, z, z, z, w_branch)


def _kv_rows_body(z_ref, *rest):
    o_ref = rest[-1]
    tm = z_ref.shape[0]
    for kg in range(4 * NSA_KV):
        o_ref[pl.ds(kg, tm, stride=4 * NSA_KV), :] = z_ref[:, kg * HD:(kg + 1) * HD]


def kv_rows(z, buf, layer, depth):
    M = z.shape[0]
    tm = min(512, M)
    nb = M // tm
    W = 4 * NSA_KV * HD
    assert OFF_KV % W == 0 and M % tm == 0
    in_specs = [pl.BlockSpec((tm, W), lambda i: (i, OFF_KV // W))]
    args = [z]
    aliases = {}
    if buf is not None:
        in_specs.append(pl.BlockSpec(memory_space=pl.ANY))
        args.append(buf)
        aliases = {1: 0}
    return pl.pallas_call(
        _kv_rows_body,
        grid=(nb,),
        in_specs=in_specs,
        out_specs=pl.BlockSpec((tm * 4 * NSA_KV, HD), lambda i: (layer * nb + i, 0)),
        out_shape=jax.ShapeDtypeStruct((depth * M * 4 * NSA_KV, HD), F32),
        input_output_aliases=aliases,
        compiler_params=_cp("arbitrary"),
        name="kv_rows",
    )(*args)


def _compress_body(z_ref, w1_ref, pe_ref, b1_ref, w2_ref, o_ref, cb_ref, *, n):
    half = CMP_STRIDE * HD
    x = jnp.concatenate([z_ref[0, pl.ds(c, n, stride=CMP_STRIDE), :] for c in range(CMP_STRIDE)],
                        axis=1).astype(BF16)
    w1 = w1_ref[0].astype(BF16)
    lead = _dot(x, w1[:half])
    trail = _dot(x, w1[half:])
    pe8 = jnp.broadcast_to(pe_ref[0], (8, 2 * half)).astype(BF16)
    cb = _dot(pe8, w1) + b1_ref[0]
    trail_next = pltpu.roll(trail, n - 1, axis=0)
    hid = _gelu(lead + trail_next + cb[0:1])
    o_ref[0, 0] = _dot(hid.astype(BF16), w2_ref[0].astype(BF16))
    cb_ref[0] = cb


def compress_from_z(z3, w1, pe, b1, w2):
    B, T, _ = z3.shape
    G = NSA_KV
    S = B * G
    n = T // CMP_STRIDE
    K = CMP_STRIDE * HD
    kvb = OFF_KV // HD
    out, cb = pl.pallas_call(
        functools.partial(_compress_body, n=n),
        grid=(2, S),
        in_specs=[pl.BlockSpec((1, T, HD), lambda k, s: (s // G, 0, kvb + k * G + s % G)),
                  pl.BlockSpec((1, 2 * K, HD), lambda k, s: (k, 0, 0)),
                  pl.BlockSpec((1, 1, 2 * K), lambda k, s: (k, 0, 0)),
                  pl.BlockSpec((1, 1, HD), lambda k, s: (k, 0, 0)),
                  pl.BlockSpec((1, HD, HD), lambda k, s: (k, 0, 0))],
        out_specs=[pl.BlockSpec((1, 1, n, HD), lambda k, s: (k, s, 0, 0)),
                   pl.BlockSpec((1, 8, HD), lambda k, s: (k, 0, 0))],
        out_shape=[jax.ShapeDtypeStruct((2, S, n, HD), F32),
                   jax.ShapeDtypeStruct((2, 8, HD), F32)],
        compiler_params=_cp("arbitrary", "arbitrary"),
        name="compress_prompt",
    )(z3, w1, pe.reshape(2, 1, CMP_BLOCK * HD), b1.reshape(2, 1, HD), w2)
    return out[0], out[1], cb


def alibi_slopes():
    return jnp.asarray([2.0 ** (-8.0 * (h + 1) / NSA_HEADS) for h in range(NSA_HEADS)], F32)


SEL_CHUNK = 512


def _online_update(carry, s, mask, v):
    m, l, acc = carry
    s = jnp.where(mask, s, NEG_INF)
    m_new = jnp.maximum(m, jnp.max(s, axis=1, keepdims=True))
    alpha = jnp.exp(m - m_new)
    p = jnp.where(mask, jnp.exp(s - m_new), 0.0)
    l = alpha * l + jnp.sum(p, axis=1, keepdims=True)
    acc = alpha * acc + _dot(p.astype(BF16), v)
    return m_new, l, acc


MASK_BIG = 1.0e30
LOG2E = 1.4426950408889634
WIN_KEYS = WINDOW + Q_BLOCK


def _nsa_prompt_body(slopes_ref, q_ref, ks_ref, vs_ref, kw_ref, vw_ref, kc_ref, vc_ref,
                     sm_ref, o_ref, ks_bf, vs_aug, et_bf, kw_pad, vw_aug, wbias, sbias, y_a, y_b,
                     clist, *, T):
    assert T % (2 * SEL_CHUNK) == 0
    g = pl.program_id(1)
    i = pl.program_id(2)
    QB = Q_BLOCK
    R = NSA_GROUP
    RQ = R * QB
    NCP = T // CMP_STRIDE
    NC = NCP - 1
    NS = T // SEL_BLOCK
    NSP = 128
    assert NS <= NSP
    t0 = i * QB
    scale = HD ** -0.5
    c2 = scale * LOG2E

    @pl.when(i == 0)
    def _():
        lane = lax.broadcasted_iota(I32, (T, HD), 1)
        ones_col = jnp.where(lane == 0, 1.0, 0.0).astype(BF16)
        ks_bf[...] = ks_ref[0].astype(BF16)
        vs_aug[:, 0:HD] = vs_ref[0].astype(BF16)
        vs_aug[:, HD:2 * HD] = ones_col
        kk = lax.broadcasted_iota(I32, (T, HD), 0)
        et_bf[...] = jnp.where(kk // SEL_BLOCK == lane, MASK_BIG, 0.0).astype(BF16)
        kw_pad[0:WINDOW, :] = jnp.zeros((WINDOW, HD), BF16)
        kw_pad[WINDOW:WINDOW + T, :] = kw_ref[0].astype(BF16)
        vw_aug[0:WINDOW, :] = jnp.zeros((WINDOW, 2 * HD), BF16)
        vw_aug[WINDOW:WINDOW + T, 0:HD] = vw_ref[0].astype(BF16)
        vw_aug[WINDOW:WINDOW + T, HD:2 * HD] = ones_col
        wd = (lax.broadcasted_iota(I32, (QB, WIN_KEYS), 0) + WINDOW
              - lax.broadcasted_iota(I32, (QB, WIN_KEYS), 1))
        band = (wd >= 0) & (wd < WINDOW)
        wdf = wd.astype(F32)
        lf = lax.broadcasted_iota(I32, (QB, SEL_CHUNK), 1).astype(F32)
        for r in range(R):
            sl = slopes_ref[g * R + r]
            wbias[r * QB:(r + 1) * QB, :] = jnp.where(band, (-LOG2E * sl) * wdf, -MASK_BIG)
            sbias[r * QB:(r + 1) * QB, :] = (LOG2E * sl) * lf

    qb = q_ref[0]
    q4 = jnp.concatenate([qb[:, r * HD:(r + 1) * HD] for r in range(R)], axis=0).astype(BF16)
    row = lax.broadcasted_iota(I32, (RQ, 1), 0)
    qpos = t0 + (row & (QB - 1))
    slope = jnp.concatenate(
        [jnp.full((QB, 1), slopes_ref[g * R + r], F32) for r in range(R)], axis=0)

    kc = kc_ref[0, 0].astype(BF16)
    vc = vc_ref[0, 0].astype(BF16)
    n_idx = lax.broadcasted_iota(I32, (1, NCP), 1)
    dist = qpos - (n_idx * CMP_STRIDE + (CMP_BLOCK - 1))
    mask = (dist >= 0) & (n_idx < NC)
    s = _dot_nt(q4, kc) * scale - slope * dist.astype(F32)
    s = jnp.where(mask, s, NEG_INF)
    m = jnp.max(s, axis=1, keepdims=True)
    p = jnp.where(mask, jnp.exp(s - m), 0.0)
    l = jnp.sum(p, axis=1, keepdims=True)
    p = p * (1.0 / jnp.where(l > 0.0, l, 1.0))
    o_c = _dot(p.astype(BF16), vc)

    psum = p[0:QB]
    for r in range(1, R):
        psum = psum + p[r * QB:(r + 1) * QB]
    p_hi = psum.astype(BF16)
    p_lo = (psum - p_hi.astype(F32)).astype(BF16)
    jo = lax.broadcasted_iota(I32, (NS, NCP), 0) * SEL_BLOCK
    no = lax.broadcasted_iota(I32, (NS, NCP), 1)
    ov = ((no * CMP_STRIDE < jo + SEL_BLOCK) & (no * CMP_STRIDE + CMP_BLOCK > jo)
          & (no < NC)).astype(BF16)
    imp = _dot_nt(ov, p_hi) + _dot_nt(ov, p_lo)
    jj = lax.broadcasted_iota(I32, (NS, QB), 0)
    qp = t0 + lax.broadcasted_iota(I32, (NS, QB), 1)
    cur = qp // SEL_BLOCK
    forced = (jj == 0) | (jj == cur) | (jj == cur - 1)
    imp = jnp.where(forced, imp + FORCE_BONUS, imp)
    imp = jnp.where(jj * SEL_BLOCK <= qp, imp, -1.0)
    ranks = [jnp.zeros((8, QB), F32) for _ in range(NS // 8)]
    for j2 in range(NS):
        rv = imp[j2:j2 + 1, :]
        for v in range(NS // 8):
            blk = imp[8 * v:8 * v + 8]
            if 8 * v > j2:
                beats = rv >= blk
            elif 8 * v + 7 < j2:
                beats = rv > blk
            else:
                beats = (rv > blk) | ((rv == blk) & (jj[8 * v:8 * v + 8] > j2))
            ranks[v] = ranks[v] + jnp.where(beats, 1.0, 0.0)
    rank = jnp.concatenate(ranks, axis=0)
    selm_t = jnp.where(rank < SEL_TOPK, 0.0, -1.0)
    if NSP > NS:
        selm_t = jnp.concatenate([selm_t, jnp.zeros((NSP - NS, QB), F32)], axis=0)
    selm = selm_t.T.astype(BF16)

    sl_col = LOG2E * slope
    rel = (lax.broadcasted_iota(I32, (QB, SEL_CHUNK), 1)
           - lax.broadcasted_iota(I32, (QB, SEL_CHUNK), 0))

    def scores(c):
        k0 = pl.multiple_of(c * SEL_CHUNK, SEL_CHUNK)
        mb = _dot_nt(selm, et_bf[pl.ds(k0, SEL_CHUNK), :])
        mb = jnp.where(rel <= t0 - k0, mb, -MASK_BIG)
        return (_dot_nt(q4, ks_bf[pl.ds(k0, SEL_CHUNK), :]) * c2 + sbias[...]
                + jnp.concatenate([mb] * R, axis=0))

    def consume(c, y, carry):
        m, acc = carry
        k0 = pl.multiple_of(c * SEL_CHUNK, SEL_CHUNK)
        off = sl_col * (k0 - t0).astype(F32)
        m_new = jnp.maximum(m, jnp.max(y, axis=1, keepdims=True) + off)
        p = jnp.exp2(y - (m_new - off))
        acc = jnp.exp2(m - m_new) * acc + _dot(p.astype(BF16), vs_aug[pl.ds(k0, SEL_CHUNK), :])
        return m_new, acc

    n_chunks = (t0 + QB + SEL_CHUNK - 1) // SEL_CHUNK
    bpc = SEL_CHUNK // SEL_BLOCK
    n_act = jnp.int32(0)
    dummy = jnp.int32(T // SEL_CHUNK - 1)
    for c in reversed(range(T // SEL_CHUNK)):
        act = (jnp.sum(selm_t[c * bpc:(c + 1) * bpc] + 1.0) > 0.0) & (c < n_chunks)
        dummy = jnp.where(act, dummy, c)
    for c in range(T // SEL_CHUNK):
        act = (jnp.sum(selm_t[c * bpc:(c + 1) * bpc] + 1.0) > 0.0) & (c < n_chunks)
        clist[n_act] = c
        n_act = n_act + act.astype(I32)
    clist[n_act] = dummy

    def pair(pi, carry):
        k = 2 * pi
        y_b[...] = scores(clist[k + 1])
        carry = consume(clist[k], y_a[...], carry)
        y_a[...] = scores(clist[jnp.minimum(k + 2, n_act)])
        return consume(clist[k + 1], y_b[...], carry)

    init = (jnp.full((RQ, 1), -MASK_BIG, F32), jnp.zeros((RQ, 2 * HD), F32))
    y_a[...] = scores(clist[0])
    _, acc_s = lax.fori_loop(0, (n_act + 1) // 2, pair, init)
    o_s = acc_s[:, 0:HD] * (1.0 / acc_s[:, HD:HD + 1])

    w0 = pl.multiple_of(t0, QB)
    vrow = jnp.where(lax.broadcasted_iota(I32, (1, WIN_KEYS), 1) >= WINDOW - t0, 0.0, -MASK_BIG)
    y = _dot_nt(q4, kw_pad[pl.ds(w0, WIN_KEYS), :]) * c2 + wbias[...] + vrow
    p = jnp.exp2(y - jnp.max(y, axis=1, keepdims=True))
    acc_w = _dot(p.astype(BF16), vw_aug[pl.ds(w0, WIN_KEYS), :])
    o_w = acc_w[:, 0:HD] * (1.0 / acc_w[:, HD:HD + 1])

    smb = sm_ref[0]
    lane = lax.broadcasted_iota(I32, (QB, 128), 1)

    def gate(c):
        cols = []
        for r in range(R):
            idx = (g * R + r) * 3 + c
            cols.append(jnp.sum(jnp.where(lane == idx, smb, 0.0), axis=1, keepdims=True))
        return _sigmoid(jnp.concatenate(cols, axis=0))

    o = gate(0) * o_c + gate(1) * o_s + gate(2) * o_w
    o_ref[0] = jnp.concatenate([o[r * QB:(r + 1) * QB] for r in range(R)], axis=1).astype(o_ref.dtype)


def nsa_prompt(z3, kc, vc, slopes):
    B, T, _ = z3.shape
    G = NSA_KV
    kvb = OFF_KV // HD

    def kvspec(kind):
        return pl.BlockSpec((1, T, HD), lambda b, g, i: (b, 0, kvb + kind * G + g))

    cspec = pl.BlockSpec((1, 1, T // CMP_STRIDE, HD), lambda b, g, i: (b * G + g, 0, 0, 0))
    GW = NSA_GROUP * HD
    assert OFF_Q % GW == 0
    qspec = pl.BlockSpec((1, Q_BLOCK, GW), lambda b, g, i: (b, i, OFF_Q // GW + g))
    return pl.pallas_call(
        functools.partial(_nsa_prompt_body, T=T),
        grid=(B, G, T // Q_BLOCK),
        in_specs=[pl.BlockSpec(memory_space=pltpu.SMEM),
                  qspec, kvspec(2), kvspec(3), kvspec(4), kvspec(5), cspec, cspec,
                  pl.BlockSpec((1, Q_BLOCK, 128), lambda b, g, i: (b, i, OFF_SM // 128))],
        out_specs=pl.BlockSpec((1, Q_BLOCK, GW), lambda b, g, i: (b, i, g)),
        out_shape=jax.ShapeDtypeStruct((B, T, NSA_WIDTH), BF16),
        scratch_shapes=[pltpu.VMEM((T, HD), BF16),
                        pltpu.VMEM((T, 2 * HD), BF16),
                        pltpu.VMEM((T, HD), BF16),
                        pltpu.VMEM((T + WINDOW, HD), BF16),
                        pltpu.VMEM((T + WINDOW, 2 * HD), BF16),
                        pltpu.VMEM((NSA_GROUP * Q_BLOCK, WIN_KEYS), F32),
                        pltpu.VMEM((NSA_GROUP * Q_BLOCK, SEL_CHUNK), F32),
                        pltpu.VMEM((NSA_GROUP * Q_BLOCK, SEL_CHUNK), F32),
                        pltpu.VMEM((NSA_GROUP * Q_BLOCK, SEL_CHUNK), F32),
                        pltpu.SMEM((T // SEL_CHUNK + 8,), I32)],
        compiler_params=_cp("arbitrary", "arbitrary", "arbitrary"),
        name="nsa_prompt",
    )(slopes, z3, z3, z3, z3, z3, kc[:, None], vc[:, None], z3)


M_CHUNK_K = 128


M_HEADS_PER_STEP = 4
M_ROWS_PER_STEP = 512


def _mlstm_body(bif_ref, q_ref, k_ref, v_ref, sm_ref, mo_ref, ng_ref,
                o_ref, c_ref, n_ref, m_ref, c_sc, n_sc, m_sc, *, NT, TB):
    hb = pl.program_id(1)
    t = pl.program_id(2)
    L = M_CHUNK_K
    HB = M_HEADS_PER_STEP
    lane = lax.broadcasted_iota(I32, (L, 128), 1)
    ri = lax.broadcasted_iota(I32, (L, L), 0)
    ci = lax.broadcasted_iota(I32, (L, L), 1)
    eye = ri == ci
    tril = ri >= ci

    @pl.when(t == 0)
    def _():
        c_sc[...] = jnp.zeros_like(c_sc)
        n_sc[...] = jnp.zeros_like(n_sc)
        m_sc[...] = jnp.zeros_like(m_sc)

    def head_chunk(hh, r0):
        h = hb * HB + hh
        cs = slice(hh * HD, (hh + 1) * HD)
        bi = bif_ref[0, h]
        bf = bif_ref[1, h]
        ng = ng_ref[:, cs]
        C = c_sc[hh]
        n = n_sc[hh]
        m = m_sc[hh][:, 0:1]
        q = q_ref[0, pl.ds(r0, L), cs]
        k = k_ref[0, pl.ds(r0, L), cs] * (HD ** -0.5)
        v = v_ref[0, pl.ds(r0, L), cs]
        smb = sm_ref[0, pl.ds(r0, L), :]
        ig_col = jnp.sum(jnp.where(lane == SM_I + h, smb, 0.0), axis=1, keepdims=True) + bi
        fp_col = jnp.sum(jnp.where(lane == SM_F + h, smb, 0.0), axis=1, keepdims=True) + bf
        lf_col = -_softplus(-fp_col)
        lf_row = jnp.sum(jnp.where(eye, lf_col, 0.0), axis=0, keepdims=True)
        ig_row = jnp.sum(jnp.where(eye, ig_col, 0.0), axis=0, keepdims=True)
        b_col = jnp.sum(jnp.where(tril, lf_row, 0.0), axis=1, keepdims=True)
        b_row = jnp.sum(jnp.where(ri <= ci, lf_col, 0.0), axis=0, keepdims=True)
        d = jnp.where(tril, b_col - b_row + ig_row, NEG_INF)
        inter = b_col + m
        m_t = jnp.maximum(inter, jnp.max(d, axis=1, keepdims=True))
        w_intra = jnp.exp(d - m_t)
        w_inter = jnp.exp(inter - m_t)
        qb = q.astype(BF16)
        vb = v.astype(BF16)
        s = _dot_nt(qb, k.astype(BF16)) * w_intra
        num = _dot(s.astype(BF16), vb) + w_inter * _dot(qb, C.astype(BF16))
        den = jnp.sum(s, axis=1, keepdims=True) + w_inter * jnp.sum(q * n, axis=1, keepdims=True)
        hh_ = num * (1.0 / jnp.maximum(jnp.abs(den), jnp.exp(-m_t)))
        m_new = m_t[L - 1:L, :]
        b_last = b_col[L - 1:L, :]
        decay = jnp.exp(b_last + m - m_new)
        w_s = jnp.exp(b_last - b_col + ig_col - m_new)
        kw = k * w_s
        c_sc[hh] = decay * C + _dot(kw.T.astype(BF16), vb)
        n_sc[hh] = decay * n + jnp.sum(kw, axis=0, keepdims=True)
        m_sc[hh] = jnp.broadcast_to(m_new, (1, 128))
        mu = jnp.mean(hh_, axis=1, keepdims=True)
        hc = hh_ - mu
        var = jnp.mean(hc * hc, axis=1, keepdims=True)
        hn = hc * lax.rsqrt(var + LN_EPS) * ng
        o_ref[0, pl.ds(r0, L), cs] = (_sigmoid(mo_ref[0, pl.ds(r0, L), cs]) * hn).astype(o_ref.dtype)

    def sub(c, carry):
        r0 = pl.multiple_of(c * L, L)
        for hh in range(HB):
            head_chunk(hh, r0)
        return carry

    lax.fori_loop(0, TB // L, sub, 0)

    @pl.when(t == NT - 1)
    def _():
        c_ref[0] = c_sc[...]
        n_ref[0] = n_sc[...]
        m_ref[0] = m_sc[...]


def mlstm_prompt(z3, b_if, norm_g):
    B, T, _ = z3.shape
    H = M_HEADS
    HB = M_HEADS_PER_STEP
    TB = min(M_ROWS_PER_STEP, T)
    W = HB * HD
    assert OFF_MQKV % W == 0 and OFF_MO % W == 0 and T % TB == 0

    def hspec(off):
        return pl.BlockSpec((1, TB, W), lambda b, h, t: (b, t, off // W + h))

    return pl.pallas_call(
        functools.partial(_mlstm_body, NT=T // TB, TB=TB),
        grid=(B, H // HB, T // TB),
        in_specs=[pl.BlockSpec(memory_space=pltpu.SMEM),
                  hspec(OFF_MQKV), hspec(OFF_MQKV + M_WIDTH), hspec(OFF_MQKV + 2 * M_WIDTH),
                  pl.BlockSpec((1, TB, 128), lambda b, h, t: (b, t, OFF_SM // 128)),
                  hspec(OFF_MO),
                  pl.BlockSpec((1, W), lambda b, h, t: (0, h))],
        out_specs=[pl.BlockSpec((1, TB, W), lambda b, h, t: (b, t, h)),
                   pl.BlockSpec((1, HB, HD, HD), lambda b, h, t: (b, h, 0, 0)),
                   pl.BlockSpec((1, HB, 1, HD), lambda b, h, t: (b, h, 0, 0)),
                   pl.BlockSpec((1, HB, 1, 128), lambda b, h, t: (b, h, 0, 0))],
        out_shape=[jax.ShapeDtypeStruct((B, T, M_WIDTH), BF16),
                   jax.ShapeDtypeStruct((B, H, HD, HD), F32),
                   jax.ShapeDtypeStruct((B, H, 1, HD), F32),
                   jax.ShapeDtypeStruct((B, H, 1, 128), F32)],
        scratch_shapes=[pltpu.VMEM((HB, HD, HD), F32), pltpu.VMEM((HB, 1, HD), F32),
                        pltpu.VMEM((HB, 1, 128), F32)],
        compiler_params=_cp("arbitrary", "arbitrary", "arbitrary"),
        name="mlstm_prompt",
    )(b_if, z3, z3, z3, z3, z3, norm_g.reshape(1, M_WIDTH))


R_CHUNK = 128


def _rglru_gates(xc, was, wxs, vec):
    xb = xc.astype(BF16)
    nb = len(was)
    ya = jnp.concatenate([_dot(xb[:, j * 128:(j + 1) * 128], was[j]) for j in range(nb)], axis=1)
    yx = jnp.concatenate([_dot(xb[:, j * 128:(j + 1) * 128], wxs[j]) for j in range(nb)], axis=1)
    r = _sigmoid(ya + vec[1:2])
    i = _sigmoid(yx + vec[2:3])
    log_a = -LRU_C * r * _softplus(-vec[3:4])
    a = jnp.exp(log_a)
    at = -jnp.tanh(log_a)
    u = jnp.sqrt(2.0 * at) * lax.rsqrt(1.0 + at) * (i * xc)
    return a, u


R_BLOCKS_PER_STEP = 2


def _rglru_body(rx_ref, rgt_ref, cw_ref, vec_ref, wa_ref, wx_ref, o_ref, hl_ref, xpad_ref, *, T):
    Tc = R_CHUNK
    W = rx_ref.shape[2]
    xpad_ref[0:8, :] = jnp.zeros((8, W), F32)
    xpad_ref[8:T + 8, :] = rx_ref[0]
    cw = cw_ref[...]
    vec = vec_ref[...]
    was = [wa_ref[j].astype(BF16) for j in range(W // 128)]
    wxs = [wx_ref[j].astype(BF16) for j in range(W // 128)]
    rowmod = lax.broadcasted_iota(I32, (Tc, W), 0) & 7

    def chunk(c, h):
        r0 = pl.multiple_of(c * Tc, Tc)
        win = xpad_ref[pl.ds(r0, Tc + 8), :]
        xc = vec[0:1]
        for j in range(CONV_W):
            off = 8 - (CONV_W - 1) + j
            xc = xc + win[off:off + Tc] * cw[j:j + 1]
        a, u = _rglru_gates(xc, was, wxs, vec)
        for sft in (1, 2, 4):
            a1 = pltpu.roll(a, sft, axis=0)
            u1 = pltpu.roll(u, sft, axis=0)
            ok = rowmod >= sft
            u = jnp.where(ok, a * u1 + u, u)
            a = jnp.where(ok, a * a1, a)
        hs = []
        for gi in range(Tc // 8):
            hg = a[gi * 8:(gi + 1) * 8] * h + u[gi * 8:(gi + 1) * 8]
            hs.append(hg)
            h = hg[7:8]
        hf = jnp.concatenate(hs, axis=0)
        o_ref[0, pl.ds(r0, Tc), :] = (hf * _gelu(rgt_ref[0, pl.ds(r0, Tc), :])).astype(o_ref.dtype)
        return h

    h = lax.fori_loop(0, T // Tc, chunk, jnp.zeros((1, W), F32))
    hl_ref[0] = h


def rglru_prompt(z3, conv_w, vec, w_a, w_x):
    B, T, _ = z3.shape
    RB = R_BLOCKS_PER_STEP
    W = RB * 128
    assert OFF_RX % W == 0 and OFF_RGT % W == 0 and R_BLOCKS % RB == 0
    return pl.pallas_call(
        functools.partial(_rglru_body, T=T),
        grid=(B, R_BLOCKS // RB),
        in_specs=[pl.BlockSpec((1, T, W), lambda b, n: (b, 0, OFF_RX // W + n)),
                  pl.BlockSpec((1, T, W), lambda b, n: (b, 0, OFF_RGT // W + n)),
                  pl.BlockSpec((CONV_W, W), lambda b, n: (0, n)),
                  pl.BlockSpec((4, W), lambda b, n: (0, n)),
                  pl.BlockSpec((RB, 128, 128), lambda b, n: (n, 0, 0)),
                  pl.BlockSpec((RB, 128, 128), lambda b, n: (n, 0, 0))],
        out_specs=[pl.BlockSpec((1, T, W), lambda b, n: (b, 0, n)),
                   pl.BlockSpec((1, 1, W), lambda b, n: (b, 0, n))],
        out_shape=[jax.ShapeDtypeStruct((B, T, R_WIDTH), BF16),
                   jax.ShapeDtypeStruct((B, 1, R_WIDTH), F32)],
        scratch_shapes=[pltpu.VMEM((T + 8, W), F32)],
        compiler_params=_cp("arbitrary", "arbitrary"),
        name="rglru_prompt",
    )(z3, z3, conv_w, vec, w_a, w_x)


ROWS_PER_PAGE = PAGE_SIZE * 4 * NSA_KV
CHUNKS_PER_PAGE = PAGE_SIZE // CMP_STRIDE
PAGES_PER_STEP = 16


def _sample_cmp_body(pt_ref, slopes_ref, *refs, NP, GRP):
    page_refs = refs[:GRP]
    w1c_ref, cb_ref, w2_ref, q_ref, oc_ref, idx_ref, xs_ref, lt_ref = refs[GRP:]
    p = pl.program_id(1)
    P = NP * PAGE_SIZE
    CPG = GRP * CHUNKS_PER_PAGE
    NCH = NP * CHUNKS_PER_PAGE
    n_sel = P // SEL_BLOCK + 1
    NSP = -(-n_sel // 128) * 128
    scale = HD ** -0.5

    for k in range(GRP):
        for kg in range(2 * NSA_KV):
            plane = page_refs[k][0, pl.ds(kg, PAGE_SIZE, stride=4 * NSA_KV), :]
            xs_ref[kg, k * CHUNKS_PER_PAGE:(k + 1) * CHUNKS_PER_PAGE, :] = plane.reshape(
                CHUNKS_PER_PAGE, CMP_STRIDE * HD)

    r0 = pl.multiple_of(p * CPG, CPG)
    for kind in range(2):
        x = jnp.concatenate([xs_ref[kind * NSA_KV + g] for g in range(NSA_KV)], axis=0).astype(BF16)
        lt = _dot(x, w1c_ref[kind])
        for g in range(NSA_KV):
            lt_ref[kind * NSA_KV + g, pl.ds(r0, CPG), :] = lt[g * CPG:(g + 1) * CPG]

    @pl.when(p == NP // GRP - 1)
    def _():
        q128 = jnp.concatenate([q_ref[0], jnp.zeros((128 - NSA_HEADS, HD), F32)], axis=0).astype(BF16)
        lane = lax.broadcasted_iota(I32, (1, 128), 1)
        slope_row = jnp.zeros((1, 128), F32)
        for h in range(NSA_HEADS):
            slope_row = jnp.where(lane == h, slopes_ref[h], slope_row)
        n_col = lax.broadcasted_iota(I32, (NCH, 1), 0)
        dist = P - (n_col * CMP_STRIDE + (CMP_BLOCK - 1))
        maskc = (dist >= 0) & (n_col < NCH - 1)
        dist_f = dist.astype(F32)

        def cmp_rows(kind, g):
            lead = lt_ref[kind * NSA_KV + g, :, 0:HD]
            trail = lt_ref[kind * NSA_KV + g, :, HD:2 * HD]
            hid = _gelu(lead + pltpu.roll(trail, NCH - 1, axis=0) + cb_ref[kind, 0:1])
            return _dot(hid.astype(BF16), w2_ref[kind].astype(BF16)).astype(BF16)

        oc = jnp.zeros((128, HD), F32)
        psum2 = jnp.zeros((NCH, 128), F32)
        for g in range(NSA_KV):
            kc = cmp_rows(0, g)
            vc = cmp_rows(1, g)
            s = _dot_nt(kc, q128) * scale - slope_row * dist_f
            s = jnp.where(maskc, s, NEG_INF)
            m = jnp.max(s, axis=0, keepdims=True)
            pt = jnp.where(maskc, jnp.exp(s - m), 0.0)
            l = jnp.sum(pt, axis=0, keepdims=True)
            pt = pt / jnp.where(l > 0.0, l, 1.0)
            in_g = (lane >= g * NSA_GROUP) & (lane < (g + 1) * NSA_GROUP)
            pg = jnp.where(in_g, pt, 0.0)
            oc = oc + _dot(pg.T.astype(BF16), vc)
            psum2 = psum2 + jnp.where(lane == g, jnp.sum(pg, axis=1, keepdims=True), 0.0)
        oc_ref[0] = oc[0:NSA_HEADS]

        p_hi = psum2.astype(BF16)
        p_lo = (psum2 - p_hi.astype(F32)).astype(BF16)
        jo = lax.broadcasted_iota(I32, (NSP, NCH), 0) * SEL_BLOCK
        no = lax.broadcasted_iota(I32, (NSP, NCH), 1)
        ov = ((no * CMP_STRIDE < jo + SEL_BLOCK) & (no * CMP_STRIDE + CMP_BLOCK > jo)
              & (no < NCH - 1)).astype(BF16)
        imp = _dot(ov, p_hi) + _dot(ov, p_lo)
        jcol = lax.broadcasted_iota(I32, (NSP, 1), 0)
        cur = P // SEL_BLOCK
        forced = (jcol == 0) | (jcol == cur) | (jcol == cur - 1)
        imp = jnp.where(forced, imp + FORCE_BONUS, imp)
        imp = jnp.where(jcol * SEL_BLOCK <= P, imp, -1.0)
        imp = jnp.where(jcol < n_sel, imp, -2.0)
        ri = lax.broadcasted_iota(I32, (NSP, NSP), 0)
        ci = lax.broadcasted_iota(I32, (NSP, NSP), 1)
        lane_n = lax.broadcasted_iota(I32, (NSP, 128), 1)
        jcol_f = jcol.astype(F32)
        idx_ref[0] = jnp.zeros((8, 128), I32)
        for g in range(NSA_KV):
            col = jnp.sum(jnp.where(lane_n == g, imp, 0.0), axis=1, keepdims=True)
            rowv = jnp.sum(jnp.where(ri == ci, col, 0.0), axis=0, keepdims=True)
            beats = (rowv > col) | ((rowv == col) & (ci < ri))
            rank = jnp.sum(beats.astype(F32), axis=1, keepdims=True)
            hit = rank == lane_n.astype(F32)
            idx_ref[0, g:g + 1, :] = jnp.sum(jnp.where(hit, jcol_f, 0.0), axis=0,
                                             keepdims=True).astype(I32)


def sample_cmp_select(page_table, pool, layer, n_pool, w1c, cb, w2, q8, slopes):
    DB, NP = page_table.shape
    GRP = min(PAGES_PER_STEP, NP)
    assert NP % GRP == 0
    NCH = NP * CHUNKS_PER_PAGE
    K2 = CMP_STRIDE * HD
    base = layer * n_pool

    def page_spec(k):
        return pl.BlockSpec((1, ROWS_PER_PAGE, HD),
                            lambda b, p, pt: (base + pt[b * NP + p * GRP + k], 0, 0))

    grid_spec = pltpu.PrefetchScalarGridSpec(
        num_scalar_prefetch=1,
        grid=(DB, NP // GRP),
        in_specs=[pl.BlockSpec(memory_space=pltpu.SMEM)] + [page_spec(k) for k in range(GRP)] + [
                  pl.BlockSpec((2, K2, 2 * HD), lambda b, p, pt: (0, 0, 0)),
                  pl.BlockSpec((2, 8, HD), lambda b, p, pt: (0, 0, 0)),
                  pl.BlockSpec((2, HD, HD), lambda b, p, pt: (0, 0, 0)),
                  pl.BlockSpec((1, NSA_HEADS, HD), lambda b, p, pt: (b, 0, 0))],
        out_specs=[pl.BlockSpec((1, NSA_HEADS, HD), lambda b, p, pt: (b, 0, 0)),
                   pl.BlockSpec((1, 8, 128), lambda b, p, pt: (b, 0, 0))],
        scratch_shapes=[pltpu.VMEM((2 * NSA_KV, GRP * CHUNKS_PER_PAGE, K2), F32),
                        pltpu.VMEM((2 * NSA_KV, NCH, 2 * HD), F32)],
    )
    return pl.pallas_call(
        functools.partial(_sample_cmp_body, NP=NP, GRP=GRP),
        grid_spec=grid_spec,
        out_shape=[jax.ShapeDtypeStruct((DB, NSA_HEADS, HD), F32),
                   jax.ShapeDtypeStruct((DB, 8, 128), I32)],
        compiler_params=_cp("arbitrary", "arbitrary"),
        name="sample_cmp_select",
    )(page_table.reshape(-1), slopes, *([pool] * GRP), w1c, cb, w2, q8)


def _sample_sel_body(idx_ref, pt_ref, slopes_ref, *refs, NP, WB):
    page_refs = refs[:SEL_TOPK]
    win_ref, q_ref, new_ref, oc_ref, sm_ref, o_ref = refs[SEL_TOPK:]
    b = pl.program_id(0)
    g = pl.program_id(1)
    P = NP * PAGE_SIZE
    n_past = P // SEL_BLOCK
    scale = HD ** -0.5
    q8 = q_ref[0].astype(BF16)
    rowi = lax.broadcasted_iota(I32, (NSA_HEADS, 1), 0)
    slope = jnp.zeros((NSA_HEADS, 1), F32)
    for h in range(NSA_HEADS):
        slope = jnp.where(rowi == h, slopes_ref[h], slope)
    ibase = (b * NSA_KV + g) * SEL_TOPK

    def vec_dot(krow):
        return jnp.sum(q8.astype(F32) * krow.astype(BF16).astype(F32), axis=1, keepdims=True)

    def add_token(carry, s, valid, vrow):
        m, l, acc = carry
        s = jnp.where(valid, s, NEG_INF)
        m_new = jnp.maximum(m, s)
        alpha = jnp.exp(m - m_new)
        p = jnp.where(valid, jnp.exp(s - m_new), 0.0)
        return (m_new, alpha * l + p,
                alpha * acc + p.astype(BF16).astype(F32) * vrow.astype(BF16).astype(F32))

    carry = (jnp.full((NSA_HEADS, 1), NEG_INF, F32), jnp.zeros((NSA_HEADS, 1), F32),
             jnp.zeros((NSA_HEADS, HD), F32))
    n_new = jnp.int32(0)
    lane = lax.broadcasted_iota(I32, (1, PAGE_SIZE), 1)
    for t in range(SEL_TOPK):
        j = idx_ref[ibase + t]
        n_new = n_new + (j == n_past).astype(I32)
        kb = page_refs[t][0, pl.ds(2 * NSA_KV + g, PAGE_SIZE, stride=8), :].astype(BF16)
        vb = page_refs[t][0, pl.ds(3 * NSA_KV + g, PAGE_SIZE, stride=8), :].astype(BF16)
        kpos = (j // 2) * PAGE_SIZE + lane
        s = _dot_nt(q8, kb) * scale - slope * (P - kpos).astype(F32)
        jv = jnp.full((1, PAGE_SIZE), j, I32)
        carry = _online_update(carry, s, ((kpos // SEL_BLOCK) == jv) & (jv < n_past), vb)

    has_new = jnp.full((NSA_HEADS, 1), n_new, I32) > 0
    k_new = new_ref[0, pl.ds(2 * NSA_KV + g, 1), :]
    v_new = new_ref[0, pl.ds(3 * NSA_KV + g, 1), :]
    _, l_s, acc_s = add_token(carry, vec_dot(k_new) * scale, has_new, v_new)
    o_s = acc_s / l_s

    kw = win_ref[0, pl.ds(g, WB, stride=2 * NSA_KV), :].astype(BF16)
    vw = win_ref[0, pl.ds(NSA_KV + g, WB, stride=2 * NSA_KV), :].astype(BF16)
    d = WB - lax.broadcasted_iota(I32, (1, WB), 1)
    s = _dot_nt(q8, kw) * scale - slope * d.astype(F32)
    mask = d < WINDOW
    s = jnp.where(mask, s, NEG_INF)
    m = jnp.max(s, axis=1, keepdims=True)
    p = jnp.where(mask, jnp.exp(s - m), 0.0)
    carry = (m, jnp.sum(p, axis=1, keepdims=True), _dot(p.astype(BF16), vw))
    kw_new = new_ref[0, pl.ds(4 * NSA_KV + g, 1), :]
    vw_new = new_ref[0, pl.ds(5 * NSA_KV + g, 1), :]
    _, l_w, acc_w = add_token(carry, vec_dot(kw_new) * scale, rowi >= 0, vw_new)
    o_w = acc_w / l_w

    smb = jnp.broadcast_to(sm_ref[0], (NSA_HEADS, 128))
    lane_h = lax.broadcasted_iota(I32, (NSA_HEADS, 128), 1)

    def gate(c):
        return _sigmoid(jnp.sum(jnp.where(lane_h == rowi * 3 + c, smb, 0.0), axis=1, keepdims=True))

    o_ref[0, 0] = gate(0) * oc_ref[0] + gate(1) * o_s + gate(2) * o_w


def sample_sel_win(idx, page_table, pool, win, layer, n_pool, q8, newkv, oc, sm, slopes):
    DB, NP = page_table.shape
    WB = win.shape[1] // (2 * NSA_KV)
    n_past = NP * PAGE_SIZE // SEL_BLOCK
    pbase = layer * n_pool
    wbase = layer * DB
    idx_flat = idx[:, :NSA_KV, :SEL_TOPK].reshape(-1)

    def page_spec(t):
        def page_map(b, g, idx_r, pt_r):
            j = jnp.minimum(idx_r[(b * NSA_KV + g) * SEL_TOPK + t], n_past - 1)
            return (pbase + pt_r[b * NP + j // 2], 0, 0)
        return pl.BlockSpec((1, ROWS_PER_PAGE, HD), page_map)

    grid_spec = pltpu.PrefetchScalarGridSpec(
        num_scalar_prefetch=2,
        grid=(DB, NSA_KV),
        in_specs=[pl.BlockSpec(memory_space=pltpu.SMEM)] + [page_spec(t) for t in range(SEL_TOPK)] + [
                  pl.BlockSpec((1, WB * 2 * NSA_KV, HD), lambda b, g, i_, p_: (wbase + b, 0, 0)),
                  pl.BlockSpec((1, NSA_HEADS, HD), lambda b, g, i_, p_: (b, 0, 0)),
                  pl.BlockSpec((1, 6 * NSA_KV, HD), lambda b, g, i_, p_: (b, 0, 0)),
                  pl.BlockSpec((1, NSA_HEADS, HD), lambda b, g, i_, p_: (b, 0, 0)),
                  pl.BlockSpec((1, 1, 128), lambda b, g, i_, p_: (b, 0, 0))],
        out_specs=pl.BlockSpec((1, 1, NSA_HEADS, HD), lambda b, g, i_, p_: (b, g, 0, 0)),
    )
    return pl.pallas_call(
        functools.partial(_sample_sel_body, NP=NP, WB=WB),
        grid_spec=grid_spec,
        out_shape=jax.ShapeDtypeStruct((DB, NSA_KV, NSA_HEADS, HD), F32),
        compiler_params=_cp("arbitrary", "arbitrary"),
        name="sample_sel_win",
    )(idx_flat, page_table.reshape(-1), slopes, *([pool] * SEL_TOPK), win, q8, newkv, oc, sm)


def _mlstm_step_body(bif_ref, gs_ref, ms_ref, qkv_ref, mo_ref, ng_ref, c_ref, n_ref,
                     o_ref, co_ref, no_ref, mo_out_ref):
    b = pl.program_id(0)
    H = M_HEADS
    ri = lax.broadcasted_iota(I32, (HD, HD), 0)
    ci = lax.broadcasted_iota(I32, (HD, HD), 1)
    for h in range(H):
        ig = jnp.full((1, HD), gs_ref[b, h] + bif_ref[0, h], F32)
        fp = jnp.full((1, HD), gs_ref[b, H + h] + bif_ref[1, h], F32)
        m = jnp.full((1, HD), ms_ref[b, h], F32)
        lf = -_softplus(-fp)
        inter = lf + m
        m_t = jnp.maximum(inter, ig)
        w_intra = jnp.exp(ig - m_t)
        w_inter = jnp.exp(inter - m_t)
        q = qkv_ref[0, h:h + 1, :]
        k = qkv_ref[0, H + h:H + h + 1, :] * (HD ** -0.5)
        v = qkv_ref[0, 2 * H + h:2 * H + h + 1, :]
        C = c_ref[0, h]
        n = n_ref[0, h:h + 1, :]
        s = jnp.sum(q * k, axis=1, keepdims=True) * w_intra
        qC = _dot(jnp.broadcast_to(q, (8, HD)).astype(BF16), C.astype(BF16))[0:1]
        num = s * v + w_inter * qC
        den = s + w_inter * jnp.sum(q * n, axis=1, keepdims=True)
        hh = num / jnp.maximum(jnp.abs(den), jnp.exp(-m_t))
        k_col = jnp.sum(jnp.where(ri == ci, jnp.broadcast_to(k, (HD, HD)), 0.0), axis=1, keepdims=True)
        co_ref[0, h] = w_inter * C + (k_col * w_intra) * v
        no_ref[0, h:h + 1, :] = w_inter * n + w_intra * k
        mo_out_ref[0, h:h + 1, :] = m_t
        mu = jnp.mean(hh, axis=1, keepdims=True)
        hc = hh - mu
        var = jnp.mean(hc * hc, axis=1, keepdims=True)
        hn = hc * lax.rsqrt(var + LN_EPS) * ng_ref[h:h + 1, :]
        o_ref[0, h:h + 1, :] = _sigmoid(mo_ref[0, h:h + 1, :]) * hn


def mlstm_step(b_if, gates, m_state, qkv, mo, norm_g, C, n):
    DB = qkv.shape[0]
    H = M_HEADS
    smem = pl.BlockSpec(memory_space=pltpu.SMEM)
    row = pl.BlockSpec((1, H, HD), lambda b: (b, 0, 0))
    cspec = pl.BlockSpec((1, H, HD, HD), lambda b: (b, 0, 0, 0))
    return pl.pallas_call(
        _mlstm_step_body,
        grid=(DB,),
        in_specs=[smem, smem, smem, pl.BlockSpec((1, 3 * H, HD), lambda b: (b, 0, 0)), row,
                  pl.BlockSpec((H, HD), lambda b: (0, 0)), cspec, row],
        out_specs=[row, cspec, row, row],
        out_shape=[jax.ShapeDtypeStruct((DB, H, HD), F32), jax.ShapeDtypeStruct((DB, H, HD, HD), F32),
                   jax.ShapeDtypeStruct((DB, H, HD), F32), jax.ShapeDtypeStruct((DB, H, HD), F32)],
        compiler_params=_cp("arbitrary"),
        name="mlstm_step",
    )(b_if, gates, m_state, qkv, mo, norm_g.reshape(H, HD), C, n)


def _rglru_step_body(rx_ref, rgt_ref, buf_ref, h_ref, cw_ref, vec_ref, wa_ref, wx_ref, o_ref, ho_ref):
    cw = cw_ref[...]
    vec = vec_ref[...]
    xc = vec[0:1] + rx_ref[...] * cw[CONV_W - 1:CONV_W]
    for j in range(CONV_W - 1):
        xc = xc + buf_ref[j] * cw[j:j + 1]
    a, u = _rglru_gates(xc, [wa_ref[0].astype(BF16)], [wx_ref[0].astype(BF16)], vec)
    h = a * h_ref[...] + u
    ho_ref[...] = h
    o_ref[...] = h * _gelu(rgt_ref[...])


def rglru_step(rx, rgt, buf_t, h0, conv_w, vec, w_a, w_x):
    DB = rx.shape[0]
    col = pl.BlockSpec((DB, 128), lambda n: (0, n))
    return pl.pallas_call(
        _rglru_step_body,
        grid=(R_BLOCKS,),
        in_specs=[col, col, pl.BlockSpec((CONV_W - 1, DB, 128), lambda n: (0, 0, n)), col,
                  pl.BlockSpec((CONV_W, 128), lambda n: (0, n)),
                  pl.BlockSpec((4, 128), lambda n: (0, n)),
                  pl.BlockSpec((1, 128, 128), lambda n: (n, 0, 0)),
                  pl.BlockSpec((1, 128, 128), lambda n: (n, 0, 0))],
        out_specs=[col, col],
        out_shape=[jax.ShapeDtypeStruct((DB, R_WIDTH), F32), jax.ShapeDtypeStruct((DB, R_WIDTH), F32)],
        compiler_params=_cp("arbitrary"),
        name="rglru_step",
    )(rx, rgt, buf_t, h0, conv_w, vec, w_a, w_x)


def pack_w_in(w):
    pieces = [(8744, 14888), (0, 1024), (2584, 5656), (5672, 6696), (6696, 7720), (7720, 8744),
              (1024, 2560), (2560, 2584), (5656, 5672)]
    out = jnp.zeros(w.shape[:2] + (N_AL,), BF16)
    dst = 0
    for a, b in pieces:
        out = lax.dynamic_update_slice_in_dim(out, w[..., a:b].astype(BF16), dst, axis=2)
        dst += b - a
    return out


def prompt_layer(xp, lw, gw, layer, kv_buf, depth):
    B, T, D = xp.shape
    x2 = xp.reshape(B * T, D)
    M = B * T
    z = mm(x2, gw["w_in"], layer, tm=min(IN_TM, M), tn=IN_TN, out_dtype=F32)
    z3 = z.reshape(B, T, N_AL)
    kv_buf = kv_rows(z, kv_buf, layer, depth)
    kc, vc, cb = compress_from_z(z3, lw["phi_w1"], lw["pe"], lw["phi_b1"], lw["phi_w2"])
    o_a = nsa_prompt(z3, kc, vc, lw["slopes"])
    o_b, Cp, np_, mp = mlstm_prompt(z3, lw["b_if"], lw["norm_g"])
    o_c, hp = rglru_prompt(z3, lw["conv_w"], lw["rg_vec"], lw["w_a"], lw["w_x"])
    merged = merge(o_a.reshape(B * T, -1), o_b.reshape(B * T, -1), o_c.reshape(B * T, -1), z,
                   gw["w_branch"], layer, tm=min(MERGE_TM, M), tn=MERGE_TN, out_dtype=BF16)
    h = mm_res_ln(merged, gw["w_out"], layer, x2, lw["ln_g"][0:1], lw["ln_b"][0:1], tm=min(OUT_TM, M), tk=D)
    f1 = mm(h, gw["mlp_w1"], layer, tm=min(UP_TM, M), tn=UP_TN, out_dtype=BF16, act="relu2")
    x_new = mm_res_ln(f1, gw["mlp_w2"], layer, h, lw["ln_g"][1:2], lw["ln_b"][1:2], tm=min(DOWN_TM, M),
                      tk=DOWN_TK)
    n_win = min(WINDOW, T)
    wkv = z3[:, T - n_win:, OFF_KV + 4 * NSA_KV * HD:OFF_KV + 6 * NSA_KV * HD]
    states = (wkv.reshape(B, n_win, 2, NSA_KV, HD), Cp, np_[:, :, 0], mp[:, :, 0, 0], hp[:, 0],
              z3[:, T - (CONV_W - 1):, OFF_RX:OFF_RX + R_WIDTH])
    return x_new.reshape(B, T, D), states, cb, kv_buf


def sample_layer(xs, lw, gw, cb, layer, page_table, pool, n_pool, win, win_l, C0, n0, m0, h0, conv0):
    DB, D = xs.shape
    z = mm(xs, gw["w_in"], layer, tm=DB, tn=IN_TN, out_dtype=F32)
    q8 = z[:, OFF_Q:OFF_Q + NSA_WIDTH].reshape(DB, NSA_HEADS, HD)
    newkv = z[:, OFF_KV:OFF_KV + 6 * NSA_KV * HD].reshape(DB, 6 * NSA_KV, HD)
    sm = z[:, OFF_SM:OFF_SM + 128].reshape(DB, 1, 128)
    oc, idx = sample_cmp_select(page_table, pool, layer, n_pool, lw["w1c"], cb, lw["phi_w2"], q8,
                                lw["slopes"])
    osw = sample_sel_win(idx, page_table, pool, win, layer, n_pool, q8, newkv, oc, sm, lw["slopes"])
    o_a = jnp.concatenate([osw[:, g, g * NSA_GROUP:(g + 1) * NSA_GROUP] for g in range(NSA_KV)],
                          axis=1).reshape(DB, NSA_WIDTH)
    qkv = z[:, OFF_MQKV:OFF_MQKV + 3 * M_WIDTH].reshape(DB, 3 * M_HEADS, HD)
    mo = z[:, OFF_MO:OFF_MO + M_WIDTH].reshape(DB, M_HEADS, HD)
    gates = z[:, OFF_SM + SM_I:OFF_SM + SM_I + 2 * M_HEADS]
    o_b, Cs, ns, ms = mlstm_step(lw["b_if"], gates, m0, qkv, mo, lw["norm_g"], C0, n0)
    rx = z[:, OFF_RX:OFF_RX + R_WIDTH]
    o_c, hs = rglru_step(rx, z[:, OFF_RGT:OFF_RGT + R_WIDTH], conv0.transpose(1, 0, 2), h0,
                         lw["conv_w"], lw["rg_vec"], lw["w_a"], lw["w_x"])
    merged = merge(o_a, o_b.reshape(DB, M_WIDTH), o_c, z, gw["w_branch"], layer, tm=DB, tn=MERGE_TN,
                   out_dtype=F32)
    h = mm_res_ln(merged, gw["w_out"], layer, xs, lw["ln_g"][0:1], lw["ln_b"][0:1], tm=DB, tk=D)
    f1 = mm(h, gw["mlp_w1"], layer, tm=DB, tn=UP_TN, out_dtype=F32, act="relu2")
    x_new = mm_res_ln(f1, gw["mlp_w2"], layer, h, lw["ln_g"][1:2], lw["ln_b"][1:2], tm=DB, tk=DOWN_TK)
    kvn = newkv.reshape(DB, 1, 6, NSA_KV, HD)
    Wb = win_l.shape[1]
    keep = min(WINDOW, Wb + 1)
    win_new = jnp.concatenate([win_l, kvn[:, :, 4:6]], axis=1)[:, Wb + 1 - keep:]
    conv_new = jnp.concatenate([conv0[:, 1:], rx[:, None]], axis=1)
    states = (kvn[:, :, :4], win_new, Cs, ns, ms[:, :, 0], hs, conv_new)
    return x_new, states


def kernel(x_prompt, x_sample, cache_nsa_kv, cache_win_kv, state_mlstm_C, state_mlstm_n, state_mlstm_m,
           state_rglru_h, state_rglru_conv, page_table, w_in, nsa_pe, nsa_phi_w1, nsa_phi_b1, nsa_phi_w2,
           mlstm_b_if, mlstm_norm_g, rg_conv_w, rg_conv_b, rg_w_a, rg_b_a, rg_w_x, rg_b_x, rg_lambda,
           w_branch, w_out, ln_g, ln_b, mlp_w1, mlp_w2):
    DB, Tn, D = x_sample.shape
    assert Tn == 1 and D == D_MODEL
    depth, n_pool = cache_nsa_kv.shape[:2]
    pool = cache_nsa_kv.reshape(depth * n_pool, ROWS_PER_PAGE, HD)
    Wb = cache_win_kv.shape[2]
    win = cache_win_kv.reshape(depth * DB, Wb * 2 * NSA_KV, HD)
    slopes = alibi_slopes()
    half = CMP_STRIDE * HD
    B, T = x_prompt.shape[:2]
    xp = x_prompt
    xs = x_sample.reshape(DB, D)
    gw = {"w_in": pack_w_in(w_in), "w_branch": w_branch.astype(BF16), "w_out": w_out.astype(BF16),
          "mlp_w1": mlp_w1.astype(BF16), "mlp_w2": mlp_w2.astype(BF16)}
    new_p = [[] for _ in range(6)]
    new_s = [[] for _ in range(7)]
    kv_buf = None
    for l in range(depth):
        lw = {
            "pe": nsa_pe[l], "phi_w1": nsa_phi_w1[l], "phi_b1": nsa_phi_b1[l], "phi_w2": nsa_phi_w2[l],
            "w1c": jnp.concatenate([nsa_phi_w1[l][:, :half], nsa_phi_w1[l][:, half:]], axis=2).astype(BF16),
            "slopes": slopes, "b_if": mlstm_b_if[l], "norm_g": mlstm_norm_g[l],
            "conv_w": rg_conv_w[l],
            "rg_vec": jnp.stack([rg_conv_b[l], rg_b_a[l], rg_b_x[l], rg_lambda[l]]),
            "w_a": rg_w_a[l], "w_x": rg_w_x[l],
            "ln_g": ln_g[l], "ln_b": ln_b[l],
        }
        xp, st_p, cb, kv_buf = prompt_layer(xp, lw, gw, l, kv_buf, depth)
        xs, st_s = sample_layer(xs, lw, gw, cb, l, page_table, pool, n_pool, win, cache_win_kv[l],
                                state_mlstm_C[l], state_mlstm_n[l], state_mlstm_m[l],
                                state_rglru_h[l], state_rglru_conv[l])
        for lst, val in zip(new_p, st_p):
            lst.append(val)
        for lst, val in zip(new_s, st_s):
            lst.append(val)
    P = [jnp.stack(a) for a in new_p]
    S = [jnp.stack(a) for a in new_s]
    nsa_kv_p = kv_buf.reshape(depth, B, T, 4, NSA_KV, HD)
    return (xp, xs.reshape(DB, Tn, D), nsa_kv_p, S[0], P[0], S[1], P[1], S[2], P[2], S[3], P[3], S[4],
            P[4], S[5], P[5], S[6])
```

```python
import functools

import jax
import jax.numpy as jnp
from jax import lax
from jax.experimental import pallas as pl
from jax.experimental.pallas import tpu as pltpu

F32 = jnp.float32
BF16 = jnp.bfloat16
I32 = jnp.int32

D_MODEL = 2048
DEPTH = 4
PAGE_SIZE = 128
HD = 128
NSA_HEADS = 8
NSA_KV = 2
NSA_GROUP = 4
NSA_WIDTH = NSA_HEADS * HD
CMP_BLOCK = 32
CMP_STRIDE = 16
SEL_BLOCK = 64
SEL_TOPK = 16
WINDOW = 512
Q_BLOCK = 128
FORCE_BONUS = 1.0e4
M_HEADS = 8
M_WIDTH = M_HEADS * HD
R_WIDTH = 1024
R_BLOCKS = 8
CONV_W = 4
LRU_C = 8.0
D_FF = 4 * D_MODEL
DEEPNORM_ALPHA = (2 * DEPTH) ** 0.25
LN_EPS = 1e-5
NEG_INF = -1e30

OFF_MG = 0
OFF_Q = 6144
OFF_MQKV = 7168
OFF_MO = 10240
OFF_RX = 11264
OFF_RGT = 12288
OFF_KV = 13312
OFF_SM = 14848
N_AL = 15360
SM_I = 24
SM_F = 32

IN_TM, IN_TN = 1024, 1536
MERGE_TM, MERGE_TN = 512, 2048
OUT_TM = 512
UP_TM, UP_TN = 1024, 1024
DOWN_TM, DOWN_TK = 512, 2048

VMEM_LIMIT = 52 * 1024 * 1024


def _cp(*sem):
    return pltpu.CompilerParams(dimension_semantics=sem, vmem_limit_bytes=VMEM_LIMIT)


def _dot(a, b):
    return jnp.dot(a, b, preferred_element_type=F32)


def _dot_nt(a, b):
    return lax.dot_general(a, b, (((1,), (1,)), ((), ())), preferred_element_type=F32)


def _gelu(x):
    return 0.5 * x * (1.0 + jnp.tanh(0.7978845608028654 * (x + 0.044715 * (x * x * x))))


def _sigmoid(x):
    return 0.5 * jnp.tanh(0.5 * x) + 0.5


def _softplus(x):
    return jnp.maximum(x, 0.0) + jnp.log1p(jnp.exp(-jnp.abs(x)))


def _mm_body(a_ref, b_ref, o_ref, *, act):
    acc = _dot(a_ref[...].astype(BF16), b_ref[...])
    if act == "relu2":
        acc = jnp.square(jnp.maximum(acc, 0.0))
    o_ref[...] = acc.astype(o_ref.dtype)


def _mm_nt_body(a_ref, bt_ref, o_ref):
    o_ref[...] = _dot_nt(a_ref[...].astype(BF16), bt_ref[...]).astype(o_ref.dtype)


def mm_nt(a, bt, layer, *, tm, tn, out_dtype):
    M, K = a.shape
    N = bt.shape[1]
    assert M % tm == 0 and N % tn == 0
    return pl.pallas_call(
        _mm_nt_body,
        grid=(M // tm, N // tn),
        in_specs=[pl.BlockSpec((tm, K), lambda i, j: (i, 0)),
                  pl.BlockSpec((None, tn, K), lambda i, j: (layer, j, 0))],
        out_specs=pl.BlockSpec((tm, tn), lambda i, j: (i, j)),
        out_shape=jax.ShapeDtypeStruct((M, N), out_dtype),
        compiler_params=_cp("parallel", "arbitrary"),
        name="mm_nt",
    )(a, bt)


def mm(a, b, layer, *, tm, tn, out_dtype, act=None):
    M, K = a.shape
    N = b.shape[2]
    assert M % tm == 0 and N % tn == 0
    return pl.pallas_call(
        functools.partial(_mm_body, act=act),
        grid=(M // tm, N // tn),
        in_specs=[pl.BlockSpec((tm, K), lambda i, j: (i, 0)),
                  pl.BlockSpec((None, K, tn), lambda i, j: (layer, 0, j))],
        out_specs=pl.BlockSpec((tm, tn), lambda i, j: (i, j)),
        out_shape=jax.ShapeDtypeStruct((M, N), out_dtype),
        compiler_params=_cp("parallel", "arbitrary"),
        name="mm",
    )(a, b)


def _mm_ln_body(a_ref, b_ref, x_ref, g_ref, bb_ref, o_ref, acc_ref, *, nk):
    k = pl.program_id(1)

    if nk > 1:
        @pl.when(k == 0)
        def _():
            acc_ref[...] = _dot(a_ref[...].astype(BF16), b_ref[...])

        @pl.when((k > 0) & (k < nk - 1))
        def _():
            acc_ref[...] += _dot(a_ref[...].astype(BF16), b_ref[...])

    @pl.when(k == nk - 1)
    def _():
        y = DEEPNORM_ALPHA * x_ref[...] + _dot(a_ref[...].astype(BF16), b_ref[...])
        if nk > 1:
            y = y + acc_ref[...]
        mu = jnp.mean(y, axis=-1, keepdims=True)
        yc = y - mu
        var = jnp.mean(yc * yc, axis=-1, keepdims=True)
        o_ref[...] = yc * lax.rsqrt(var + LN_EPS) * g_ref[...] + bb_ref[...]


def mm_res_ln(a, b, layer, x, g, bb, *, tm, tk):
    M, K = a.shape
    N = b.shape[2]
    nk = K // tk
    assert M % tm == 0 and K % tk == 0
    return pl.pallas_call(
        functools.partial(_mm_ln_body, nk=nk),
        grid=(M // tm, nk),
        in_specs=[pl.BlockSpec((tm, tk), lambda i, k: (i, k)),
                  pl.BlockSpec((None, tk, N), lambda i, k: (layer, k, 0)),
                  pl.BlockSpec((tm, N), lambda i, k: (i, 0)),
                  pl.BlockSpec((1, N), lambda i, k: (0, 0)),
                  pl.BlockSpec((1, N), lambda i, k: (0, 0))],
        out_specs=pl.BlockSpec((tm, N), lambda i, k: (i, 0)),
        out_shape=jax.ShapeDtypeStruct((M, N), F32),
        scratch_shapes=[pltpu.VMEM((tm, N), F32)],
        compiler_params=_cp("parallel", "arbitrary"),
        name="mm_res_ln",
    )(a, b, x, g, bb)


def _merge_body(oa_ref, ob_ref, oc_ref, g0_ref, g1_ref, g2_ref, w_ref, o_ref):
    acc = _sigmoid(g0_ref[...]) * _dot(oa_ref[...].astype(BF16), w_ref[0])
    acc += _sigmoid(g1_ref[...]) * _dot(ob_ref[...].astype(BF16), w_ref[1])
    acc += _sigmoid(g2_ref[...]) * _dot(oc_ref[...].astype(BF16), w_ref[2])
    o_ref[...] = acc.astype(o_ref.dtype)


def merge(o_a, o_b, o_c, z, w_branch, layer, *, tm, tn, out_dtype):
    M, W = o_a.shape
    N = w_branch.shape[3]
    assert OFF_MG % tn == 0 and N % tn == 0 and M % tm == 0
    gb = OFF_MG // tn
    nb = N // tn
    br = pl.BlockSpec((tm, W), lambda i, j: (i, 0))

    def gspec(k):
        return pl.BlockSpec((tm, tn), lambda i, j: (i, gb + k * nb + j))

    return pl.pallas_call(
        _merge_body,
        grid=(M // tm, nb),
        in_specs=[br, br, br, gspec(0), gspec(1), gspec(2),
                  pl.BlockSpec((None, 3, W, tn), lambda i, j: (layer, 0, 0, j),
                               pipeline_mode=pl.Buffered(1) if nb == 1 else None)],
        out_specs=pl.BlockSpec((tm, tn), lambda i, j: (i, j)),
        out_shape=jax.ShapeDtypeStruct((M, N), out_dtype),
        compiler_params=_cp("parallel", "arbitrary"),
        name="merge",
    )(o_a, o_b, o_c, z, z, z, w_branch)


def _kv_rows_body(z_ref, *rest):
    o_ref = rest[-1]
    tm = z_ref.shape[0]
    for kg in range(4 * NSA_KV):
        o_ref[pl.ds(kg, tm, stride=4 * NSA_KV), :] = z_ref[:, kg * HD:(kg + 1) * HD]


def kv_rows(z, buf, layer, depth):
    M = z.shape[0]
    tm = min(512, M)
    nb = M // tm
    W = 4 * NSA_KV * HD
    assert OFF_KV % W == 0 and M % tm == 0
    in_specs = [pl.BlockSpec((tm, W), lambda i: (i, OFF_KV // W))]
    args = [z]
    aliases = {}
    if buf is not None:
        in_specs.append(pl.BlockSpec(memory_space=pl.ANY))
        args.append(buf)
        aliases = {1: 0}
    return pl.pallas_call(
        _kv_rows_body,
        grid=(nb,),
        in_specs=in_specs,
        out_specs=pl.BlockSpec((tm * 4 * NSA_KV, HD), lambda i: (layer * nb + i, 0)),
        out_shape=jax.ShapeDtypeStruct((depth * M * 4 * NSA_KV, HD), F32),
        input_output_aliases=aliases,
        compiler_params=_cp("arbitrary"),
        name="kv_rows",
    )(*args)


def _compress_body(z_ref, w1_ref, pe_ref, b1_ref, w2_ref, o_ref, cb_ref, *, n):
    half = CMP_STRIDE * HD
    x = jnp.concatenate([z_ref[0, pl.ds(c, n, stride=CMP_STRIDE), :] for c in range(CMP_STRIDE)],
                        axis=1).astype(BF16)
    w1 = w1_ref[0].astype(BF16)
    lead = _dot(x, w1[:half])
    trail = _dot(x, w1[half:])
    pe8 = jnp.broadcast_to(pe_ref[0], (8, 2 * half)).astype(BF16)
    cb = _dot(pe8, w1) + b1_ref[0]
    trail_next = pltpu.roll(trail, n - 1, axis=0)
    hid = _gelu(lead + trail_next + cb[0:1])
    o_ref[0, 0] = _dot(hid.astype(BF16), w2_ref[0].astype(BF16))
    cb_ref[0] = cb


def compress_from_z(z3, w1, pe, b1, w2):
    B, T, _ = z3.shape
    G = NSA_KV
    S = B * G
    n = T // CMP_STRIDE
    K = CMP_STRIDE * HD
    kvb = OFF_KV // HD
    out, cb = pl.pallas_call(
        functools.partial(_compress_body, n=n),
        grid=(2, S),
        in_specs=[pl.BlockSpec((1, T, HD), lambda k, s: (s // G, 0, kvb + k * G + s % G)),
                  pl.BlockSpec((1, 2 * K, HD), lambda k, s: (k, 0, 0)),
                  pl.BlockSpec((1, 1, 2 * K), lambda k, s: (k, 0, 0)),
                  pl.BlockSpec((1, 1, HD), lambda k, s: (k, 0, 0)),
                  pl.BlockSpec((1, HD, HD), lambda k, s: (k, 0, 0))],
        out_specs=[pl.BlockSpec((1, 1, n, HD), lambda k, s: (k, s, 0, 0)),
                   pl.BlockSpec((1, 8, HD), lambda k, s: (k, 0, 0))],
        out_shape=[jax.ShapeDtypeStruct((2, S, n, HD), F32),
                   jax.ShapeDtypeStruct((2, 8, HD), F32)],
        compiler_params=_cp("arbitrary", "arbitrary"),
        name="compress_prompt",
    )(z3, w1, pe.reshape(2, 1, CMP_BLOCK * HD), b1.reshape(2, 1, HD), w2)
    return out[0], out[1], cb


def alibi_slopes():
    return jnp.asarray([2.0 ** (-8.0 * (h + 1) / NSA_HEADS) for h in range(NSA_HEADS)], F32)


SEL_CHUNK = 512


def _online_update(carry, s, mask, v):
    m, l, acc = carry
    s = jnp.where(mask, s, NEG_INF)
    m_new = jnp.maximum(m, jnp.max(s, axis=1, keepdims=True))
    alpha = jnp.exp(m - m_new)
    p = jnp.where(mask, jnp.exp(s - m_new), 0.0)
    l = alpha * l + jnp.sum(p, axis=1, keepdims=True)
    acc = alpha * acc + _dot(p.astype(BF16), v)
    return m_new, l, acc


MASK_BIG = 1.0e30
LOG2E = 1.4426950408889634
WIN_KEYS = WINDOW + Q_BLOCK


def _nsa_prompt_body(slopes_ref, q_ref, ks_ref, vs_ref, kw_ref, vw_ref, kc_ref, vc_ref,
                     sm_ref, o_ref, ks_bf, vs_aug, et_bf, kw_pad, vw_aug, wbias, sbias, y_a, y_b,
                     clist, *, T):
    assert T % (2 * SEL_CHUNK) == 0
    g = pl.program_id(1)
    i = pl.program_id(2)
    QB = Q_BLOCK
    R = NSA_GROUP
    RQ = R * QB
    NCP = T // CMP_STRIDE
    NC = NCP - 1
    NS = T // SEL_BLOCK
    NSP = 128
    assert NS <= NSP
    t0 = i * QB
    scale = HD ** -0.5
    c2 = scale * LOG2E

    @pl.when(i == 0)
    def _():
        lane = lax.broadcasted_iota(I32, (T, HD), 1)
        ones_col = jnp.where(lane == 0, 1.0, 0.0).astype(BF16)
        ks_bf[...] = ks_ref[0].astype(BF16)
        vs_aug[:, 0:HD] = vs_ref[0].astype(BF16)
        vs_aug[:, HD:2 * HD] = ones_col
        kk = lax.broadcasted_iota(I32, (T, HD), 0)
        et_bf[...] = jnp.where(kk // SEL_BLOCK == lane, MASK_BIG, 0.0).astype(BF16)
        kw_pad[0:WINDOW, :] = jnp.zeros((WINDOW, HD), BF16)
        kw_pad[WINDOW:WINDOW + T, :] = kw_ref[0].astype(BF16)
        vw_aug[0:WINDOW, :] = jnp.zeros((WINDOW, 2 * HD), BF16)
        vw_aug[WINDOW:WINDOW + T, 0:HD] = vw_ref[0].astype(BF16)
        vw_aug[WINDOW:WINDOW + T, HD:2 * HD] = ones_col
        wd = (lax.broadcasted_iota(I32, (QB, WIN_KEYS), 0) + WINDOW
              - lax.broadcasted_iota(I32, (QB, WIN_KEYS), 1))
        band = (wd >= 0) & (wd < WINDOW)
        wdf = wd.astype(F32)
        lf = lax.broadcasted_iota(I32, (QB, SEL_CHUNK), 1).astype(F32)
        for r in range(R):
            sl = slopes_ref[g * R + r]
            wbias[r * QB:(r + 1) * QB, :] = jnp.where(band, (-LOG2E * sl) * wdf, -MASK_BIG)
            sbias[r * QB:(r + 1) * QB, :] = (LOG2E * sl) * lf

    qb = q_ref[0]
    q4 = jnp.concatenate([qb[:, r * HD:(r + 1) * HD] for r in range(R)], axis=0).astype(BF16)
    row = lax.broadcasted_iota(I32, (RQ, 1), 0)
    qpos = t0 + (row & (QB - 1))
    slope = jnp.concatenate(
        [jnp.full((QB, 1), slopes_ref[g * R + r], F32) for r in range(R)], axis=0)

    kc = kc_ref[0, 0].astype(BF16)
    vc = vc_ref[0, 0].astype(BF16)
    n_idx = lax.broadcasted_iota(I32, (1, NCP), 1)
    dist = qpos - (n_idx * CMP_STRIDE + (CMP_BLOCK - 1))
    mask = (dist >= 0) & (n_idx < NC)
    s = _dot_nt(q4, kc) * scale - slope * dist.astype(F32)
    s = jnp.where(mask, s, NEG_INF)
    m = jnp.max(s, axis=1, keepdims=True)
    p = jnp.where(mask, jnp.exp(s - m), 0.0)
    l = jnp.sum(p, axis=1, keepdims=True)
    p = p * (1.0 / jnp.where(l > 0.0, l, 1.0))
    o_c = _dot(p.astype(BF16), vc)

    psum = p[0:QB]
    for r in range(1, R):
        psum = psum + p[r * QB:(r + 1) * QB]
    p_hi = psum.astype(BF16)
    p_lo = (psum - p_hi.astype(F32)).astype(BF16)
    jo = lax.broadcasted_iota(I32, (NS, NCP), 0) * SEL_BLOCK
    no = lax.broadcasted_iota(I32, (NS, NCP), 1)
    ov = ((no * CMP_STRIDE < jo + SEL_BLOCK) & (no * CMP_STRIDE + CMP_BLOCK > jo)
          & (no < NC)).astype(BF16)
    imp = _dot_nt(ov, p_hi) + _dot_nt(ov, p_lo)
    jj = lax.broadcasted_iota(I32, (NS, QB), 0)
    qp = t0 + lax.broadcasted_iota(I32, (NS, QB), 1)
    cur = qp // SEL_BLOCK
    forced = (jj == 0) | (jj == cur) | (jj == cur - 1)
    imp = jnp.where(forced, imp + FORCE_BONUS, imp)
    imp = jnp.where(jj * SEL_BLOCK <= qp, imp, -1.0)
    ranks = [jnp.zeros((8, QB), F32) for _ in range(NS // 8)]
    for j2 in range(NS):
        rv = imp[j2:j2 + 1, :]
        for v in range(NS // 8):
            blk = imp[8 * v:8 * v + 8]
            if 8 * v > j2:
                beats = rv >= blk
            elif 8 * v + 7 < j2:
                beats = rv > blk
            else:
                beats = (rv > blk) | ((rv == blk) & (jj[8 * v:8 * v + 8] > j2))
            ranks[v] = ranks[v] + jnp.where(beats, 1.0, 0.0)
    rank = jnp.concatenate(ranks, axis=0)
    selm_t = jnp.where(rank < SEL_TOPK, 0.0, -1.0)
    if NSP > NS:
        selm_t = jnp.concatenate([selm_t, jnp.zeros((NSP - NS, QB), F32)], axis=0)
    selm = selm_t.T.astype(BF16)

    sl_col = LOG2E * slope
    rel = (lax.broadcasted_iota(I32, (QB, SEL_CHUNK), 1)
           - lax.broadcasted_iota(I32, (QB, SEL_CHUNK), 0))

    def scores(c):
        k0 = pl.multiple_of(c * SEL_CHUNK, SEL_CHUNK)
        mb = _dot_nt(selm, et_bf[pl.ds(k0, SEL_CHUNK), :])
        mb = jnp.where(rel <= t0 - k0, mb, -MASK_BIG)
        return (_dot_nt(q4, ks_bf[pl.ds(k0, SEL_CHUNK), :]) * c2 + sbias[...]
                + jnp.concatenate([mb] * R, axis=0))

    def consume(c, y, carry):
        m, acc = carry
        k0 = pl.multiple_of(c * SEL_CHUNK, SEL_CHUNK)
        off = sl_col * (k0 - t0).astype(F32)
        m_new = jnp.maximum(m, jnp.max(y, axis=1, keepdims=True) + off)
        p = jnp.exp2(y - (m_new - off))
        acc = jnp.exp2(m - m_new) * acc + _dot(p.astype(BF16), vs_aug[pl.ds(k0, SEL_CHUNK), :])
        return m_new, acc

    n_chunks = (t0 + QB + SEL_CHUNK - 1) // SEL_CHUNK
    bpc = SEL_CHUNK // SEL_BLOCK
    n_act = jnp.int32(0)
    dummy = jnp.int32(T // SEL_CHUNK - 1)
    for c in reversed(range(T // SEL_CHUNK)):
        act = (jnp.sum(selm_t[c * bpc:(c + 1) * bpc] + 1.0) > 0.0) & (c < n_chunks)
        dummy = jnp.where(act, dummy, c)
    for c in range(T // SEL_CHUNK):
        act = (jnp.sum(selm_t[c * bpc:(c + 1) * bpc] + 1.0) > 0.0) & (c < n_chunks)
        clist[n_act] = c
        n_act = n_act + act.astype(I32)
    clist[n_act] = dummy

    def pair(pi, carry):
        k = 2 * pi
        y_b[...] = scores(clist[k + 1])
        carry = consume(clist[k], y_a[...], carry)
        y_a[...] = scores(clist[jnp.minimum(k + 2, n_act)])
        return consume(clist[k + 1], y_b[...], carry)

    init = (jnp.full((RQ, 1), -MASK_BIG, F32), jnp.zeros((RQ, 2 * HD), F32))
    y_a[...] = scores(clist[0])
    _, acc_s = lax.fori_loop(0, (n_act + 1) // 2, pair, init)
    o_s = acc_s[:, 0:HD] * (1.0 / acc_s[:, HD:HD + 1])

    w0 = pl.multiple_of(t0, QB)
    vrow = jnp.where(lax.broadcasted_iota(I32, (1, WIN_KEYS), 1) >= WINDOW - t0, 0.0, -MASK_BIG)
    y = _dot_nt(q4, kw_pad[pl.ds(w0, WIN_KEYS), :]) * c2 + wbias[...] + vrow
    p = jnp.exp2(y - jnp.max(y, axis=1, keepdims=True))
    acc_w = _dot(p.astype(BF16), vw_aug[pl.ds(w0, WIN_KEYS), :])
    o_w = acc_w[:, 0:HD] * (1.0 / acc_w[:, HD:HD + 1])

    smb = sm_ref[0]
    lane = lax.broadcasted_iota(I32, (QB, 128), 1)

    def gate(c):
        cols = []
        for r in range(R):
            idx = (g * R + r) * 3 + c
            cols.append(jnp.sum(jnp.where(lane == idx, smb, 0.0), axis=1, keepdims=True))
        return _sigmoid(jnp.concatenate(cols, axis=0))

    o = gate(0) * o_c + gate(1) * o_s + gate(2) * o_w
    o_ref[0] = jnp.concatenate([o[r * QB:(r + 1) * QB] for r in range(R)], axis=1).astype(o_ref.dtype)


def nsa_prompt(z3, kc, vc, slopes):
    B, T, _ = z3.shape
    G = NSA_KV
    kvb = OFF_KV // HD

    def kvspec(kind):
        return pl.BlockSpec((1, T, HD), lambda b, g, i: (b, 0, kvb + kind * G + g))

    cspec = pl.BlockSpec((1, 1, T // CMP_STRIDE, HD), lambda b, g, i: (b * G + g, 0, 0, 0))
    GW = NSA_GROUP * HD
    assert OFF_Q % GW == 0
    qspec = pl.BlockSpec((1, Q_BLOCK, GW), lambda b, g, i: (b, i, OFF_Q // GW + g))
    return pl.pallas_call(
        functools.partial(_nsa_prompt_body, T=T),
        grid=(B, G, T // Q_BLOCK),
        in_specs=[pl.BlockSpec(memory_space=pltpu.SMEM),
                  qspec, kvspec(2), kvspec(3), kvspec(4), kvspec(5), cspec, cspec,
                  pl.BlockSpec((1, Q_BLOCK, 128), lambda b, g, i: (b, i, OFF_SM // 128))],
        out_specs=pl.BlockSpec((1, Q_BLOCK, GW), lambda b, g, i: (b, i, g)),
        out_shape=jax.ShapeDtypeStruct((B, T, NSA_WIDTH), BF16),
        scratch_shapes=[pltpu.VMEM((T, HD), BF16),
                        pltpu.VMEM((T, 2 * HD), BF16),
                        pltpu.VMEM((T, HD), BF16),
                        pltpu.VMEM((T + WINDOW, HD), BF16),
                        pltpu.VMEM((T + WINDOW, 2 * HD), BF16),
                        pltpu.VMEM((NSA_GROUP * Q_BLOCK, WIN_KEYS), F32),
                        pltpu.VMEM((NSA_GROUP * Q_BLOCK, SEL_CHUNK), F32),
                        pltpu.VMEM((NSA_GROUP * Q_BLOCK, SEL_CHUNK), F32),
                        pltpu.VMEM((NSA_GROUP * Q_BLOCK, SEL_CHUNK), F32),
                        pltpu.SMEM((T // SEL_CHUNK + 8,), I32)],
        compiler_params=_cp("arbitrary", "arbitrary", "arbitrary"),
        name="nsa_prompt",
    )(slopes, z3, z3, z3, z3, z3, kc[:, None], vc[:, None], z3)


M_CHUNK_K = 128


M_HEADS_PER_STEP = 4
M_ROWS_PER_STEP = 512


def _mlstm_body(bif_ref, q_ref, k_ref, v_ref, sm_ref, mo_ref, ng_ref,
                o_ref, c_ref, n_ref, m_ref, c_sc, n_sc, m_sc, *, NT, TB):
    hb = pl.program_id(1)
    t = pl.program_id(2)
    L = M_CHUNK_K
    HB = M_HEADS_PER_STEP
    lane = lax.broadcasted_iota(I32, (L, 128), 1)
    ri = lax.broadcasted_iota(I32, (L, L), 0)
    ci = lax.broadcasted_iota(I32, (L, L), 1)
    eye = ri == ci
    tril = ri >= ci

    @pl.when(t == 0)
    def _():
        c_sc[...] = jnp.zeros_like(c_sc)
        n_sc[...] = jnp.zeros_like(n_sc)
        m_sc[...] = jnp.zeros_like(m_sc)

    def head_chunk(hh, r0):
        h = hb * HB + hh
        cs = slice(hh * HD, (hh + 1) * HD)
        bi = bif_ref[0, h]
        bf = bif_ref[1, h]
        ng = ng_ref[:, cs]
        C = c_sc[hh]
        n = n_sc[hh]
        m = m_sc[hh][:, 0:1]
        q = q_ref[0, pl.ds(r0, L), cs]
        k = k_ref[0, pl.ds(r0, L), cs] * (HD ** -0.5)
        v = v_ref[0, pl.ds(r0, L), cs]
        smb = sm_ref[0, pl.ds(r0, L), :]
        ig_col = jnp.sum(jnp.where(lane == SM_I + h, smb, 0.0), axis=1, keepdims=True) + bi
        fp_col = jnp.sum(jnp.where(lane == SM_F + h, smb, 0.0), axis=1, keepdims=True) + bf
        lf_col = -_softplus(-fp_col)
        lf_row = jnp.sum(jnp.where(eye, lf_col, 0.0), axis=0, keepdims=True)
        ig_row = jnp.sum(jnp.where(eye, ig_col, 0.0), axis=0, keepdims=True)
        b_col = jnp.sum(jnp.where(tril, lf_row, 0.0), axis=1, keepdims=True)
        b_row = jnp.sum(jnp.where(ri <= ci, lf_col, 0.0), axis=0, keepdims=True)
        d = jnp.where(tril, b_col - b_row + ig_row, NEG_INF)
        inter = b_col + m
        m_t = jnp.maximum(inter, jnp.max(d, axis=1, keepdims=True))
        w_intra = jnp.exp(d - m_t)
        w_inter = jnp.exp(inter - m_t)
        qb = q.astype(BF16)
        vb = v.astype(BF16)
        s = _dot_nt(qb, k.astype(BF16)) * w_intra
        num = _dot(s.astype(BF16), vb) + w_inter * _dot(qb, C.astype(BF16))
        den = jnp.sum(s, axis=1, keepdims=True) + w_inter * jnp.sum(q * n, axis=1, keepdims=True)
        hh_ = num * (1.0 / jnp.maximum(jnp.abs(den), jnp.exp(-m_t)))
        m_new = m_t[L - 1:L, :]
        b_last = b_col[L - 1:L, :]
        decay = jnp.exp(b_last + m - m_new)
        w_s = jnp.exp(b_last - b_col + ig_col - m_new)
        kw = k * w_s
        c_sc[hh] = decay * C + _dot(kw.T.astype(BF16), vb)
        n_sc[hh] = decay * n + jnp.sum(kw, axis=0, keepdims=True)
        m_sc[hh] = jnp.broadcast_to(m_new, (1, 128))
        mu = jnp.mean(hh_, axis=1, keepdims=True)
        hc = hh_ - mu
        var = jnp.mean(hc * hc, axis=1, keepdims=True)
        hn = hc * lax.rsqrt(var + LN_EPS) * ng
        o_ref[0, pl.ds(r0, L), cs] = (_sigmoid(mo_ref[0, pl.ds(r0, L), cs]) * hn).astype(o_ref.dtype)

    def sub(c, carry):
        r0 = pl.multiple_of(c * L, L)
        for hh in range(HB):
            head_chunk(hh, r0)
        return carry

    lax.fori_loop(0, TB // L, sub, 0)

    @pl.when(t == NT - 1)
    def _():
        c_ref[0] = c_sc[...]
        n_ref[0] = n_sc[...]
        m_ref[0] = m_sc[...]


def mlstm_prompt(z3, b_if, norm_g):
    B, T, _ = z3.shape
    H = M_HEADS
    HB = M_HEADS_PER_STEP
    TB = min(M_ROWS_PER_STEP, T)
    W = HB * HD
    assert OFF_MQKV % W == 0 and OFF_MO % W == 0 and T % TB == 0

    def hspec(off):
        return pl.BlockSpec((1, TB, W), lambda b, h, t: (b, t, off // W + h))

    return pl.pallas_call(
        functools.partial(_mlstm_body, NT=T // TB, TB=TB),
        grid=(B, H // HB, T // TB),
        in_specs=[pl.BlockSpec(memory_space=pltpu.SMEM),
                  hspec(OFF_MQKV), hspec(OFF_MQKV + M_WIDTH), hspec(OFF_MQKV + 2 * M_WIDTH),
                  pl.BlockSpec((1, TB, 128), lambda b, h, t: (b, t, OFF_SM // 128)),
                  hspec(OFF_MO),
                  pl.BlockSpec((1, W), lambda b, h, t: (0, h))],
        out_specs=[pl.BlockSpec((1, TB, W), lambda b, h, t: (b, t, h)),
                   pl.BlockSpec((1, HB, HD, HD), lambda b, h, t: (b, h, 0, 0)),
                   pl.BlockSpec((1, HB, 1, HD), lambda b, h, t: (b, h, 0, 0)),
                   pl.BlockSpec((1, HB, 1, 128), lambda b, h, t: (b, h, 0, 0))],
        out_shape=[jax.ShapeDtypeStruct((B, T, M_WIDTH), BF16),
                   jax.ShapeDtypeStruct((B, H, HD, HD), F32),
                   jax.ShapeDtypeStruct((B, H, 1, HD), F32),
                   jax.ShapeDtypeStruct((B, H, 1, 128), F32)],
        scratch_shapes=[pltpu.VMEM((HB, HD, HD), F32), pltpu.VMEM((HB, 1, HD), F32),
                        pltpu.VMEM((HB, 1, 128), F32)],
        compiler_params=_cp("arbitrary", "arbitrary", "arbitrary"),
        name="mlstm_prompt",
    )(b_if, z3, z3, z3, z3, z3, norm_g.reshape(1, M_WIDTH))


R_CHUNK = 128


def _rglru_gates(xc, was, wxs, vec):
    xb = xc.astype(BF16)
    nb = len(was)
    ya = jnp.concatenate([_dot(xb[:, j * 128:(j + 1) * 128], was[j]) for j in range(nb)], axis=1)
    yx = jnp.concatenate([_dot(xb[:, j * 128:(j + 1) * 128], wxs[j]) for j in range(nb)], axis=1)
    r = _sigmoid(ya + vec[1:2])
    i = _sigmoid(yx + vec[2:3])
    log_a = -LRU_C * r * _softplus(-vec[3:4])
    a = jnp.exp(log_a)
    at = -jnp.tanh(log_a)
    u = jnp.sqrt(2.0 * at) * lax.rsqrt(1.0 + at) * (i * xc)
    return a, u


R_BLOCKS_PER_STEP = 2


def _rglru_body(rx_ref, rgt_ref, cw_ref, vec_ref, wa_ref, wx_ref, o_ref, hl_ref, xpad_ref, *, T):
    Tc = R_CHUNK
    W = rx_ref.shape[2]
    xpad_ref[0:8, :] = jnp.zeros((8, W), F32)
    xpad_ref[8:T + 8, :] = rx_ref[0]
    cw = cw_ref[...]
    vec = vec_ref[...]
    was = [wa_ref[j].astype(BF16) for j in range(W // 128)]
    wxs = [wx_ref[j].astype(BF16) for j in range(W // 128)]
    rowmod = lax.broadcasted_iota(I32, (Tc, W), 0) & 7

    def chunk(c, h):
        r0 = pl.multiple_of(c * Tc, Tc)
        win = xpad_ref[pl.ds(r0, Tc + 8), :]
        xc = vec[0:1]
        for j in range(CONV_W):
            off = 8 - (CONV_W - 1) + j
            xc = xc + win[off:off + Tc] * cw[j:j + 1]
        a, u = _rglru_gates(xc, was, wxs, vec)
        for sft in (1, 2, 4):
            a1 = pltpu.roll(a, sft, axis=0)
            u1 = pltpu.roll(u, sft, axis=0)
            ok = rowmod >= sft
            u = jnp.where(ok, a * u1 + u, u)
            a = jnp.where(ok, a * a1, a)
        hs = []
        for gi in range(Tc // 8):
            hg = a[gi * 8:(gi + 1) * 8] * h + u[gi * 8:(gi + 1) * 8]
            hs.append(hg)
            h = hg[7:8]
        hf = jnp.concatenate(hs, axis=0)
        o_ref[0, pl.ds(r0, Tc), :] = (hf * _gelu(rgt_ref[0, pl.ds(r0, Tc), :])).astype(o_ref.dtype)
        return h

    h = lax.fori_loop(0, T // Tc, chunk, jnp.zeros((1, W), F32))
    hl_ref[0] = h


def rglru_prompt(z3, conv_w, vec, w_a, w_x):
    B, T, _ = z3.shape
    RB = R_BLOCKS_PER_STEP
    W = RB * 128
    assert OFF_RX % W == 0 and OFF_RGT % W == 0 and R_BLOCKS % RB == 0
    return pl.pallas_call(
        functools.partial(_rglru_body, T=T),
        grid=(B, R_BLOCKS // RB),
        in_specs=[pl.BlockSpec((1, T, W), lambda b, n: (b, 0, OFF_RX // W + n)),
                  pl.BlockSpec((1, T, W), lambda b, n: (b, 0, OFF_RGT // W + n)),
                  pl.BlockSpec((CONV_W, W), lambda b, n: (0, n)),
                  pl.BlockSpec((4, W), lambda b, n: (0, n)),
                  pl.BlockSpec((RB, 128, 128), lambda b, n: (n, 0, 0)),
                  pl.BlockSpec((RB, 128, 128), lambda b, n: (n, 0, 0))],
        out_specs=[pl.BlockSpec((1, T, W), lambda b, n: (b, 0, n)),
                   pl.BlockSpec((1, 1, W), lambda b, n: (b, 0, n))],
        out_shape=[jax.ShapeDtypeStruct((B, T, R_WIDTH), BF16),
                   jax.ShapeDtypeStruct((B, 1, R_WIDTH), F32)],
        scratch_shapes=[pltpu.VMEM((T + 8, W), F32)],
        compiler_params=_cp("arbitrary", "arbitrary"),
        name="rglru_prompt",
    )(z3, z3, conv_w, vec, w_a, w_x)


ROWS_PER_PAGE = PAGE_SIZE * 4 * NSA_KV
CHUNKS_PER_PAGE = PAGE_SIZE // CMP_STRIDE
PAGES_PER_STEP = 16


def _sample_cmp_body(pt_ref, slopes_ref, *refs, NP, GRP):
    page_refs = refs[:GRP]
    w1c_ref, cb_ref, w2_ref, q_ref, oc_ref, idx_ref, xs_ref, lt_ref = refs[GRP:]
    p = pl.program_id(1)
    P = NP * PAGE_SIZE
    CPG = GRP * CHUNKS_PER_PAGE
    NCH = NP * CHUNKS_PER_PAGE
    n_sel = P // SEL_BLOCK + 1
    NSP = -(-n_sel // 128) * 128
    scale = HD ** -0.5

    for k in range(GRP):
        for kg in range(2 * NSA_KV):
            plane = page_refs[k][0, pl.ds(kg, PAGE_SIZE, stride=4 * NSA_KV), :]
            xs_ref[kg, k * CHUNKS_PER_PAGE:(k + 1) * CHUNKS_PER_PAGE, :] = plane.reshape(
                CHUNKS_PER_PAGE, CMP_STRIDE * HD)

    r0 = pl.multiple_of(p * CPG, CPG)
    for kind in range(2):
        x = jnp.concatenate([xs_ref[kind * NSA_KV + g] for g in range(NSA_KV)], axis=0).astype(BF16)
        lt = _dot(x, w1c_ref[kind])
        for g in range(NSA_KV):
            lt_ref[kind * NSA_KV + g, pl.ds(r0, CPG), :] = lt[g * CPG:(g + 1) * CPG]

    @pl.when(p == NP // GRP - 1)
    def _():
        q128 = jnp.concatenate([q_ref[0], jnp.zeros((128 - NSA_HEADS, HD), F32)], axis=0).astype(BF16)
        lane = lax.broadcasted_iota(I32, (1, 128), 1)
        slope_row = jnp.zeros((1, 128), F32)
        for h in range(NSA_HEADS):
            slope_row = jnp.where(lane == h, slopes_ref[h], slope_row)
        n_col = lax.broadcasted_iota(I32, (NCH, 1), 0)
        dist = P - (n_col * CMP_STRIDE + (CMP_BLOCK - 1))
        maskc = (dist >= 0) & (n_col < NCH - 1)
        dist_f = dist.astype(F32)

        def cmp_rows(kind, g):
            lead = lt_ref[kind * NSA_KV + g, :, 0:HD]
            trail = lt_ref[kind * NSA_KV + g, :, HD:2 * HD]
            hid = _gelu(lead + pltpu.roll(trail, NCH - 1, axis=0) + cb_ref[kind, 0:1])
            return _dot(hid.astype(BF16), w2_ref[kind].astype(BF16)).astype(BF16)

        oc = jnp.zeros((128, HD), F32)
        psum2 = jnp.zeros((NCH, 128), F32)
        for g in range(NSA_KV):
            kc = cmp_rows(0, g)
            vc = cmp_rows(1, g)
            s = _dot_nt(kc, q128) * scale - slope_row * dist_f
            s = jnp.where(maskc, s, NEG_INF)
            m = jnp.max(s, axis=0, keepdims=True)
            pt = jnp.where(maskc, jnp.exp(s - m), 0.0)
            l = jnp.sum(pt, axis=0, keepdims=True)
            pt = pt / jnp.where(l > 0.0, l, 1.0)
            in_g = (lane >= g * NSA_GROUP) & (lane < (g + 1) * NSA_GROUP)
            pg = jnp.where(in_g, pt, 0.0)
            oc = oc + _dot(pg.T.astype(BF16), vc)
            psum2 = psum2 + jnp.where(lane == g, jnp.sum(pg, axis=1, keepdims=True), 0.0)
        oc_ref[0] = oc[0:NSA_HEADS]

        p_hi = psum2.astype(BF16)
        p_lo = (psum2 - p_hi.astype(F32)).astype(BF16)
        jo = lax.broadcasted_iota(I32, (NSP, NCH), 0) * SEL_BLOCK
        no = lax.broadcasted_iota(I32, (NSP, NCH), 1)
        ov = ((no * CMP_STRIDE < jo + SEL_BLOCK) & (no * CMP_STRIDE + CMP_BLOCK > jo)
              & (no < NCH - 1)).astype(BF16)
        imp = _dot(ov, p_hi) + _dot(ov, p_lo)
        jcol = lax.broadcasted_iota(I32, (NSP, 1), 0)
        cur = P // SEL_BLOCK
        forced = (jcol == 0) | (jcol == cur) | (jcol == cur - 1)
        imp = jnp.where(forced, imp + FORCE_BONUS, imp)
        imp = jnp.where(jcol * SEL_BLOCK <= P, imp, -1.0)
        imp = jnp.where(jcol < n_sel, imp, -2.0)
        ri = lax.broadcasted_iota(I32, (NSP, NSP), 0)
        ci = lax.broadcasted_iota(I32, (NSP, NSP), 1)
        lane_n = lax.broadcasted_iota(I32, (NSP, 128), 1)
        jcol_f = jcol.astype(F32)
        idx_ref[0] = jnp.zeros((8, 128), I32)
        for g in range(NSA_KV):
            col = jnp.sum(jnp.where(lane_n == g, imp, 0.0), axis=1, keepdims=True)
            rowv = jnp.sum(jnp.where(ri == ci, col, 0.0), axis=0, keepdims=True)
            beats = (rowv > col) | ((rowv == col) & (ci < ri))
            rank = jnp.sum(beats.astype(F32), axis=1, keepdims=True)
            hit = rank == lane_n.astype(F32)
            idx_ref[0, g:g + 1, :] = jnp.sum(jnp.where(hit, jcol_f, 0.0), axis=0,
                                             keepdims=True).astype(I32)


def sample_cmp_select(page_table, pool, layer, n_pool, w1c, cb, w2, q8, slopes):
    DB, NP = page_table.shape
    GRP = min(PAGES_PER_STEP, NP)
    assert NP % GRP == 0
    NCH = NP * CHUNKS_PER_PAGE
    K2 = CMP_STRIDE * HD
    base = layer * n_pool

    def page_spec(k):
        return pl.BlockSpec((1, ROWS_PER_PAGE, HD),
                            lambda b, p, pt: (base + pt[b * NP + p * GRP + k], 0, 0))

    grid_spec = pltpu.PrefetchScalarGridSpec(
        num_scalar_prefetch=1,
        grid=(DB, NP // GRP),
        in_specs=[pl.BlockSpec(memory_space=pltpu.SMEM)] + [page_spec(k) for k in range(GRP)] + [
                  pl.BlockSpec((2, K2, 2 * HD), lambda b, p, pt: (0, 0, 0)),
                  pl.BlockSpec((2, 8, HD), lambda b, p, pt: (0, 0, 0)),
                  pl.BlockSpec((2, HD, HD), lambda b, p, pt: (0, 0, 0)),
                  pl.BlockSpec((1, NSA_HEADS, HD), lambda b, p, pt: (b, 0, 0))],
        out_specs=[pl.BlockSpec((1, NSA_HEADS, HD), lambda b, p, pt: (b, 0, 0)),
                   pl.BlockSpec((1, 8, 128), lambda b, p, pt: (b, 0, 0))],
        scratch_shapes=[pltpu.VMEM((2 * NSA_KV, GRP * CHUNKS_PER_PAGE, K2), F32),
                        pltpu.VMEM((2 * NSA_KV, NCH, 2 * HD), F32)],
    )
    return pl.pallas_call(
        functools.partial(_sample_cmp_body, NP=NP, GRP=GRP),
        grid_spec=grid_spec,
        out_shape=[jax.ShapeDtypeStruct((DB, NSA_HEADS, HD), F32),
                   jax.ShapeDtypeStruct((DB, 8, 128), I32)],
        compiler_params=_cp("arbitrary", "arbitrary"),
        name="sample_cmp_select",
    )(page_table.reshape(-1), slopes, *([pool] * GRP), w1c, cb, w2, q8)


def _sample_sel_body(idx_ref, pt_ref, slopes_ref, *refs, NP, WB):
    page_refs = refs[:SEL_TOPK]
    win_ref, q_ref, new_ref, oc_ref, sm_ref, o_ref = refs[SEL_TOPK:]
    b = pl.program_id(0)
    g = pl.program_id(1)
    P = NP * PAGE_SIZE
    n_past = P // SEL_BLOCK
    scale = HD ** -0.5
    q8 = q_ref[0].astype(BF16)
    rowi = lax.broadcasted_iota(I32, (NSA_HEADS, 1), 0)
    slope = jnp.zeros((NSA_HEADS, 1), F32)
    for h in range(NSA_HEADS):
        slope = jnp.where(rowi == h, slopes_ref[h], slope)
    ibase = (b * NSA_KV + g) * SEL_TOPK

    def vec_dot(krow):
        return jnp.sum(q8.astype(F32) * krow.astype(BF16).astype(F32), axis=1, keepdims=True)

    def add_token(carry, s, valid, vrow):
        m, l, acc = carry
        s = jnp.where(valid, s, NEG_INF)
        m_new = jnp.maximum(m, s)
        alpha = jnp.exp(m - m_new)
        p = jnp.where(valid, jnp.exp(s - m_new), 0.0)
        return (m_new, alpha * l + p,
                alpha * acc + p.astype(BF16).astype(F32) * vrow.astype(BF16).astype(F32))

    carry = (jnp.full((NSA_HEADS, 1), NEG_INF, F32), jnp.zeros((NSA_HEADS, 1), F32),
             jnp.zeros((NSA_HEADS, HD), F32))
    n_new = jnp.int32(0)
    lane = lax.broadcasted_iota(I32, (1, PAGE_SIZE), 1)
    for t in range(SEL_TOPK):
        j = idx_ref[ibase + t]
        n_new = n_new + (j == n_past).astype(I32)
        kb = page_refs[t][0, pl.ds(2 * NSA_KV + g, PAGE_SIZE, stride=8), :].astype(BF16)
        vb = page_refs[t][0, pl.ds(3 * NSA_KV + g, PAGE_SIZE, stride=8), :].astype(BF16)
        kpos = (j // 2) * PAGE_SIZE + lane
        s = _dot_nt(q8, kb) * scale - slope * (P - kpos).astype(F32)
        jv = jnp.full((1, PAGE_SIZE), j, I32)
        carry = _online_update(carry, s, ((kpos // SEL_BLOCK) == jv) & (jv < n_past), vb)

    has_new = jnp.full((NSA_HEADS, 1), n_new, I32) > 0
    k_new = new_ref[0, pl.ds(2 * NSA_KV + g, 1), :]
    v_new = new_ref[0, pl.ds(3 * NSA_KV + g, 1), :]
    _, l_s, acc_s = add_token(carry, vec_dot(k_new) * scale, has_new, v_new)
    o_s = acc_s / l_s

    kw = win_ref[0, pl.ds(g, WB, stride=2 * NSA_KV), :].astype(BF16)
    vw = win_ref[0, pl.ds(NSA_KV + g, WB, stride=2 * NSA_KV), :].astype(BF16)
    d = WB - lax.broadcasted_iota(I32, (1, WB), 1)
    s = _dot_nt(q8, kw) * scale - slope * d.astype(F32)
    mask = d < WINDOW
    s = jnp.where(mask, s, NEG_INF)
    m = jnp.max(s, axis=1, keepdims=True)
    p = jnp.where(mask, jnp.exp(s - m), 0.0)
    carry = (m, jnp.sum(p, axis=1, keepdims=True), _dot(p.astype(BF16), vw))
    kw_new = new_ref[0, pl.ds(4 * NSA_KV + g, 1), :]
    vw_new = new_ref[0, pl.ds(5 * NSA_KV + g, 1), :]
    _, l_w, acc_w = add_token(carry, vec_dot(kw_new) * scale, rowi >= 0, vw_new)
    o_w = acc_w / l_w

    smb = jnp.broadcast_to(sm_ref[0], (NSA_HEADS, 128))
    lane_h = lax.broadcasted_iota(I32, (NSA_HEADS, 128), 1)

    def gate(c):
        return _sigmoid(jnp.sum(jnp.where(lane_h == rowi * 3 + c, smb, 0.0), axis=1, keepdims=True))

    o_ref[0, 0] = gate(0) * oc_ref[0] + gate(1) * o_s + gate(2) * o_w


def sample_sel_win(idx, page_table, pool, win, layer, n_pool, q8, newkv, oc, sm, slopes):
    DB, NP = page_table.shape
    WB = win.shape[1] // (2 * NSA_KV)
    n_past = NP * PAGE_SIZE // SEL_BLOCK
    pbase = layer * n_pool
    wbase = layer * DB
    idx_flat = idx[:, :NSA_KV, :SEL_TOPK].reshape(-1)

    def page_spec(t):
        def page_map(b, g, idx_r, pt_r):
            j = jnp.minimum(idx_r[(b * NSA_KV + g) * SEL_TOPK + t], n_past - 1)
            return (pbase + pt_r[b * NP + j // 2], 0, 0)
        return pl.BlockSpec((1, ROWS_PER_PAGE, HD), page_map)

    grid_spec = pltpu.PrefetchScalarGridSpec(
        num_scalar_prefetch=2,
        grid=(DB, NSA_KV),
        in_specs=[pl.BlockSpec(memory_space=pltpu.SMEM)] + [page_spec(t) for t in range(SEL_TOPK)] + [
                  pl.BlockSpec((1, WB * 2 * NSA_KV, HD), lambda b, g, i_, p_: (wbase + b, 0, 0)),
                  pl.BlockSpec((1, NSA_HEADS, HD), lambda b, g, i_, p_: (b, 0, 0)),
                  pl.BlockSpec((1, 6 * NSA_KV, HD), lambda b, g, i_, p_: (b, 0, 0)),
                  pl.BlockSpec((1, NSA_HEADS, HD), lambda b, g, i_, p_: (b, 0, 0)),
                  pl.BlockSpec((1, 1, 128), lambda b, g, i_, p_: (b, 0, 0))],
        out_specs=pl.BlockSpec((1, 1, NSA_HEADS, HD), lambda b, g, i_, p_: (b, g, 0, 0)),
    )
    return pl.pallas_call(
        functools.partial(_sample_sel_body, NP=NP, WB=WB),
        grid_spec=grid_spec,
        out_shape=jax.ShapeDtypeStruct((DB, NSA_KV, NSA_HEADS, HD), F32),
        compiler_params=_cp("arbitrary", "arbitrary"),
        name="sample_sel_win",
    )(idx_flat, page_table.reshape(-1), slopes, *([pool] * SEL_TOPK), win, q8, newkv, oc, sm)


def _mlstm_step_body(bif_ref, gs_ref, ms_ref, qkv_ref, mo_ref, ng_ref, c_ref, n_ref,
                     o_ref, co_ref, no_ref, mo_out_ref):
    b = pl.program_id(0)
    H = M_HEADS
    ri = lax.broadcasted_iota(I32, (HD, HD), 0)
    ci = lax.broadcasted_iota(I32, (HD, HD), 1)
    for h in range(H):
        ig = jnp.full((1, HD), gs_ref[b, h] + bif_ref[0, h], F32)
        fp = jnp.full((1, HD), gs_ref[b, H + h] + bif_ref[1, h], F32)
        m = jnp.full((1, HD), ms_ref[b, h], F32)
        lf = -_softplus(-fp)
        inter = lf + m
        m_t = jnp.maximum(inter, ig)
        w_intra = jnp.exp(ig - m_t)
        w_inter = jnp.exp(inter - m_t)
        q = qkv_ref[0, h:h + 1, :]
        k = qkv_ref[0, H + h:H + h + 1, :] * (HD ** -0.5)
        v = qkv_ref[0, 2 * H + h:2 * H + h + 1, :]
        C = c_ref[0, h]
        n = n_ref[0, h:h + 1, :]
        s = jnp.sum(q * k, axis=1, keepdims=True) * w_intra
        qC = _dot(jnp.broadcast_to(q, (8, HD)).astype(BF16), C.astype(BF16))[0:1]
        num = s * v + w_inter * qC
        den = s + w_inter * jnp.sum(q * n, axis=1, keepdims=True)
        hh = num / jnp.maximum(jnp.abs(den), jnp.exp(-m_t))
        k_col = jnp.sum(jnp.where(ri == ci, jnp.broadcast_to(k, (HD, HD)), 0.0), axis=1, keepdims=True)
        co_ref[0, h] = w_inter * C + (k_col * w_intra) * v
        no_ref[0, h:h + 1, :] = w_inter * n + w_intra * k
        mo_out_ref[0, h:h + 1, :] = m_t
        mu = jnp.mean(hh, axis=1, keepdims=True)
        hc = hh - mu
        var = jnp.mean(hc * hc, axis=1, keepdims=True)
        hn = hc * lax.rsqrt(var + LN_EPS) * ng_ref[h:h + 1, :]
        o_ref[0, h:h + 1, :] = _sigmoid(mo_ref[0, h:h + 1, :]) * hn


def mlstm_step(b_if, gates, m_state, qkv, mo, norm_g, C, n):
    DB = qkv.shape[0]
    H = M_HEADS
    smem = pl.BlockSpec(memory_space=pltpu.SMEM)
    row = pl.BlockSpec((1, H, HD), lambda b: (b, 0, 0))
    cspec = pl.BlockSpec((1, H, HD, HD), lambda b: (b, 0, 0, 0))
    return pl.pallas_call(
        _mlstm_step_body,
        grid=(DB,),
        in_specs=[smem, smem, smem, pl.BlockSpec((1, 3 * H, HD), lambda b: (b, 0, 0)), row,
                  pl.BlockSpec((H, HD), lambda b: (0, 0)), cspec, row],
        out_specs=[row, cspec, row, row],
        out_shape=[jax.ShapeDtypeStruct((DB, H, HD), F32), jax.ShapeDtypeStruct((DB, H, HD, HD), F32),
                   jax.ShapeDtypeStruct((DB, H, HD), F32), jax.ShapeDtypeStruct((DB, H, HD), F32)],
        compiler_params=_cp("arbitrary"),
        name="mlstm_step",
    )(b_if, gates, m_state, qkv, mo, norm_g.reshape(H, HD), C, n)


def _rglru_step_body(rx_ref, rgt_ref, buf_ref, h_ref, cw_ref, vec_ref, wa_ref, wx_ref, o_ref, ho_ref):
    cw = cw_ref[...]
    vec = vec_ref[...]
    xc = vec[0:1] + rx_ref[...] * cw[CONV_W - 1:CONV_W]
    for j in range(CONV_W - 1):
        xc = xc + buf_ref[j] * cw[j:j + 1]
    a, u = _rglru_gates(xc, [wa_ref[0].astype(BF16)], [wx_ref[0].astype(BF16)], vec)
    h = a * h_ref[...] + u
    ho_ref[...] = h
    o_ref[...] = h * _gelu(rgt_ref[...])


def rglru_step(rx, rgt, buf_t, h0, conv_w, vec, w_a, w_x):
    DB = rx.shape[0]
    col = pl.BlockSpec((DB, 128), lambda n: (0, n))
    return pl.pallas_call(
        _rglru_step_body,
        grid=(R_BLOCKS,),
        in_specs=[col, col, pl.BlockSpec((CONV_W - 1, DB, 128), lambda n: (0, 0, n)), col,
                  pl.BlockSpec((CONV_W, 128), lambda n: (0, n)),
                  pl.BlockSpec((4, 128), lambda n: (0, n)),
                  pl.BlockSpec((1, 128, 128), lambda n: (n, 0, 0)),
                  pl.BlockSpec((1, 128, 128), lambda n: (n, 0, 0))],
        out_specs=[col, col],
        out_shape=[jax.ShapeDtypeStruct((DB, R_WIDTH), F32), jax.ShapeDtypeStruct((DB, R_WIDTH), F32)],
        compiler_params=_cp("arbitrary"),
        name="rglru_step",
    )(rx, rgt, buf_t, h0, conv_w, vec, w_a, w_x)


def pack_w_in(w):
    wt = jnp.swapaxes(w, 1, 2)
    pieces = [(8744, 14888), (0, 1024), (2584, 5656), (5672, 6696), (6696, 7720), (7720, 8744),
              (1024, 2560), (2560, 2584), (5656, 5672)]
    out = jnp.zeros((w.shape[0], N_AL, w.shape[1]), BF16)
    dst = 0
    for a, b in pieces:
        out = lax.dynamic_update_slice_in_dim(out, wt[:, a:b].astype(BF16), dst, axis=1)
        dst += b - a
    return out


def prompt_layer(xp, lw, gw, layer, kv_buf, depth):
    B, T, D = xp.shape
    x2 = xp.reshape(B * T, D)
    M = B * T
    z = mm_nt(x2, gw["w_in"], layer, tm=min(IN_TM, M), tn=IN_TN, out_dtype=F32)
    z3 = z.reshape(B, T, N_AL)
    kv_buf = kv_rows(z, kv_buf, layer, depth)
    kc, vc, cb = compress_from_z(z3, lw["phi_w1"], lw["pe"], lw["phi_b1"], lw["phi_w2"])
    o_a = nsa_prompt(z3, kc, vc, lw["slopes"])
    o_b, Cp, np_, mp = mlstm_prompt(z3, lw["b_if"], lw["norm_g"])
    o_c, hp = rglru_prompt(z3, lw["conv_w"], lw["rg_vec"], lw["w_a"], lw["w_x"])
    merged = merge(o_a.reshape(B * T, -1), o_b.reshape(B * T, -1), o_c.reshape(B * T, -1), z,
                   gw["w_branch"], layer, tm=min(MERGE_TM, M), tn=MERGE_TN, out_dtype=BF16)
    h = mm_res_ln(merged, gw["w_out"], layer, x2, lw["ln_g"][0:1], lw["ln_b"][0:1], tm=min(OUT_TM, M), tk=D)
    f1 = mm(h, gw["mlp_w1"], layer, tm=min(UP_TM, M), tn=UP_TN, out_dtype=BF16, act="relu2")
    x_new = mm_res_ln(f1, gw["mlp_w2"], layer, h, lw["ln_g"][1:2], lw["ln_b"][1:2], tm=min(DOWN_TM, M),
                      tk=DOWN_TK)
    n_win = min(WINDOW, T)
    wkv = z3[:, T - n_win:, OFF_KV + 4 * NSA_KV * HD:OFF_KV + 6 * NSA_KV * HD]
    states = (wkv.reshape(B, n_win, 2, NSA_KV, HD), Cp, np_[:, :, 0], mp[:, :, 0, 0], hp[:, 0],
              z3[:, T - (CONV_W - 1):, OFF_RX:OFF_RX + R_WIDTH])
    return x_new.reshape(B, T, D), states, cb, kv_buf


def sample_layer(xs, lw, gw, cb, layer, page_table, pool, n_pool, win, win_l, C0, n0, m0, h0, conv0):
    DB, D = xs.shape
    z = mm_nt(xs, gw["w_in"], layer, tm=DB, tn=IN_TN, out_dtype=F32)
    q8 = z[:, OFF_Q:OFF_Q + NSA_WIDTH].reshape(DB, NSA_HEADS, HD)
    newkv = z[:, OFF_KV:OFF_KV + 6 * NSA_KV * HD].reshape(DB, 6 * NSA_KV, HD)
    sm = z[:, OFF_SM:OFF_SM + 128].reshape(DB, 1, 128)
    oc, idx = sample_cmp_select(page_table, pool, layer, n_pool, lw["w1c"], cb, lw["phi_w2"], q8,
                                lw["slopes"])
    osw = sample_sel_win(idx, page_table, pool, win, layer, n_pool, q8, newkv, oc, sm, lw["slopes"])
    o_a = jnp.concatenate([osw[:, g, g * NSA_GROUP:(g + 1) * NSA_GROUP] for g in range(NSA_KV)],
                          axis=1).reshape(DB, NSA_WIDTH)
    qkv = z[:, OFF_MQKV:OFF_MQKV + 3 * M_WIDTH].reshape(DB, 3 * M_HEADS, HD)
    mo = z[:, OFF_MO:OFF_MO + M_WIDTH].reshape(DB, M_HEADS, HD)
    gates = z[:, OFF_SM + SM_I:OFF_SM + SM_I + 2 * M_HEADS]
    o_b, Cs, ns, ms = mlstm_step(lw["b_if"], gates, m0, qkv, mo, lw["norm_g"], C0, n0)
    rx = z[:, OFF_RX:OFF_RX + R_WIDTH]
    o_c, hs = rglru_step(rx, z[:, OFF_RGT:OFF_RGT + R_WIDTH], conv0.transpose(1, 0, 2), h0,
                         lw["conv_w"], lw["rg_vec"], lw["w_a"], lw["w_x"])
    merged = merge(o_a, o_b.reshape(DB, M_WIDTH), o_c, z, gw["w_branch"], layer, tm=DB, tn=MERGE_TN,
                   out_dtype=F32)
    h = mm_res_ln(merged, gw["w_out"], layer, xs, lw["ln_g"][0:1], lw["ln_b"][0:1], tm=DB, tk=D)
    f1 = mm(h, gw["mlp_w1"], layer, tm=DB, tn=UP_TN, out_dtype=F32, act="relu2")
    x_new = mm_res_ln(f1, gw["mlp_w2"], layer, h, lw["ln_g"][1:2], lw["ln_b"][1:2], tm=DB, tk=DOWN_TK)
    kvn = newkv.reshape(DB, 1, 6, NSA_KV, HD)
    Wb = win_l.shape[1]
    keep = min(WINDOW, Wb + 1)
    win_new = jnp.concatenate([win_l, kvn[:, :, 4:6]], axis=1)[:, Wb + 1 - keep:]
    conv_new = jnp.concatenate([conv0[:, 1:], rx[:, None]], axis=1)
    states = (kvn[:, :, :4], win_new, Cs, ns, ms[:, :, 0], hs, conv_new)
    return x_new, states


def kernel(x_prompt, x_sample, cache_nsa_kv, cache_win_kv, state_mlstm_C, state_mlstm_n, state_mlstm_m,
           state_rglru_h, state_rglru_conv, page_table, w_in, nsa_pe, nsa_phi_w1, nsa_phi_b1, nsa_phi_w2,
           mlstm_b_if, mlstm_norm_g, rg_conv_w, rg_conv_b, rg_w_a, rg_b_a, rg_w_x, rg_b_x, rg_lambda,
           w_branch, w_out, ln_g, ln_b, mlp_w1, mlp_w2):
    DB, Tn, D = x_sample.shape
    assert Tn == 1 and D == D_MODEL
    depth, n_pool = cache_nsa_kv.shape[:2]
    pool = cache_nsa_kv.reshape(depth * n_pool, ROWS_PER_PAGE, HD)
    Wb = cache_win_kv.shape[2]
    win = cache_win_kv.reshape(depth * DB, Wb * 2 * NSA_KV, HD)
    slopes = alibi_slopes()
    half = CMP_STRIDE * HD
    B, T = x_prompt.shape[:2]
    xp = x_prompt
    xs = x_sample.reshape(DB, D)
    gw = {"w_in": pack_w_in(w_in), "w_branch": w_branch.astype(BF16), "w_out": w_out.astype(BF16),
          "mlp_w1": mlp_w1.astype(BF16), "mlp_w2": mlp_w2.astype(BF16)}
    new_p = [[] for _ in range(6)]
    new_s = [[] for _ in range(7)]
    kv_buf = None
    for l in range(depth):
        lw = {
            "pe": nsa_pe[l], "phi_w1": nsa_phi_w1[l], "phi_b1": nsa_phi_b1[l], "phi_w2": nsa_phi_w2[l],
            "w1c": jnp.concatenate([nsa_phi_w1[l][:, :half], nsa_phi_w1[l][:, half:]], axis=2).astype(BF16),
            "slopes": slopes, "b_if": mlstm_b_if[l], "norm_g": mlstm_norm_g[l],
            "conv_w": rg_conv_w[l],
            "rg_vec": jnp.stack([rg_conv_b[l], rg_b_a[l], rg_b_x[l], rg_lambda[l]]),
            "w_a": rg_w_a[l], "w_x": rg_w_x[l],
            "ln_g": ln_g[l], "ln_b": ln_b[l],
        }
        xp, st_p, cb, kv_buf = prompt_layer(xp, lw, gw, l, kv_buf, depth)
        xs, st_s = sample_layer(xs, lw, gw, cb, l, page_table, pool, n_pool, win, cache_win_kv[l],
                                state_mlstm_C[l], state_mlstm_n[l], state_mlstm_m[l],
                                state_rglru_h[l], state_rglru_conv[l])
        for lst, val in zip(new_p, st_p):
            lst.append(val)
        for lst, val in zip(new_s, st_s):
            lst.append(val)
    P = [jnp.stack(a) for a in new_p]
    S = [jnp.stack(a) for a in new_s]
    nsa_kv_p = kv_buf.reshape(depth, B, T, 4, NSA_KV, HD)
    return (xp, xs.reshape(DB, Tn, D), nsa_kv_p, S[0], P[0], S[1], P[1], S[2], P[2], S[3], P[3], S[4],
            P[4], S[5], P[5], S[6])
```

```python
import functools

import jax
import jax.numpy as jnp
from jax import lax
from jax.experimental import pallas as pl
from jax.experimental.pallas import tpu as pltpu

F32 = jnp.float32
BF16 = jnp.bfloat16
I32 = jnp.int32

D_MODEL = 2048
DEPTH = 4
PAGE_SIZE = 128
HD = 128
NSA_HEADS = 8
NSA_KV = 2
NSA_GROUP = 4
NSA_WIDTH = NSA_HEADS * HD
CMP_BLOCK = 32
CMP_STRIDE = 16
SEL_BLOCK = 64
SEL_TOPK = 16
WINDOW = 512
Q_BLOCK = 128
FORCE_BONUS = 1.0e4
M_HEADS = 8
M_WIDTH = M_HEADS * HD
R_WIDTH = 1024
R_BLOCKS = 8
CONV_W = 4
LRU_C = 8.0
D_FF = 4 * D_MODEL
DEEPNORM_ALPHA = (2 * DEPTH) ** 0.25
LN_EPS = 1e-5
NEG_INF = -1e30

OFF_MG = 0
OFF_Q = 6144
OFF_MQKV = 7168
OFF_MO = 10240
OFF_RX = 11264
OFF_RGT = 12288
OFF_KV = 13312
OFF_SM = 14848
N_AL = 15360
SM_I = 24
SM_F = 32

IN_TM, IN_TN = 1024, 1536
MERGE_TM, MERGE_TN = 512, 2048
OUT_TM = 512
UP_TM, UP_TN = 1024, 1024
DOWN_TM, DOWN_TK = 512, 2048

VMEM_LIMIT = 52 * 1024 * 1024


def _cp(*sem):
    return pltpu.CompilerParams(dimension_semantics=sem, vmem_limit_bytes=VMEM_LIMIT)


def _dot(a, b):
    return jnp.dot(a, b, preferred_element_type=F32)


def _dot_nt(a, b):
    return lax.dot_general(a, b, (((1,), (1,)), ((), ())), preferred_element_type=F32)


def _gelu(x):
    return 0.5 * x * (1.0 + jnp.tanh(0.7978845608028654 * (x + 0.044715 * (x * x * x))))


def _sigmoid(x):
    return 0.5 * jnp.tanh(0.5 * x) + 0.5


def _softplus(x):
    return jnp.maximum(x, 0.0) + jnp.log1p(jnp.exp(-jnp.abs(x)))


def _mm_body(a_ref, b_ref, o_ref, *, act):
    acc = _dot(a_ref[...].astype(BF16), b_ref[...])
    if act == "relu2":
        acc = jnp.square(jnp.maximum(acc, 0.0))
    o_ref[...] = acc.astype(o_ref.dtype)


def _mm_nt_body(a_ref, bt_ref, o_ref):
    o_ref[...] = _dot_nt(a_ref[...].astype(BF16), bt_ref[...]).astype(o_ref.dtype)


def mm_nt(a, bt, layer, *, tm, tn, out_dtype):
    M, K = a.shape
    N = bt.shape[1]
    assert M % tm == 0 and N % tn == 0
    return pl.pallas_call(
        _mm_nt_body,
        grid=(M // tm, N // tn),
        in_specs=[pl.BlockSpec((tm, K), lambda i, j: (i, 0)),
                  pl.BlockSpec((None, tn, K), lambda i, j: (layer, j, 0))],
        out_specs=pl.BlockSpec((tm, tn), lambda i, j: (i, j)),
        out_shape=jax.ShapeDtypeStruct((M, N), out_dtype),
        compiler_params=_cp("parallel", "arbitrary"),
        name="mm_nt",
    )(a, bt)


def mm(a, b, layer, *, tm, tn, out_dtype, act=None):
    M, K = a.shape
    N = b.shape[2]
    assert M % tm == 0 and N % tn == 0
    return pl.pallas_call(
        functools.partial(_mm_body, act=act),
        grid=(M // tm, N // tn),
        in_specs=[pl.BlockSpec((tm, K), lambda i, j: (i, 0)),
                  pl.BlockSpec((None, K, tn), lambda i, j: (layer, 0, j))],
        out_specs=pl.BlockSpec((tm, tn), lambda i, j: (i, j)),
        out_shape=jax.ShapeDtypeStruct((M, N), out_dtype),
        compiler_params=_cp("parallel", "arbitrary"),
        name="mm",
    )(a, b)


def _mm_ln_body(a_ref, b_ref, x_ref, g_ref, bb_ref, o_ref, acc_ref, *, nk):
    k = pl.program_id(1)

    if nk > 1:
        @pl.when(k == 0)
        def _():
            acc_ref[...] = _dot(a_ref[...].astype(BF16), b_ref[...])

        @pl.when((k > 0) & (k < nk - 1))
        def _():
            acc_ref[...] += _dot(a_ref[...].astype(BF16), b_ref[...])

    @pl.when(k == nk - 1)
    def _():
        y = DEEPNORM_ALPHA * x_ref[...] + _dot(a_ref[...].astype(BF16), b_ref[...])
        if nk > 1:
            y = y + acc_ref[...]
        mu = jnp.mean(y, axis=-1, keepdims=True)
        yc = y - mu
        var = jnp.mean(yc * yc, axis=-1, keepdims=True)
        o_ref[...] = yc * lax.rsqrt(var + LN_EPS) * g_ref[...] + bb_ref[...]


def mm_res_ln(a, b, layer, x, g, bb, *, tm, tk):
    M, K = a.shape
    N = b.shape[2]
    nk = K // tk
    assert M % tm == 0 and K % tk == 0
    return pl.pallas_call(
        functools.partial(_mm_ln_body, nk=nk),
        grid=(M // tm, nk),
        in_specs=[pl.BlockSpec((tm, tk), lambda i, k: (i, k)),
                  pl.BlockSpec((None, tk, N), lambda i, k: (layer, k, 0)),
                  pl.BlockSpec((tm, N), lambda i, k: (i, 0)),
                  pl.BlockSpec((1, N), lambda i, k: (0, 0)),
                  pl.BlockSpec((1, N), lambda i, k: (0, 0))],
        out_specs=pl.BlockSpec((tm, N), lambda i, k: (i, 0)),
        out_shape=jax.ShapeDtypeStruct((M, N), F32),
        scratch_shapes=[pltpu.VMEM((tm, N), F32)],
        compiler_params=_cp("parallel", "arbitrary"),
        name="mm_res_ln",
    )(a, b, x, g, bb)


def _merge_body(oa_ref, ob_ref, oc_ref, g0_ref, g1_ref, g2_ref, w_ref, o_ref):
    acc = _sigmoid(g0_ref[...]) * _dot(oa_ref[...].astype(BF16), w_ref[0])
    acc += _sigmoid(g1_ref[...]) * _dot(ob_ref[...].astype(BF16), w_ref[1])
    acc += _sigmoid(g2_ref[...]) * _dot(oc_ref[...].astype(BF16), w_ref[2])
    o_ref[...] = acc.astype(o_ref.dtype)


def merge(o_a, o_b, o_c, z, w_branch, layer, *, tm, tn, out_dtype):
    M, W = o_a.shape
    N = w_branch.shape[3]
    assert OFF_MG % tn == 0 and N % tn == 0 and M % tm == 0
    gb = OFF_MG // tn
    nb = N // tn
    br = pl.BlockSpec((tm, W), lambda i, j: (i, 0))

    def gspec(k):
        return pl.BlockSpec((tm, tn), lambda i, j: (i, gb + k * nb + j))

    return pl.pallas_call(
        _merge_body,
        grid=(M // tm, nb),
        in_specs=[br, br, br, gspec(0), gspec(1), gspec(2),
                  pl.BlockSpec((None, 3, W, tn), lambda i, j: (layer, 0, 0, j),
                               pipeline_mode=pl.Buffered(1) if nb == 1 else None)],
        out_specs=pl.BlockSpec((tm, tn), lambda i, j: (i, j)),
        out_shape=jax.ShapeDtypeStruct((M, N), out_dtype),
        compiler_params=_cp("parallel", "arbitrary"),
        name="merge",
    )(o_a, o_b, o_c, z, z, z, w_branch)


def _kv_rows_body(z_ref, *rest):
    o_ref = rest[-1]
    tm = z_ref.shape[0]
    for kg in range(4 * NSA_KV):
        o_ref[pl.ds(kg, tm, stride=4 * NSA_KV), :] = z_ref[:, kg * HD:(kg + 1) * HD]


def kv_rows(z, buf, layer, depth):
    M = z.shape[0]
    tm = min(512, M)
    nb = M // tm
    W = 4 * NSA_KV * HD
    assert OFF_KV % W == 0 and M % tm == 0
    in_specs = [pl.BlockSpec((tm, W), lambda i: (i, OFF_KV // W))]
    args = [z]
    aliases = {}
    if buf is not None:
        in_specs.append(pl.BlockSpec(memory_space=pl.ANY))
        args.append(buf)
        aliases = {1: 0}
    return pl.pallas_call(
        _kv_rows_body,
        grid=(nb,),
        in_specs=in_specs,
        out_specs=pl.BlockSpec((tm * 4 * NSA_KV, HD), lambda i: (layer * nb + i, 0)),
        out_shape=jax.ShapeDtypeStruct((depth * M * 4 * NSA_KV, HD), F32),
        input_output_aliases=aliases,
        compiler_params=_cp("arbitrary"),
        name="kv_rows",
    )(*args)


def _compress_body(z_ref, w1_ref, pe_ref, b1_ref, w2_ref, o_ref, cb_ref, *, n):
    half = CMP_STRIDE * HD
    x = jnp.concatenate([z_ref[0, pl.ds(c, n, stride=CMP_STRIDE), :] for c in range(CMP_STRIDE)],
                        axis=1).astype(BF16)
    w1 = w1_ref[0].astype(BF16)
    lead = _dot(x, w1[:half])
    trail = _dot(x, w1[half:])
    pe8 = jnp.broadcast_to(pe_ref[0], (8, 2 * half)).astype(BF16)
    cb = _dot(pe8, w1) + b1_ref[0]
    trail_next = pltpu.roll(trail, n - 1, axis=0)
    hid = _gelu(lead + trail_next + cb[0:1])
    o_ref[0, 0] = _dot(hid.astype(BF16), w2_ref[0].astype(BF16))
    cb_ref[0] = cb


def compress_from_z(z3, w1, pe, b1, w2):
    B, T, _ = z3.shape
    G = NSA_KV
    S = B * G
    n = T // CMP_STRIDE
    K = CMP_STRIDE * HD
    kvb = OFF_KV // HD
    out, cb = pl.pallas_call(
        functools.partial(_compress_body, n=n),
        grid=(2, S),
        in_specs=[pl.BlockSpec((1, T, HD), lambda k, s: (s // G, 0, kvb + k * G + s % G)),
                  pl.BlockSpec((1, 2 * K, HD), lambda k, s: (k, 0, 0)),
                  pl.BlockSpec((1, 1, 2 * K), lambda k, s: (k, 0, 0)),
                  pl.BlockSpec((1, 1, HD), lambda k, s: (k, 0, 0)),
                  pl.BlockSpec((1, HD, HD), lambda k, s: (k, 0, 0))],
        out_specs=[pl.BlockSpec((1, 1, n, HD), lambda k, s: (k, s, 0, 0)),
                   pl.BlockSpec((1, 8, HD), lambda k, s: (k, 0, 0))],
        out_shape=[jax.ShapeDtypeStruct((2, S, n, HD), F32),
                   jax.ShapeDtypeStruct((2, 8, HD), F32)],
        compiler_params=_cp("arbitrary", "arbitrary"),
        name="compress_prompt",
    )(z3, w1, pe.reshape(2, 1, CMP_BLOCK * HD), b1.reshape(2, 1, HD), w2)
    return out[0], out[1], cb


def alibi_slopes():
    return jnp.asarray([2.0 ** (-8.0 * (h + 1) / NSA_HEADS) for h in range(NSA_HEADS)], F32)


SEL_CHUNK = 512


def _online_update(carry, s, mask, v):
    m, l, acc = carry
    s = jnp.where(mask, s, NEG_INF)
    m_new = jnp.maximum(m, jnp.max(s, axis=1, keepdims=True))
    alpha = jnp.exp(m - m_new)
    p = jnp.where(mask, jnp.exp(s - m_new), 0.0)
    l = alpha * l + jnp.sum(p, axis=1, keepdims=True)
    acc = alpha * acc + _dot(p.astype(BF16), v)
    return m_new, l, acc


MASK_BIG = 1.0e30
LOG2E = 1.4426950408889634
WIN_KEYS = WINDOW + Q_BLOCK


def _nsa_prompt_body(slopes_ref, q_ref, ks_ref, vs_ref, kw_ref, vw_ref, kc_ref, vc_ref,
                     sm_ref, o_ref, ks_bf, vs_aug, et_bf, kw_pad, vw_aug, wbias, sbias, y_a, y_b,
                     clist, *, T):
    assert T % (2 * SEL_CHUNK) == 0
    g = pl.program_id(1)
    i = pl.program_id(2)
    QB = Q_BLOCK
    R = NSA_GROUP
    RQ = R * QB
    NCP = T // CMP_STRIDE
    NC = NCP - 1
    NS = T // SEL_BLOCK
    NSP = 128
    assert NS <= NSP
    t0 = i * QB
    scale = HD ** -0.5
    c2 = scale * LOG2E

    @pl.when(i == 0)
    def _():
        lane = lax.broadcasted_iota(I32, (T, HD), 1)
        ones_col = jnp.where(lane == 0, 1.0, 0.0).astype(BF16)
        ks_bf[...] = ks_ref[0].astype(BF16)
        vs_aug[:, 0:HD] = vs_ref[0].astype(BF16)
        vs_aug[:, HD:2 * HD] = ones_col
        kk = lax.broadcasted_iota(I32, (T, HD), 0)
        et_bf[...] = jnp.where(kk // SEL_BLOCK == lane, MASK_BIG, 0.0).astype(BF16)
        kw_pad[0:WINDOW, :] = jnp.zeros((WINDOW, HD), BF16)
        kw_pad[WINDOW:WINDOW + T, :] = kw_ref[0].astype(BF16)
        vw_aug[0:WINDOW, :] = jnp.zeros((WINDOW, 2 * HD), BF16)
        vw_aug[WINDOW:WINDOW + T, 0:HD] = vw_ref[0].astype(BF16)
        vw_aug[WINDOW:WINDOW + T, HD:2 * HD] = ones_col
        wd = (lax.broadcasted_iota(I32, (QB, WIN_KEYS), 0) + WINDOW
              - lax.broadcasted_iota(I32, (QB, WIN_KEYS), 1))
        band = (wd >= 0) & (wd < WINDOW)
        wdf = wd.astype(F32)
        lf = lax.broadcasted_iota(I32, (QB, SEL_CHUNK), 1).astype(F32)
        for r in range(R):
            sl = slopes_ref[g * R + r]
            wbias[r * QB:(r + 1) * QB, :] = jnp.where(band, (-LOG2E * sl) * wdf, -MASK_BIG)
            sbias[r * QB:(r + 1) * QB, :] = (LOG2E * sl) * lf

    qb = q_ref[0]
    q4 = jnp.concatenate([qb[:, r * HD:(r + 1) * HD] for r in range(R)], axis=0).astype(BF16)
    row = lax.broadcasted_iota(I32, (RQ, 1), 0)
    qpos = t0 + (row & (QB - 1))
    slope = jnp.concatenate(
        [jnp.full((QB, 1), slopes_ref[g * R + r], F32) for r in range(R)], axis=0)

    kc = kc_ref[0, 0].astype(BF16)
    vc = vc_ref[0, 0].astype(BF16)
    n_idx = lax.broadcasted_iota(I32, (1, NCP), 1)
    dist = qpos - (n_idx * CMP_STRIDE + (CMP_BLOCK - 1))
    mask = (dist >= 0) & (n_idx < NC)
    s = _dot_nt(q4, kc) * scale - slope * dist.astype(F32)
    s = jnp.where(mask, s, NEG_INF)
    m = jnp.max(s, axis=1, keepdims=True)
    p = jnp.where(mask, jnp.exp(s - m), 0.0)
    l = jnp.sum(p, axis=1, keepdims=True)
    p = p * (1.0 / jnp.where(l > 0.0, l, 1.0))
    o_c = _dot(p.astype(BF16), vc)

    psum = p[0:QB]
    for r in range(1, R):
        psum = psum + p[r * QB:(r + 1) * QB]
    p_hi = psum.astype(BF16)
    p_lo = (psum - p_hi.astype(F32)).astype(BF16)
    jo = lax.broadcasted_iota(I32, (NS, NCP), 0) * SEL_BLOCK
    no = lax.broadcasted_iota(I32, (NS, NCP), 1)
    ov = ((no * CMP_STRIDE < jo + SEL_BLOCK) & (no * CMP_STRIDE + CMP_BLOCK > jo)
          & (no < NC)).astype(BF16)
    imp = _dot_nt(ov, p_hi) + _dot_nt(ov, p_lo)
    jj = lax.broadcasted_iota(I32, (NS, QB), 0)
    qp = t0 + lax.broadcasted_iota(I32, (NS, QB), 1)
    cur = qp // SEL_BLOCK
    forced = (jj == 0) | (jj == cur) | (jj == cur - 1)
    imp = jnp.where(forced, imp + FORCE_BONUS, imp)
    imp = jnp.where(jj * SEL_BLOCK <= qp, imp, -1.0)
    ranks = [jnp.zeros((8, QB), F32) for _ in range(NS // 8)]
    for j2 in range(NS):
        rv = imp[j2:j2 + 1, :]
        for v in range(NS // 8):
            blk = imp[8 * v:8 * v + 8]
            if 8 * v > j2:
                beats = rv >= blk
            elif 8 * v + 7 < j2:
                beats = rv > blk
            else:
                beats = (rv > blk) | ((rv == blk) & (jj[8 * v:8 * v + 8] > j2))
            ranks[v] = ranks[v] + jnp.where(beats, 1.0, 0.0)
    rank = jnp.concatenate(ranks, axis=0)
    selm_t = jnp.where(rank < SEL_TOPK, 0.0, -1.0)
    if NSP > NS:
        selm_t = jnp.concatenate([selm_t, jnp.zeros((NSP - NS, QB), F32)], axis=0)
    selm = selm_t.T.astype(BF16)

    sl_col = LOG2E * slope
    rel = (lax.broadcasted_iota(I32, (QB, SEL_CHUNK), 1)
           - lax.broadcasted_iota(I32, (QB, SEL_CHUNK), 0))

    def scores(c):
        k0 = pl.multiple_of(c * SEL_CHUNK, SEL_CHUNK)
        mb = _dot_nt(selm, et_bf[pl.ds(k0, SEL_CHUNK), :])
        mb = jnp.where(rel <= t0 - k0, mb, -MASK_BIG)
        return (_dot_nt(q4, ks_bf[pl.ds(k0, SEL_CHUNK), :]) * c2 + sbias[...]
                + jnp.concatenate([mb] * R, axis=0))

    def consume(c, y, carry):
        m, acc = carry
        k0 = pl.multiple_of(c * SEL_CHUNK, SEL_CHUNK)
        off = sl_col * (k0 - t0).astype(F32)
        m_new = jnp.maximum(m, jnp.max(y, axis=1, keepdims=True) + off)
        p = jnp.exp2(y - (m_new - off))
        acc = jnp.exp2(m - m_new) * acc + _dot(p.astype(BF16), vs_aug[pl.ds(k0, SEL_CHUNK), :])
        return m_new, acc

    n_chunks = (t0 + QB + SEL_CHUNK - 1) // SEL_CHUNK
    bpc = SEL_CHUNK // SEL_BLOCK
    n_act = jnp.int32(0)
    dummy = jnp.int32(T // SEL_CHUNK - 1)
    for c in reversed(range(T // SEL_CHUNK)):
        act = (jnp.sum(selm_t[c * bpc:(c + 1) * bpc] + 1.0) > 0.0) & (c < n_chunks)
        dummy = jnp.where(act, dummy, c)
    for c in range(T // SEL_CHUNK):
        act = (jnp.sum(selm_t[c * bpc:(c + 1) * bpc] + 1.0) > 0.0) & (c < n_chunks)
        clist[n_act] = c
        n_act = n_act + act.astype(I32)
    clist[n_act] = dummy

    def pair(pi, carry):
        k = 2 * pi
        y_b[...] = scores(clist[k + 1])
        carry = consume(clist[k], y_a[...], carry)
        y_a[...] = scores(clist[jnp.minimum(k + 2, n_act)])
        return consume(clist[k + 1], y_b[...], carry)

    init = (jnp.full((RQ, 1), -MASK_BIG, F32), jnp.zeros((RQ, 2 * HD), F32))
    y_a[...] = scores(clist[0])
    _, acc_s = lax.fori_loop(0, (n_act + 1) // 2, pair, init)
    o_s = acc_s[:, 0:HD] * (1.0 / acc_s[:, HD:HD + 1])

    w0 = pl.multiple_of(t0, QB)
    vrow = jnp.where(lax.broadcasted_iota(I32, (1, WIN_KEYS), 1) >= WINDOW - t0, 0.0, -MASK_BIG)
    y = _dot_nt(q4, kw_pad[pl.ds(w0, WIN_KEYS), :]) * c2 + wbias[...] + vrow
    p = jnp.exp2(y - jnp.max(y, axis=1, keepdims=True))
    acc_w = _dot(p.astype(BF16), vw_aug[pl.ds(w0, WIN_KEYS), :])
    o_w = acc_w[:, 0:HD] * (1.0 / acc_w[:, HD:HD + 1])

    smb = sm_ref[0]
    lane = lax.broadcasted_iota(I32, (QB, 128), 1)

    def gate(c):
        cols = []
        for r in range(R):
            idx = (g * R + r) * 3 + c
            cols.append(jnp.sum(jnp.where(lane == idx, smb, 0.0), axis=1, keepdims=True))
        return _sigmoid(jnp.concatenate(cols, axis=0))

    o = gate(0) * o_c + gate(1) * o_s + gate(2) * o_w
    o_ref[0] = jnp.concatenate([o[r * QB:(r + 1) * QB] for r in range(R)], axis=1).astype(o_ref.dtype)


def nsa_prompt(z3, kc, vc, slopes):
    B, T, _ = z3.shape
    G = NSA_KV
    kvb = OFF_KV // HD

    def kvspec(kind):
        return pl.BlockSpec((1, T, HD), lambda b, g, i: (b, 0, kvb + kind * G + g))

    cspec = pl.BlockSpec((1, 1, T // CMP_STRIDE, HD), lambda b, g, i: (b * G + g, 0, 0, 0))
    GW = NSA_GROUP * HD
    assert OFF_Q % GW == 0
    qspec = pl.BlockSpec((1, Q_BLOCK, GW), lambda b, g, i: (b, i, OFF_Q // GW + g))
    return pl.pallas_call(
        functools.partial(_nsa_prompt_body, T=T),
        grid=(B, G, T // Q_BLOCK),
        in_specs=[pl.BlockSpec(memory_space=pltpu.SMEM),
                  qspec, kvspec(2), kvspec(3), kvspec(4), kvspec(5), cspec, cspec,
                  pl.BlockSpec((1, Q_BLOCK, 128), lambda b, g, i: (b, i, OFF_SM // 128))],
        out_specs=pl.BlockSpec((1, Q_BLOCK, GW), lambda b, g, i: (b, i, g)),
        out_shape=jax.ShapeDtypeStruct((B, T, NSA_WIDTH), BF16),
        scratch_shapes=[pltpu.VMEM((T, HD), BF16),
                        pltpu.VMEM((T, 2 * HD), BF16),
                        pltpu.VMEM((T, HD), BF16),
                        pltpu.VMEM((T + WINDOW, HD), BF16),
                        pltpu.VMEM((T + WINDOW, 2 * HD), BF16),
                        pltpu.VMEM((NSA_GROUP * Q_BLOCK, WIN_KEYS), F32),
                        pltpu.VMEM((NSA_GROUP * Q_BLOCK, SEL_CHUNK), F32),
                        pltpu.VMEM((NSA_GROUP * Q_BLOCK, SEL_CHUNK), F32),
                        pltpu.VMEM((NSA_GROUP * Q_BLOCK, SEL_CHUNK), F32),
                        pltpu.SMEM((T // SEL_CHUNK + 8,), I32)],
        compiler_params=_cp("arbitrary", "arbitrary", "arbitrary"),
        name="nsa_prompt",
    )(slopes, z3, z3, z3, z3, z3, kc[:, None], vc[:, None], z3)


M_CHUNK_K = 128


M_HEADS_PER_STEP = 4
M_ROWS_PER_STEP = 512


def _mlstm_body(bif_ref, q_ref, k_ref, v_ref, sm_ref, mo_ref, ng_ref,
                o_ref, c_ref, n_ref, m_ref, c_sc, n_sc, m_sc, *, NT, TB):
    hb = pl.program_id(1)
    t = pl.program_id(2)
    L = M_CHUNK_K
    HB = M_HEADS_PER_STEP
    lane = lax.broadcasted_iota(I32, (L, 128), 1)
    ri = lax.broadcasted_iota(I32, (L, L), 0)
    ci = lax.broadcasted_iota(I32, (L, L), 1)
    eye = ri == ci
    tril = ri >= ci

    @pl.when(t == 0)
    def _():
        c_sc[...] = jnp.zeros_like(c_sc)
        n_sc[...] = jnp.zeros_like(n_sc)
        m_sc[...] = jnp.zeros_like(m_sc)

    def head_chunk(hh, r0):
        h = hb * HB + hh
        cs = slice(hh * HD, (hh + 1) * HD)
        bi = bif_ref[0, h]
        bf = bif_ref[1, h]
        ng = ng_ref[:, cs]
        C = c_sc[hh]
        n = n_sc[hh]
        m = m_sc[hh][:, 0:1]
        q = q_ref[0, pl.ds(r0, L), cs]
        k = k_ref[0, pl.ds(r0, L), cs] * (HD ** -0.5)
        v = v_ref[0, pl.ds(r0, L), cs]
        smb = sm_ref[0, pl.ds(r0, L), :]
        ig_col = jnp.sum(jnp.where(lane == SM_I + h, smb, 0.0), axis=1, keepdims=True) + bi
        fp_col = jnp.sum(jnp.where(lane == SM_F + h, smb, 0.0), axis=1, keepdims=True) + bf
        lf_col = -_softplus(-fp_col)
        lf_row = jnp.sum(jnp.where(eye, lf_col, 0.0), axis=0, keepdims=True)
        ig_row = jnp.sum(jnp.where(eye, ig_col, 0.0), axis=0, keepdims=True)
        b_col = jnp.sum(jnp.where(tril, lf_row, 0.0), axis=1, keepdims=True)
        b_row = jnp.sum(jnp.where(ri <= ci, lf_col, 0.0), axis=0, keepdims=True)
        d = jnp.where(tril, b_col - b_row + ig_row, NEG_INF)
        inter = b_col + m
        m_t = jnp.maximum(inter, jnp.max(d, axis=1, keepdims=True))
        w_intra = jnp.exp(d - m_t)
        w_inter = jnp.exp(inter - m_t)
        qb = q.astype(BF16)
        vb = v.astype(BF16)
        s = _dot_nt(qb, k.astype(BF16)) * w_intra
        num = _dot(s.astype(BF16), vb) + w_inter * _dot(qb, C.astype(BF16))
        den = jnp.sum(s, axis=1, keepdims=True) + w_inter * jnp.sum(q * n, axis=1, keepdims=True)
        hh_ = num * (1.0 / jnp.maximum(jnp.abs(den), jnp.exp(-m_t)))
        m_new = m_t[L - 1:L, :]
        b_last = b_col[L - 1:L, :]
        decay = jnp.exp(b_last + m - m_new)
        w_s = jnp.exp(b_last - b_col + ig_col - m_new)
        kw = k * w_s
        c_sc[hh] = decay * C + _dot(kw.T.astype(BF16), vb)
        n_sc[hh] = decay * n + jnp.sum(kw, axis=0, keepdims=True)
        m_sc[hh] = jnp.broadcast_to(m_new, (1, 128))
        mu = jnp.mean(hh_, axis=1, keepdims=True)
        hc = hh_ - mu
        var = jnp.mean(hc * hc, axis=1, keepdims=True)
        hn = hc * lax.rsqrt(var + LN_EPS) * ng
        o_ref[0, pl.ds(r0, L), cs] = (_sigmoid(mo_ref[0, pl.ds(r0, L), cs]) * hn).astype(o_ref.dtype)

    def sub(c, carry):
        r0 = pl.multiple_of(c * L, L)
        for hh in range(HB):
            head_chunk(hh, r0)
        return carry

    lax.fori_loop(0, TB // L, sub, 0)

    @pl.when(t == NT - 1)
    def _():
        c_ref[0] = c_sc[...]
        n_ref[0] = n_sc[...]
        m_ref[0] = m_sc[...]


def mlstm_prompt(z3, b_if, norm_g):
    B, T, _ = z3.shape
    H = M_HEADS
    HB = M_HEADS_PER_STEP
    TB = min(M_ROWS_PER_STEP, T)
    W = HB * HD
    assert OFF_MQKV % W == 0 and OFF_MO % W == 0 and T % TB == 0

    def hspec(off):
        return pl.BlockSpec((1, TB, W), lambda b, h, t: (b, t, off // W + h))

    return pl.pallas_call(
        functools.partial(_mlstm_body, NT=T // TB, TB=TB),
        grid=(B, H // HB, T // TB),
        in_specs=[pl.BlockSpec(memory_space=pltpu.SMEM),
                  hspec(OFF_MQKV), hspec(OFF_MQKV + M_WIDTH), hspec(OFF_MQKV + 2 * M_WIDTH),
                  pl.BlockSpec((1, TB, 128), lambda b, h, t: (b, t, OFF_SM // 128)),
                  hspec(OFF_MO),
                  pl.BlockSpec((1, W), lambda b, h, t: (0, h))],
        out_specs=[pl.BlockSpec((1, TB, W), lambda b, h, t: (b, t, h)),
                   pl.BlockSpec((1, HB, HD, HD), lambda b, h, t: (b, h, 0, 0)),
                   pl.BlockSpec((1, HB, 1, HD), lambda b, h, t: (b, h, 0, 0)),
                   pl.BlockSpec((1, HB, 1, 128), lambda b, h, t: (b, h, 0, 0))],
        out_shape=[jax.ShapeDtypeStruct((B, T, M_WIDTH), BF16),
                   jax.ShapeDtypeStruct((B, H, HD, HD), F32),
                   jax.ShapeDtypeStruct((B, H, 1, HD), F32),
                   jax.ShapeDtypeStruct((B, H, 1, 128), F32)],
        scratch_shapes=[pltpu.VMEM((HB, HD, HD), F32), pltpu.VMEM((HB, 1, HD), F32),
                        pltpu.VMEM((HB, 1, 128), F32)],
        compiler_params=_cp("arbitrary", "arbitrary", "arbitrary"),
        name="mlstm_prompt",
    )(b_if, z3, z3, z3, z3, z3, norm_g.reshape(1, M_WIDTH))


R_CHUNK = 128


def _rglru_gates(xc, was, wxs, vec):
    xb = xc.astype(BF16)
    nb = len(was)
    ya = jnp.concatenate([_dot(xb[:, j * 128:(j + 1) * 128], was[j]) for j in range(nb)], axis=1)
    yx = jnp.concatenate([_dot(xb[:, j * 128:(j + 1) * 128], wxs[j]) for j in range(nb)], axis=1)
    r = _sigmoid(ya + vec[1:2])
    i = _sigmoid(yx + vec[2:3])
    log_a = -LRU_C * r * _softplus(-vec[3:4])
    a = jnp.exp(log_a)
    at = -jnp.tanh(log_a)
    u = jnp.sqrt(2.0 * at) * lax.rsqrt(1.0 + at) * (i * xc)
    return a, u


R_BLOCKS_PER_STEP = 2


def _rglru_body(rx_ref, rgt_ref, cw_ref, vec_ref, wa_ref, wx_ref, o_ref, hl_ref, xpad_ref, *, T):
    Tc = R_CHUNK
    W = rx_ref.shape[2]
    xpad_ref[0:8, :] = jnp.zeros((8, W), F32)
    xpad_ref[8:T + 8, :] = rx_ref[0]
    cw = cw_ref[...]
    vec = vec_ref[...]
    was = [wa_ref[j].astype(BF16) for j in range(W // 128)]
    wxs = [wx_ref[j].astype(BF16) for j in range(W // 128)]
    rowmod = lax.broadcasted_iota(I32, (Tc, W), 0) & 7

    def chunk(c, h):
        r0 = pl.multiple_of(c * Tc, Tc)
        win = xpad_ref[pl.ds(r0, Tc + 8), :]
        xc = vec[0:1]
        for j in range(CONV_W):
            off = 8 - (CONV_W - 1) + j
            xc = xc + win[off:off + Tc] * cw[j:j + 1]
        a, u = _rglru_gates(xc, was, wxs, vec)
        for sft in (1, 2, 4):
            a1 = pltpu.roll(a, sft, axis=0)
            u1 = pltpu.roll(u, sft, axis=0)
            ok = rowmod >= sft
            u = jnp.where(ok, a * u1 + u, u)
            a = jnp.where(ok, a * a1, a)
        hs = []
        for gi in range(Tc // 8):
            hg = a[gi * 8:(gi + 1) * 8] * h + u[gi * 8:(gi + 1) * 8]
            hs.append(hg)
            h = hg[7:8]
        hf = jnp.concatenate(hs, axis=0)
        o_ref[0, pl.ds(r0, Tc), :] = (hf * _gelu(rgt_ref[0, pl.ds(r0, Tc), :])).astype(o_ref.dtype)
        return h

    h = lax.fori_loop(0, T // Tc, chunk, jnp.zeros((1, W), F32))
    hl_ref[0] = h


def rglru_prompt(z3, conv_w, vec, w_a, w_x):
    B, T, _ = z3.shape
    RB = R_BLOCKS_PER_STEP
    W = RB * 128
    assert OFF_RX % W == 0 and OFF_RGT % W == 0 and R_BLOCKS % RB == 0
    return pl.pallas_call(
        functools.partial(_rglru_body, T=T),
        grid=(B, R_BLOCKS // RB),
        in_specs=[pl.BlockSpec((1, T, W), lambda b, n: (b, 0, OFF_RX // W + n)),
                  pl.BlockSpec((1, T, W), lambda b, n: (b, 0, OFF_RGT // W + n)),
                  pl.BlockSpec((CONV_W, W), lambda b, n: (0, n)),
                  pl.BlockSpec((4, W), lambda b, n: (0, n)),
                  pl.BlockSpec((RB, 128, 128), lambda b, n: (n, 0, 0)),
                  pl.BlockSpec((RB, 128, 128), lambda b, n: (n, 0, 0))],
        out_specs=[pl.BlockSpec((1, T, W), lambda b, n: (b, 0, n)),
                   pl.BlockSpec((1, 1, W), lambda b, n: (b, 0, n))],
        out_shape=[jax.ShapeDtypeStruct((B, T, R_WIDTH), BF16),
                   jax.ShapeDtypeStruct((B, 1, R_WIDTH), F32)],
        scratch_shapes=[pltpu.VMEM((T + 8, W), F32)],
        compiler_params=_cp("arbitrary", "arbitrary"),
        name="rglru_prompt",
    )(z3, z3, conv_w, vec, w_a, w_x)


ROWS_PER_PAGE = PAGE_SIZE * 4 * NSA_KV
CHUNKS_PER_PAGE = PAGE_SIZE // CMP_STRIDE
PAGES_PER_STEP = 16


def _sample_cmp_body(pt_ref, slopes_ref, *refs, NP, GRP):
    page_refs = refs[:GRP]
    w1c_ref, cb_ref, w2_ref, q_ref, oc_ref, idx_ref, xs_ref, lt_ref = refs[GRP:]
    p = pl.program_id(1)
    P = NP * PAGE_SIZE
    CPG = GRP * CHUNKS_PER_PAGE
    NCH = NP * CHUNKS_PER_PAGE
    n_sel = P // SEL_BLOCK + 1
    NSP = -(-n_sel // 128) * 128
    scale = HD ** -0.5

    for k in range(GRP):
        for kg in range(2 * NSA_KV):
            plane = page_refs[k][0, pl.ds(kg, PAGE_SIZE, stride=4 * NSA_KV), :]
            xs_ref[kg, k * CHUNKS_PER_PAGE:(k + 1) * CHUNKS_PER_PAGE, :] = plane.reshape(
                CHUNKS_PER_PAGE, CMP_STRIDE * HD)

    r0 = pl.multiple_of(p * CPG, CPG)
    for kind in range(2):
        x = jnp.concatenate([xs_ref[kind * NSA_KV + g] for g in range(NSA_KV)], axis=0).astype(BF16)
        lt = _dot(x, w1c_ref[kind])
        for g in range(NSA_KV):
            lt_ref[kind * NSA_KV + g, pl.ds(r0, CPG), :] = lt[g * CPG:(g + 1) * CPG]

    @pl.when(p == NP // GRP - 1)
    def _():
        q128 = jnp.concatenate([q_ref[0], jnp.zeros((128 - NSA_HEADS, HD), F32)], axis=0).astype(BF16)
        lane = lax.broadcasted_iota(I32, (1, 128), 1)
        slope_row = jnp.zeros((1, 128), F32)
        for h in range(NSA_HEADS):
            slope_row = jnp.where(lane == h, slopes_ref[h], slope_row)
        n_col = lax.broadcasted_iota(I32, (NCH, 1), 0)
        dist = P - (n_col * CMP_STRIDE + (CMP_BLOCK - 1))
        maskc = (dist >= 0) & (n_col < NCH - 1)
        dist_f = dist.astype(F32)

        def cmp_rows(kind, g):
            lead = lt_ref[kind * NSA_KV + g, :, 0:HD]
            trail = lt_ref[kind * NSA_KV + g, :, HD:2 * HD]
            hid = _gelu(lead + pltpu.roll(trail, NCH - 1, axis=0) + cb_ref[kind, 0:1])
            return _dot(hid.astype(BF16), w2_ref[kind].astype(BF16)).astype(BF16)

        oc = jnp.zeros((128, HD), F32)
        psum2 = jnp.zeros((NCH, 128), F32)
        for g in range(NSA_KV):
            kc = cmp_rows(0, g)
            vc = cmp_rows(1, g)
            s = _dot_nt(kc, q128) * scale - slope_row * dist_f
            s = jnp.where(maskc, s, NEG_INF)
            m = jnp.max(s, axis=0, keepdims=True)
            pt = jnp.where(maskc, jnp.exp(s - m), 0.0)
            l = jnp.sum(pt, axis=0, keepdims=True)
            pt = pt / jnp.where(l > 0.0, l, 1.0)
            in_g = (lane >= g * NSA_GROUP) & (lane < (g + 1) * NSA_GROUP)
            pg = jnp.where(in_g, pt, 0.0)
            oc = oc + _dot(pg.T.astype(BF16), vc)
            psum2 = psum2 + jnp.where(lane == g, jnp.sum(pg, axis=1, keepdims=True), 0.0)
        oc_ref[0] = oc[0:NSA_HEADS]

        p_hi = psum2.astype(BF16)
        p_lo = (psum2 - p_hi.astype(F32)).astype(BF16)
        jo = lax.broadcasted_iota(I32, (NSP, NCH), 0) * SEL_BLOCK
        no = lax.broadcasted_iota(I32, (NSP, NCH), 1)
        ov = ((no * CMP_STRIDE < jo + SEL_BLOCK) & (no * CMP_STRIDE + CMP_BLOCK > jo)
              & (no < NCH - 1)).astype(BF16)
        imp = _dot(ov, p_hi) + _dot(ov, p_lo)
        jcol = lax.broadcasted_iota(I32, (NSP, 1), 0)
        cur = P // SEL_BLOCK
        forced = (jcol == 0) | (jcol == cur) | (jcol == cur - 1)
        imp = jnp.where(forced, imp + FORCE_BONUS, imp)
        imp = jnp.where(jcol * SEL_BLOCK <= P, imp, -1.0)
        imp = jnp.where(jcol < n_sel, imp, -2.0)
        ri = lax.broadcasted_iota(I32, (NSP, NSP), 0)
        ci = lax.broadcasted_iota(I32, (NSP, NSP), 1)
        lane_n = lax.broadcasted_iota(I32, (NSP, 128), 1)
        jcol_f = jcol.astype(F32)
        idx_ref[0] = jnp.zeros((8, 128), I32)
        for g in range(NSA_KV):
            col = jnp.sum(jnp.where(lane_n == g, imp, 0.0), axis=1, keepdims=True)
            rowv = jnp.sum(jnp.where(ri == ci, col, 0.0), axis=0, keepdims=True)
            beats = (rowv > col) | ((rowv == col) & (ci < ri))
            rank = jnp.sum(beats.astype(F32), axis=1, keepdims=True)
            hit = rank == lane_n.astype(F32)
            idx_ref[0, g:g + 1, :] = jnp.sum(jnp.where(hit, jcol_f, 0.0), axis=0,
                                             keepdims=True).astype(I32)


def sample_cmp_select(page_table, pool, layer, n_pool, w1c, cb, w2, q8, slopes):
    DB, NP = page_table.shape
    GRP = min(PAGES_PER_STEP, NP)
    assert NP % GRP == 0
    NCH = NP * CHUNKS_PER_PAGE
    K2 = CMP_STRIDE * HD
    base = layer * n_pool

    def page_spec(k):
        return pl.BlockSpec((1, ROWS_PER_PAGE, HD),
                            lambda b, p, pt: (base + pt[b * NP + p * GRP + k], 0, 0))

    grid_spec = pltpu.PrefetchScalarGridSpec(
        num_scalar_prefetch=1,
        grid=(DB, NP // GRP),
        in_specs=[pl.BlockSpec(memory_space=pltpu.SMEM)] + [page_spec(k) for k in range(GRP)] + [
                  pl.BlockSpec((2, K2, 2 * HD), lambda b, p, pt: (0, 0, 0)),
                  pl.BlockSpec((2, 8, HD), lambda b, p, pt: (0, 0, 0)),
                  pl.BlockSpec((2, HD, HD), lambda b, p, pt: (0, 0, 0)),
                  pl.BlockSpec((1, NSA_HEADS, HD), lambda b, p, pt: (b, 0, 0))],
        out_specs=[pl.BlockSpec((1, NSA_HEADS, HD), lambda b, p, pt: (b, 0, 0)),
                   pl.BlockSpec((1, 8, 128), lambda b, p, pt: (b, 0, 0))],
        scratch_shapes=[pltpu.VMEM((2 * NSA_KV, GRP * CHUNKS_PER_PAGE, K2), F32),
                        pltpu.VMEM((2 * NSA_KV, NCH, 2 * HD), F32)],
    )
    return pl.pallas_call(
        functools.partial(_sample_cmp_body, NP=NP, GRP=GRP),
        grid_spec=grid_spec,
        out_shape=[jax.ShapeDtypeStruct((DB, NSA_HEADS, HD), F32),
                   jax.ShapeDtypeStruct((DB, 8, 128), I32)],
        compiler_params=_cp("arbitrary", "arbitrary"),
        name="sample_cmp_select",
    )(page_table.reshape(-1), slopes, *([pool] * GRP), w1c, cb, w2, q8)


def _sample_sel_body(idx_ref, pt_ref, slopes_ref, *refs, NP, WB):
    page_refs = refs[:SEL_TOPK]
    win_ref, q_ref, new_ref, oc_ref, sm_ref, o_ref = refs[SEL_TOPK:]
    b = pl.program_id(0)
    g = pl.program_id(1)
    P = NP * PAGE_SIZE
    n_past = P // SEL_BLOCK
    scale = HD ** -0.5
    q8 = q_ref[0].astype(BF16)
    rowi = lax.broadcasted_iota(I32, (NSA_HEADS, 1), 0)
    slope = jnp.zeros((NSA_HEADS, 1), F32)
    for h in range(NSA_HEADS):
        slope = jnp.where(rowi == h, slopes_ref[h], slope)
    ibase = (b * NSA_KV + g) * SEL_TOPK

    def vec_dot(krow):
        return jnp.sum(q8.astype(F32) * krow.astype(BF16).astype(F32), axis=1, keepdims=True)

    def add_token(carry, s, valid, vrow):
        m, l, acc = carry
        s = jnp.where(valid, s, NEG_INF)
        m_new = jnp.maximum(m, s)
        alpha = jnp.exp(m - m_new)
        p = jnp.where(valid, jnp.exp(s - m_new), 0.0)
        return (m_new, alpha * l + p,
                alpha * acc + p.astype(BF16).astype(F32) * vrow.astype(BF16).astype(F32))

    n_new = jnp.int32(0)
    lane = lax.broadcasted_iota(I32, (1, PAGE_SIZE), 1)
    kbs, vbs, kps, oks = [], [], [], []
    for t in range(SEL_TOPK):
        j = idx_ref[ibase + t]
        n_new = n_new + (j == n_past).astype(I32)
        kbs.append(page_refs[t][0, pl.ds(2 * NSA_KV + g, PAGE_SIZE, stride=8), :].astype(BF16))
        vbs.append(page_refs[t][0, pl.ds(3 * NSA_KV + g, PAGE_SIZE, stride=8), :].astype(BF16))
        kpos = (j // 2) * PAGE_SIZE + lane
        jv = jnp.full((1, PAGE_SIZE), j, I32)
        kps.append(kpos)
        oks.append(jnp.where(((kpos // SEL_BLOCK) == jv) & (jv < n_past), 1, 0))
    kpos = jnp.concatenate(kps, axis=1)
    s = _dot_nt(q8, jnp.concatenate(kbs, axis=0)) * scale - slope * (P - kpos).astype(F32)
    carry = _online_update((jnp.full((NSA_HEADS, 1), NEG_INF, F32), jnp.zeros((NSA_HEADS, 1), F32),
                            jnp.zeros((NSA_HEADS, HD), F32)),
                           s, jnp.concatenate(oks, axis=1) > 0, jnp.concatenate(vbs, axis=0))

    has_new = jnp.full((NSA_HEADS, 1), n_new, I32) > 0
    k_new = new_ref[0, pl.ds(2 * NSA_KV + g, 1), :]
    v_new = new_ref[0, pl.ds(3 * NSA_KV + g, 1), :]
    _, l_s, acc_s = add_token(carry, vec_dot(k_new) * scale, has_new, v_new)
    o_s = acc_s / l_s

    kw = win_ref[0, pl.ds(g, WB, stride=2 * NSA_KV), :].astype(BF16)
    vw = win_ref[0, pl.ds(NSA_KV + g, WB, stride=2 * NSA_KV), :].astype(BF16)
    d = WB - lax.broadcasted_iota(I32, (1, WB), 1)
    s = _dot_nt(q8, kw) * scale - slope * d.astype(F32)
    mask = d < WINDOW
    s = jnp.where(mask, s, NEG_INF)
    m = jnp.max(s, axis=1, keepdims=True)
    p = jnp.where(mask, jnp.exp(s - m), 0.0)
    carry = (m, jnp.sum(p, axis=1, keepdims=True), _dot(p.astype(BF16), vw))
    kw_new = new_ref[0, pl.ds(4 * NSA_KV + g, 1), :]
    vw_new = new_ref[0, pl.ds(5 * NSA_KV + g, 1), :]
    _, l_w, acc_w = add_token(carry, vec_dot(kw_new) * scale, rowi >= 0, vw_new)
    o_w = acc_w / l_w

    smb = jnp.broadcast_to(sm_ref[0], (NSA_HEADS, 128))
    lane_h = lax.broadcasted_iota(I32, (NSA_HEADS, 128), 1)

    def gate(c):
        return _sigmoid(jnp.sum(jnp.where(lane_h == rowi * 3 + c, smb, 0.0), axis=1, keepdims=True))

    o_ref[0, 0] = gate(0) * oc_ref[0] + gate(1) * o_s + gate(2) * o_w


def sample_sel_win(idx, page_table, pool, win, layer, n_pool, q8, newkv, oc, sm, slopes):
    DB, NP = page_table.shape
    WB = win.shape[1] // (2 * NSA_KV)
    n_past = NP * PAGE_SIZE // SEL_BLOCK
    pbase = layer * n_pool
    wbase = layer * DB
    idx_flat = idx[:, :NSA_KV, :SEL_TOPK].reshape(-1)

    def page_spec(t):
        def page_map(b, g, idx_r, pt_r):
            j = jnp.minimum(idx_r[(b * NSA_KV + g) * SEL_TOPK + t], n_past - 1)
            return (pbase + pt_r[b * NP + j // 2], 0, 0)
        return pl.BlockSpec((1, ROWS_PER_PAGE, HD), page_map)

    grid_spec = pltpu.PrefetchScalarGridSpec(
        num_scalar_prefetch=2,
        grid=(DB, NSA_KV),
        in_specs=[pl.BlockSpec(memory_space=pltpu.SMEM)] + [page_spec(t) for t in range(SEL_TOPK)] + [
                  pl.BlockSpec((1, WB * 2 * NSA_KV, HD), lambda b, g, i_, p_: (wbase + b, 0, 0)),
                  pl.BlockSpec((1, NSA_HEADS, HD), lambda b, g, i_, p_: (b, 0, 0)),
                  pl.BlockSpec((1, 6 * NSA_KV, HD), lambda b, g, i_, p_: (b, 0, 0)),
                  pl.BlockSpec((1, NSA_HEADS, HD), lambda b, g, i_, p_: (b, 0, 0)),
                  pl.BlockSpec((1, 1, 128), lambda b, g, i_, p_: (b, 0, 0))],
        out_specs=pl.BlockSpec((1, 1, NSA_HEADS, HD), lambda b, g, i_, p_: (b, g, 0, 0)),
    )
    return pl.pallas_call(
        functools.partial(_sample_sel_body, NP=NP, WB=WB),
        grid_spec=grid_spec,
        out_shape=jax.ShapeDtypeStruct((DB, NSA_KV, NSA_HEADS, HD), F32),
        compiler_params=_cp("arbitrary", "arbitrary"),
        name="sample_sel_win",
    )(idx_flat, page_table.reshape(-1), slopes, *([pool] * SEL_TOPK), win, q8, newkv, oc, sm)


def _mlstm_step_body(bif_ref, gs_ref, ms_ref, qkv_ref, mo_ref, ng_ref, c_ref, n_ref,
                     o_ref, co_ref, no_ref, mo_out_ref):
    b = pl.program_id(0)
    H = M_HEADS
    ri = lax.broadcasted_iota(I32, (HD, HD), 0)
    ci = lax.broadcasted_iota(I32, (HD, HD), 1)
    for h in range(H):
        ig = jnp.full((1, HD), gs_ref[b, h] + bif_ref[0, h], F32)
        fp = jnp.full((1, HD), gs_ref[b, H + h] + bif_ref[1, h], F32)
        m = jnp.full((1, HD), ms_ref[b, h], F32)
        lf = -_softplus(-fp)
        inter = lf + m
        m_t = jnp.maximum(inter, ig)
        w_intra = jnp.exp(ig - m_t)
        w_inter = jnp.exp(inter - m_t)
        q = qkv_ref[0, h:h + 1, :]
        k = qkv_ref[0, H + h:H + h + 1, :] * (HD ** -0.5)
        v = qkv_ref[0, 2 * H + h:2 * H + h + 1, :]
        C = c_ref[0, h]
        n = n_ref[0, h:h + 1, :]
        s = jnp.sum(q * k, axis=1, keepdims=True) * w_intra
        qC = _dot(jnp.broadcast_to(q, (8, HD)).astype(BF16), C.astype(BF16))[0:1]
        num = s * v + w_inter * qC
        den = s + w_inter * jnp.sum(q * n, axis=1, keepdims=True)
        hh = num / jnp.maximum(jnp.abs(den), jnp.exp(-m_t))
        k_col = jnp.sum(jnp.where(ri == ci, jnp.broadcast_to(k, (HD, HD)), 0.0), axis=1, keepdims=True)
        co_ref[0, h] = w_inter * C + (k_col * w_intra) * v
        no_ref[0, h:h + 1, :] = w_inter * n + w_intra * k
        mo_out_ref[0, h:h + 1, :] = m_t
        mu = jnp.mean(hh, axis=1, keepdims=True)
        hc = hh - mu
        var = jnp.mean(hc * hc, axis=1, keepdims=True)
        hn = hc * lax.rsqrt(var + LN_EPS) * ng_ref[h:h + 1, :]
        o_ref[0, h:h + 1, :] = _sigmoid(mo_ref[0, h:h + 1, :]) * hn


def mlstm_step(b_if, gates, m_state, qkv, mo, norm_g, C, n):
    DB = qkv.shape[0]
    H = M_HEADS
    smem = pl.BlockSpec(memory_space=pltpu.SMEM)
    row = pl.BlockSpec((1, H, HD), lambda b: (b, 0, 0))
    cspec = pl.BlockSpec((1, H, HD, HD), lambda b: (b, 0, 0, 0))
    return pl.pallas_call(
        _mlstm_step_body,
        grid=(DB,),
        in_specs=[smem, smem, smem, pl.BlockSpec((1, 3 * H, HD), lambda b: (b, 0, 0)), row,
                  pl.BlockSpec((H, HD), lambda b: (0, 0)), cspec, row],
        out_specs=[row, cspec, row, row],
        out_shape=[jax.ShapeDtypeStruct((DB, H, HD), F32), jax.ShapeDtypeStruct((DB, H, HD, HD), F32),
                   jax.ShapeDtypeStruct((DB, H, HD), F32), jax.ShapeDtypeStruct((DB, H, HD), F32)],
        compiler_params=_cp("arbitrary"),
        name="mlstm_step",
    )(b_if, gates, m_state, qkv, mo, norm_g.reshape(H, HD), C, n)


def _rglru_step_body(rx_ref, rgt_ref, buf_ref, h_ref, cw_ref, vec_ref, wa_ref, wx_ref, o_ref, ho_ref):
    cw = cw_ref[...]
    vec = vec_ref[...]
    xc = vec[0:1] + rx_ref[...] * cw[CONV_W - 1:CONV_W]
    for j in range(CONV_W - 1):
        xc = xc + buf_ref[j] * cw[j:j + 1]
    a, u = _rglru_gates(xc, [wa_ref[0].astype(BF16)], [wx_ref[0].astype(BF16)], vec)
    h = a * h_ref[...] + u
    ho_ref[...] = h
    o_ref[...] = h * _gelu(rgt_ref[...])


def rglru_step(rx, rgt, buf_t, h0, conv_w, vec, w_a, w_x):
    DB = rx.shape[0]
    col = pl.BlockSpec((DB, 128), lambda n: (0, n))
    return pl.pallas_call(
        _rglru_step_body,
        grid=(R_BLOCKS,),
        in_specs=[col, col, pl.BlockSpec((CONV_W - 1, DB, 128), lambda n: (0, 0, n)), col,
                  pl.BlockSpec((CONV_W, 128), lambda n: (0, n)),
                  pl.BlockSpec((4, 128), lambda n: (0, n)),
                  pl.BlockSpec((1, 128, 128), lambda n: (n, 0, 0)),
                  pl.BlockSpec((1, 128, 128), lambda n: (n, 0, 0))],
        out_specs=[col, col],
        out_shape=[jax.ShapeDtypeStruct((DB, R_WIDTH), F32), jax.ShapeDtypeStruct((DB, R_WIDTH), F32)],
        compiler_params=_cp("arbitrary"),
        name="rglru_step",
    )(rx, rgt, buf_t, h0, conv_w, vec, w_a, w_x)


def pack_w_in(w):
    wt = jnp.swapaxes(w, 1, 2)
    pieces = [(8744, 14888), (0, 1024), (2584, 5656), (5672, 6696), (6696, 7720), (7720, 8744),
              (1024, 2560), (2560, 2584), (5656, 5672)]
    n_src = sum(b - a for a, b in pieces)
    parts = [wt[:, a:b] for a, b in pieces] + [jnp.zeros((w.shape[0], N_AL - n_src, w.shape[1]), w.dtype)]
    return jnp.concatenate(parts, axis=1).astype(BF16)


def prompt_layer(xp, lw, gw, layer, kv_buf, depth):
    B, T, D = xp.shape
    x2 = xp.reshape(B * T, D)
    M = B * T
    z = mm_nt(x2, gw["w_in"], layer, tm=min(IN_TM, M), tn=IN_TN, out_dtype=F32)
    z3 = z.reshape(B, T, N_AL)
    kv_buf = kv_rows(z, kv_buf, layer, depth)
    kc, vc, cb = compress_from_z(z3, lw["phi_w1"], lw["pe"], lw["phi_b1"], lw["phi_w2"])
    o_a = nsa_prompt(z3, kc, vc, lw["slopes"])
    o_b, Cp, np_, mp = mlstm_prompt(z3, lw["b_if"], lw["norm_g"])
    o_c, hp = rglru_prompt(z3, lw["conv_w"], lw["rg_vec"], lw["w_a"], lw["w_x"])
    merged = merge(o_a.reshape(B * T, -1), o_b.reshape(B * T, -1), o_c.reshape(B * T, -1), z,
                   gw["w_branch"], layer, tm=min(MERGE_TM, M), tn=MERGE_TN, out_dtype=BF16)
    h = mm_res_ln(merged, gw["w_out"], layer, x2, lw["ln_g"][0:1], lw["ln_b"][0:1], tm=min(OUT_TM, M), tk=D)
    f1 = mm(h, gw["mlp_w1"], layer, tm=min(UP_TM, M), tn=UP_TN, out_dtype=BF16, act="relu2")
    x_new = mm_res_ln(f1, gw["mlp_w2"], layer, h, lw["ln_g"][1:2], lw["ln_b"][1:2], tm=min(DOWN_TM, M),
                      tk=DOWN_TK)
    n_win = min(WINDOW, T)
    wkv = z3[:, T - n_win:, OFF_KV + 4 * NSA_KV * HD:OFF_KV + 6 * NSA_KV * HD]
    states = (wkv.reshape(B, n_win, 2, NSA_KV, HD), Cp, np_[:, :, 0], mp[:, :, 0, 0], hp[:, 0],
              z3[:, T - (CONV_W - 1):, OFF_RX:OFF_RX + R_WIDTH])
    return x_new.reshape(B, T, D), states, cb, kv_buf


def sample_layer(xs, lw, gw, cb, layer, page_table, pool, n_pool, win, win_l, C0, n0, m0, h0, conv0):
    DB, D = xs.shape
    z = mm_nt(xs, gw["w_in"], layer, tm=DB, tn=IN_TN, out_dtype=F32)
    q8 = z[:, OFF_Q:OFF_Q + NSA_WIDTH].reshape(DB, NSA_HEADS, HD)
    newkv = z[:, OFF_KV:OFF_KV + 6 * NSA_KV * HD].reshape(DB, 6 * NSA_KV, HD)
    sm = z[:, OFF_SM:OFF_SM + 128].reshape(DB, 1, 128)
    oc, idx = sample_cmp_select(page_table, pool, layer, n_pool, lw["w1c"], cb, lw["phi_w2"], q8,
                                lw["slopes"])
    osw = sample_sel_win(idx, page_table, pool, win, layer, n_pool, q8, newkv, oc, sm, lw["slopes"])
    o_a = jnp.concatenate([osw[:, g, g * NSA_GROUP:(g + 1) * NSA_GROUP] for g in range(NSA_KV)],
                          axis=1).reshape(DB, NSA_WIDTH)
    qkv = z[:, OFF_MQKV:OFF_MQKV + 3 * M_WIDTH].reshape(DB, 3 * M_HEADS, HD)
    mo = z[:, OFF_MO:OFF_MO + M_WIDTH].reshape(DB, M_HEADS, HD)
    gates = z[:, OFF_SM + SM_I:OFF_SM + SM_I + 2 * M_HEADS]
    o_b, Cs, ns, ms = mlstm_step(lw["b_if"], gates, m0, qkv, mo, lw["norm_g"], C0, n0)
    rx = z[:, OFF_RX:OFF_RX + R_WIDTH]
    o_c, hs = rglru_step(rx, z[:, OFF_RGT:OFF_RGT + R_WIDTH], conv0.transpose(1, 0, 2), h0,
                         lw["conv_w"], lw["rg_vec"], lw["w_a"], lw["w_x"])
    merged = merge(o_a, o_b.reshape(DB, M_WIDTH), o_c, z, gw["w_branch"], layer, tm=DB, tn=MERGE_TN,
                   out_dtype=F32)
    h = mm_res_ln(merged, gw["w_out"], layer, xs, lw["ln_g"][0:1], lw["ln_b"][0:1], tm=DB, tk=D)
    f1 = mm(h, gw["mlp_w1"], layer, tm=DB, tn=UP_TN, out_dtype=F32, act="relu2")
    x_new = mm_res_ln(f1, gw["mlp_w2"], layer, h, lw["ln_g"][1:2], lw["ln_b"][1:2], tm=DB, tk=DOWN_TK)
    kvn = newkv.reshape(DB, 1, 6, NSA_KV, HD)
    Wb = win_l.shape[1]
    keep = min(WINDOW, Wb + 1)
    win_new = jnp.concatenate([win_l, kvn[:, :, 4:6]], axis=1)[:, Wb + 1 - keep:]
    conv_new = jnp.concatenate([conv0[:, 1:], rx[:, None]], axis=1)
    states = (kvn[:, :, :4], win_new, Cs, ns, ms[:, :, 0], hs, conv_new)
    return x_new, states


def kernel(x_prompt, x_sample, cache_nsa_kv, cache_win_kv, state_mlstm_C, state_mlstm_n, state_mlstm_m,
           state_rglru_h, state_rglru_conv, page_table, w_in, nsa_pe, nsa_phi_w1, nsa_phi_b1, nsa_phi_w2,
           mlstm_b_if, mlstm_norm_g, rg_conv_w, rg_conv_b, rg_w_a, rg_b_a, rg_w_x, rg_b_x, rg_lambda,
           w_branch, w_out, ln_g, ln_b, mlp_w1, mlp_w2):
    DB, Tn, D = x_sample.shape
    assert Tn == 1 and D == D_MODEL
    depth, n_pool = cache_nsa_kv.shape[:2]
    pool = cache_nsa_kv.reshape(depth * n_pool, ROWS_PER_PAGE, HD)
    Wb = cache_win_kv.shape[2]
    win = cache_win_kv.reshape(depth * DB, Wb * 2 * NSA_KV, HD)
    slopes = alibi_slopes()
    half = CMP_STRIDE * HD
    B, T = x_prompt.shape[:2]
    xp = x_prompt
    xs = x_sample.reshape(DB, D)
    gw = {"w_in": pack_w_in(w_in), "w_branch": w_branch.astype(BF16), "w_out": w_out.astype(BF16),
          "mlp_w1": mlp_w1.astype(BF16), "mlp_w2": mlp_w2.astype(BF16)}
    new_p = [[] for _ in range(6)]
    new_s = [[] for _ in range(7)]
    kv_buf = None
    for l in range(depth):
        lw = {
            "pe": nsa_pe[l], "phi_w1": nsa_phi_w1[l], "phi_b1": nsa_phi_b1[l], "phi_w2": nsa_phi_w2[l],
            "w1c": jnp.concatenate([nsa_phi_w1[l][:, :half], nsa_phi_w1[l][:, half:]], axis=2).astype(BF16),
            "slopes": slopes, "b_if": mlstm_b_if[l], "norm_g": mlstm_norm_g[l],
            "conv_w": rg_conv_w[l],
            "rg_vec": jnp.stack([rg_conv_b[l], rg_b_a[l], rg_b_x[l], rg_lambda[l]]),
            "w_a": rg_w_a[l], "w_x": rg_w_x[l],
            "ln_g": ln_g[l], "ln_b": ln_b[l],
        }
        xp, st_p, cb, kv_buf = prompt_layer(xp, lw, gw, l, kv_buf, depth)
        xs, st_s = sample_layer(xs, lw, gw, cb, l, page_table, pool, n_pool, win, cache_win_kv[l],
                                state_mlstm_C[l], state_mlstm_n[l], state_mlstm_m[l],
                                state_rglru_h[l], state_rglru_conv[l])
        for lst, val in zip(new_p, st_p):
            lst.append(val)
        for lst, val in zip(new_s, st_s):
            lst.append(val)
    P = [jnp.stack(a) for a in new_p]
    S = [jnp.stack(a) for a in new_s]
    nsa_kv_p = kv_buf.reshape(depth, B, T, 4, NSA_KV, HD)
    return (xp, xs.reshape(DB, Tn, D), nsa_kv_p, S[0], P[0], S[1], P[1], S[2], P[2], S[3], P[3], S[4],
            P[4], S[5], P[5], S[6])
```

```python
import functools

import jax
import jax.numpy as jnp
from jax import lax
from jax.experimental import pallas as pl
from jax.experimental.pallas import tpu as pltpu

F32 = jnp.float32
BF16 = jnp.bfloat16
I32 = jnp.int32

D_MODEL = 2048
DEPTH = 4
PAGE_SIZE = 128
HD = 128
NSA_HEADS = 8
NSA_KV = 2
NSA_GROUP = 4
NSA_WIDTH = NSA_HEADS * HD
CMP_BLOCK = 32
CMP_STRIDE = 16
SEL_BLOCK = 64
SEL_TOPK = 16
WINDOW = 512
Q_BLOCK = 128
FORCE_BONUS = 1.0e4
M_HEADS = 8
M_WIDTH = M_HEADS * HD
R_WIDTH = 1024
R_BLOCKS = 8
CONV_W = 4
LRU_C = 8.0
D_FF = 4 * D_MODEL
DEEPNORM_ALPHA = (2 * DEPTH) ** 0.25
LN_EPS = 1e-5
NEG_INF = -1e30

OFF_MG = 0
OFF_Q = 6144
OFF_MQKV = 7168
OFF_MO = 10240
OFF_RX = 11264
OFF_RGT = 12288
OFF_KV = 13312
OFF_SM = 14848
N_AL = 15360
SM_I = 24
SM_F = 32

IN_TM, IN_TN = 1024, 1536
MERGE_TM, MERGE_TN = 512, 2048
OUT_TM = 512
UP_TM, UP_TN = 1024, 1024
DOWN_TM, DOWN_TK = 512, 2048

VMEM_LIMIT = 52 * 1024 * 1024


def _cp(*sem):
    return pltpu.CompilerParams(dimension_semantics=sem, vmem_limit_bytes=VMEM_LIMIT)


def _dot(a, b):
    return jnp.dot(a, b, preferred_element_type=F32)


def _dot_nt(a, b):
    return lax.dot_general(a, b, (((1,), (1,)), ((), ())), preferred_element_type=F32)


def _gelu(x):
    return 0.5 * x * (1.0 + jnp.tanh(0.7978845608028654 * (x + 0.044715 * (x * x * x))))


def _sigmoid(x):
    return 0.5 * jnp.tanh(0.5 * x) + 0.5


def _softplus(x):
    return jnp.maximum(x, 0.0) + jnp.log1p(jnp.exp(-jnp.abs(x)))


def _mm_body(a_ref, b_ref, o_ref, *, act):
    acc = _dot(a_ref[...].astype(BF16), b_ref[...])
    if act == "relu2":
        acc = jnp.square(jnp.maximum(acc, 0.0))
    o_ref[...] = acc.astype(o_ref.dtype)


def _mm_nt_body(a_ref, bt_ref, o_ref):
    o_ref[...] = _dot_nt(a_ref[...].astype(BF16), bt_ref[...]).astype(o_ref.dtype)


def mm_nt(a, bt, layer, *, tm, tn, out_dtype):
    M, K = a.shape
    N = bt.shape[1]
    assert M % tm == 0 and N % tn == 0
    return pl.pallas_call(
        _mm_nt_body,
        grid=(M // tm, N // tn),
        in_specs=[pl.BlockSpec((tm, K), lambda i, j: (i, 0)),
                  pl.BlockSpec((None, tn, K), lambda i, j: (layer, j, 0))],
        out_specs=pl.BlockSpec((tm, tn), lambda i, j: (i, j)),
        out_shape=jax.ShapeDtypeStruct((M, N), out_dtype),
        compiler_params=_cp("parallel", "arbitrary"),
        name="mm_nt",
    )(a, bt)


def mm(a, b, layer, *, tm, tn, out_dtype, act=None):
    M, K = a.shape
    N = b.shape[2]
    assert M % tm == 0 and N % tn == 0
    return pl.pallas_call(
        functools.partial(_mm_body, act=act),
        grid=(M // tm, N // tn),
        in_specs=[pl.BlockSpec((tm, K), lambda i, j: (i, 0)),
                  pl.BlockSpec((None, K, tn), lambda i, j: (layer, 0, j))],
        out_specs=pl.BlockSpec((tm, tn), lambda i, j: (i, j)),
        out_shape=jax.ShapeDtypeStruct((M, N), out_dtype),
        compiler_params=_cp("parallel", "arbitrary"),
        name="mm",
    )(a, b)


def _mm_ln_body(a_ref, b_ref, x_ref, g_ref, bb_ref, o_ref, acc_ref, *, nk):
    k = pl.program_id(1)

    if nk > 1:
        @pl.when(k == 0)
        def _():
            acc_ref[...] = _dot(a_ref[...].astype(BF16), b_ref[...])

        @pl.when((k > 0) & (k < nk - 1))
        def _():
            acc_ref[...] += _dot(a_ref[...].astype(BF16), b_ref[...])

    @pl.when(k == nk - 1)
    def _():
        y = DEEPNORM_ALPHA * x_ref[...] + _dot(a_ref[...].astype(BF16), b_ref[...])
        if nk > 1:
            y = y + acc_ref[...]
        mu = jnp.mean(y, axis=-1, keepdims=True)
        yc = y - mu
        var = jnp.mean(yc * yc, axis=-1, keepdims=True)
        o_ref[...] = yc * lax.rsqrt(var + LN_EPS) * g_ref[...] + bb_ref[...]


def mm_res_ln(a, b, layer, x, g, bb, *, tm, tk):
    M, K = a.shape
    N = b.shape[2]
    nk = K // tk
    assert M % tm == 0 and K % tk == 0
    return pl.pallas_call(
        functools.partial(_mm_ln_body, nk=nk),
        grid=(M // tm, nk),
        in_specs=[pl.BlockSpec((tm, tk), lambda i, k: (i, k)),
                  pl.BlockSpec((None, tk, N), lambda i, k: (layer, k, 0)),
                  pl.BlockSpec((tm, N), lambda i, k: (i, 0)),
                  pl.BlockSpec((1, N), lambda i, k: (0, 0)),
                  pl.BlockSpec((1, N), lambda i, k: (0, 0))],
        out_specs=pl.BlockSpec((tm, N), lambda i, k: (i, 0)),
        out_shape=jax.ShapeDtypeStruct((M, N), F32),
        scratch_shapes=[pltpu.VMEM((tm, N), F32)],
        compiler_params=_cp("parallel", "arbitrary"),
        name="mm_res_ln",
    )(a, b, x, g, bb)


def _merge_body(oa_ref, ob_ref, oc_ref, g0_ref, g1_ref, g2_ref, w_ref, o_ref):
    acc = _sigmoid(g0_ref[...]) * _dot(oa_ref[...].astype(BF16), w_ref[0])
    acc += _sigmoid(g1_ref[...]) * _dot(ob_ref[...].astype(BF16), w_ref[1])
    acc += _sigmoid(g2_ref[...]) * _dot(oc_ref[...].astype(BF16), w_ref[2])
    o_ref[...] = acc.astype(o_ref.dtype)


def merge(o_a, o_b, o_c, z, w_branch, layer, *, tm, tn, out_dtype):
    M, W = o_a.shape
    N = w_branch.shape[3]
    assert OFF_MG % tn == 0 and N % tn == 0 and M % tm == 0
    gb = OFF_MG // tn
    nb = N // tn
    br = pl.BlockSpec((tm, W), lambda i, j: (i, 0))

    def gspec(k):
        return pl.BlockSpec((tm, tn), lambda i, j: (i, gb + k * nb + j))

    return pl.pallas_call(
        _merge_body,
        grid=(M // tm, nb),
        in_specs=[br, br, br, gspec(0), gspec(1), gspec(2),
                  pl.BlockSpec((None, 3, W, tn), lambda i, j: (layer, 0, 0, j),
                               pipeline_mode=pl.Buffered(1) if nb == 1 else None)],
        out_specs=pl.BlockSpec((tm, tn), lambda i, j: (i, j)),
        out_shape=jax.ShapeDtypeStruct((M, N), out_dtype),
        compiler_params=_cp("parallel", "arbitrary"),
        name="merge",
    )(o_a, o_b, o_c, z, z, z, w_branch)


def _kv_rows_body(z_ref, buf_ref, o_ref):
    del buf_ref
    tm = z_ref.shape[0]
    for kg in range(4 * NSA_KV):
        o_ref[pl.ds(kg, tm, stride=4 * NSA_KV), :] = z_ref[:, kg * HD:(kg + 1) * HD]


def kv_rows(z, buf, layer):
    M = z.shape[0]
    tm = min(512, M)
    nb = M // tm
    W = 4 * NSA_KV * HD
    assert OFF_KV % W == 0 and M % tm == 0
    return pl.pallas_call(
        _kv_rows_body,
        grid=(nb,),
        in_specs=[pl.BlockSpec((tm, W), lambda i: (i, OFF_KV // W)),
                  pl.BlockSpec(memory_space=pl.ANY)],
        out_specs=pl.BlockSpec((tm * 4 * NSA_KV, HD), lambda i: (layer * nb + i, 0)),
        out_shape=jax.ShapeDtypeStruct(buf.shape, F32),
        input_output_aliases={1: 0},
        compiler_params=_cp("arbitrary"),
        name="kv_rows",
    )(z, buf)


def _compress_body(z_ref, w1_ref, pe_ref, b1_ref, w2_ref, o_ref, cb_ref, *, n):
    half = CMP_STRIDE * HD
    x = jnp.concatenate([z_ref[0, pl.ds(c, n, stride=CMP_STRIDE), :] for c in range(CMP_STRIDE)],
                        axis=1).astype(BF16)
    w1 = w1_ref[0].astype(BF16)
    lead = _dot(x, w1[:half])
    trail = _dot(x, w1[half:])
    pe8 = jnp.broadcast_to(pe_ref[0], (8, 2 * half)).astype(BF16)
    cb = _dot(pe8, w1) + b1_ref[0]
    trail_next = pltpu.roll(trail, n - 1, axis=0)
    hid = _gelu(lead + trail_next + cb[0:1])
    o_ref[0, 0] = _dot(hid.astype(BF16), w2_ref[0].astype(BF16))
    cb_ref[0] = cb


def compress_from_z(z3, w1, pe, b1, w2):
    B, T, _ = z3.shape
    G = NSA_KV
    S = B * G
    n = T // CMP_STRIDE
    K = CMP_STRIDE * HD
    kvb = OFF_KV // HD
    out, cb = pl.pallas_call(
        functools.partial(_compress_body, n=n),
        grid=(2, S),
        in_specs=[pl.BlockSpec((1, T, HD), lambda k, s: (s // G, 0, kvb + k * G + s % G)),
                  pl.BlockSpec((1, 2 * K, HD), lambda k, s: (k, 0, 0)),
                  pl.BlockSpec((1, 1, 2 * K), lambda k, s: (k, 0, 0)),
                  pl.BlockSpec((1, 1, HD), lambda k, s: (k, 0, 0)),
                  pl.BlockSpec((1, HD, HD), lambda k, s: (k, 0, 0))],
        out_specs=[pl.BlockSpec((1, 1, n, HD), lambda k, s: (k, s, 0, 0)),
                   pl.BlockSpec((1, 8, HD), lambda k, s: (k, 0, 0))],
        out_shape=[jax.ShapeDtypeStruct((2, S, n, HD), F32),
                   jax.ShapeDtypeStruct((2, 8, HD), F32)],
        compiler_params=_cp("arbitrary", "arbitrary"),
        name="compress_prompt",
    )(z3, w1, pe.reshape(2, 1, CMP_BLOCK * HD), b1.reshape(2, 1, HD), w2)
    return out[0], out[1], cb


def alibi_slopes():
    return jnp.asarray([2.0 ** (-8.0 * (h + 1) / NSA_HEADS) for h in range(NSA_HEADS)], F32)


SEL_CHUNK = 512


def _online_update(carry, s, mask, v):
    m, l, acc = carry
    s = jnp.where(mask, s, NEG_INF)
    m_new = jnp.maximum(m, jnp.max(s, axis=1, keepdims=True))
    alpha = jnp.exp(m - m_new)
    p = jnp.where(mask, jnp.exp(s - m_new), 0.0)
    l = alpha * l + jnp.sum(p, axis=1, keepdims=True)
    acc = alpha * acc + _dot(p.astype(BF16), v)
    return m_new, l, acc


MASK_BIG = 1.0e30
LOG2E = 1.4426950408889634
WIN_KEYS = WINDOW + Q_BLOCK


def _nsa_prompt_body(slopes_ref, q_ref, ks_ref, vs_ref, kw_ref, vw_ref, kc_ref, vc_ref,
                     sm_ref, o_ref, ks_bf, vs_aug, et_bf, kw_pad, vw_aug, wbias, sbias, y_a, y_b,
                     clist, *, T):
    assert T % (2 * SEL_CHUNK) == 0
    g = pl.program_id(1)
    i = pl.program_id(2)
    QB = Q_BLOCK
    R = NSA_GROUP
    RQ = R * QB
    NCP = T // CMP_STRIDE
    NC = NCP - 1
    NS = T // SEL_BLOCK
    NSP = 128
    assert NS <= NSP
    t0 = i * QB
    scale = HD ** -0.5
    c2 = scale * LOG2E

    @pl.when(i == 0)
    def _():
        lane = lax.broadcasted_iota(I32, (T, HD), 1)
        ones_col = jnp.where(lane == 0, 1.0, 0.0).astype(BF16)
        ks_bf[...] = ks_ref[0].astype(BF16)
        vs_aug[:, 0:HD] = vs_ref[0].astype(BF16)
        vs_aug[:, HD:2 * HD] = ones_col
        kk = lax.broadcasted_iota(I32, (T, HD), 0)
        et_bf[...] = jnp.where(kk // SEL_BLOCK == lane, MASK_BIG, 0.0).astype(BF16)
        kw_pad[0:WINDOW, :] = jnp.zeros((WINDOW, HD), BF16)
        kw_pad[WINDOW:WINDOW + T, :] = kw_ref[0].astype(BF16)
        vw_aug[0:WINDOW, :] = jnp.zeros((WINDOW, 2 * HD), BF16)
        vw_aug[WINDOW:WINDOW + T, 0:HD] = vw_ref[0].astype(BF16)
        vw_aug[WINDOW:WINDOW + T, HD:2 * HD] = ones_col
        wd = (lax.broadcasted_iota(I32, (QB, WIN_KEYS), 0) + WINDOW
              - lax.broadcasted_iota(I32, (QB, WIN_KEYS), 1))
        band = (wd >= 0) & (wd < WINDOW)
        wdf = wd.astype(F32)
        lf = lax.broadcasted_iota(I32, (QB, SEL_CHUNK), 1).astype(F32)
        for r in range(R):
            sl = slopes_ref[g * R + r]
            wbias[r * QB:(r + 1) * QB, :] = jnp.where(band, (-LOG2E * sl) * wdf, -MASK_BIG)
            sbias[r * QB:(r + 1) * QB, :] = (LOG2E * sl) * lf

    qb = q_ref[0]
    q4 = jnp.concatenate([qb[:, r * HD:(r + 1) * HD] for r in range(R)], axis=0).astype(BF16)
    row = lax.broadcasted_iota(I32, (RQ, 1), 0)
    qpos = t0 + (row & (QB - 1))
    slope = jnp.concatenate(
        [jnp.full((QB, 1), slopes_ref[g * R + r], F32) for r in range(R)], axis=0)

    kc = kc_ref[0, 0].astype(BF16)
    vc = vc_ref[0, 0].astype(BF16)
    n_idx = lax.broadcasted_iota(I32, (1, NCP), 1)
    dist = qpos - (n_idx * CMP_STRIDE + (CMP_BLOCK - 1))
    mask = (dist >= 0) & (n_idx < NC)
    s = _dot_nt(q4, kc) * scale - slope * dist.astype(F32)
    s = jnp.where(mask, s, NEG_INF)
    m = jnp.max(s, axis=1, keepdims=True)
    p = jnp.where(mask, jnp.exp(s - m), 0.0)
    l = jnp.sum(p, axis=1, keepdims=True)
    p = p * (1.0 / jnp.where(l > 0.0, l, 1.0))
    o_c = _dot(p.astype(BF16), vc)

    psum = p[0:QB]
    for r in range(1, R):
        psum = psum + p[r * QB:(r + 1) * QB]
    p_hi = psum.astype(BF16)
    p_lo = (psum - p_hi.astype(F32)).astype(BF16)
    jo = lax.broadcasted_iota(I32, (NS, NCP), 0) * SEL_BLOCK
    no = lax.broadcasted_iota(I32, (NS, NCP), 1)
    ov = ((no * CMP_STRIDE < jo + SEL_BLOCK) & (no * CMP_STRIDE + CMP_BLOCK > jo)
          & (no < NC)).astype(BF16)
    imp = _dot_nt(ov, p_hi) + _dot_nt(ov, p_lo)
    jj = lax.broadcasted_iota(I32, (NS, QB), 0)
    qp = t0 + lax.broadcasted_iota(I32, (NS, QB), 1)
    cur = qp // SEL_BLOCK
    forced = (jj == 0) | (jj == cur) | (jj == cur - 1)
    imp = jnp.where(forced, imp + FORCE_BONUS, imp)
    imp = jnp.where(jj * SEL_BLOCK <= qp, imp, -1.0)
    ranks = [jnp.zeros((8, QB), F32) for _ in range(NS // 8)]
    for j2 in range(NS):
        rv = imp[j2:j2 + 1, :]
        for v in range(NS // 8):
            blk = imp[8 * v:8 * v + 8]
            if 8 * v > j2:
                beats = rv >= blk
            elif 8 * v + 7 < j2:
                beats = rv > blk
            else:
                beats = (rv > blk) | ((rv == blk) & (jj[8 * v:8 * v + 8] > j2))
            ranks[v] = ranks[v] + jnp.where(beats, 1.0, 0.0)
    rank = jnp.concatenate(ranks, axis=0)
    selm_t = jnp.where(rank < SEL_TOPK, 0.0, -1.0)
    if NSP > NS:
        selm_t = jnp.concatenate([selm_t, jnp.zeros((NSP - NS, QB), F32)], axis=0)
    selm = selm_t.T.astype(BF16)

    sl_col = LOG2E * slope
    rel = (lax.broadcasted_iota(I32, (QB, SEL_CHUNK), 1)
           - lax.broadcasted_iota(I32, (QB, SEL_CHUNK), 0))

    def scores(c):
        k0 = pl.multiple_of(c * SEL_CHUNK, SEL_CHUNK)
        mb = _dot_nt(selm, et_bf[pl.ds(k0, SEL_CHUNK), :])
        mb = jnp.where(rel <= t0 - k0, mb, -MASK_BIG)
        return (_dot_nt(q4, ks_bf[pl.ds(k0, SEL_CHUNK), :]) * c2 + sbias[...]
                + jnp.concatenate([mb] * R, axis=0))

    def consume(c, y, carry):
        m, acc = carry
        k0 = pl.multiple_of(c * SEL_CHUNK, SEL_CHUNK)
        off = sl_col * (k0 - t0).astype(F32)
        m_new = jnp.maximum(m, jnp.max(y, axis=1, keepdims=True) + off)
        p = jnp.exp2(y - (m_new - off))
        acc = jnp.exp2(m - m_new) * acc + _dot(p.astype(BF16), vs_aug[pl.ds(k0, SEL_CHUNK), :])
        return m_new, acc

    n_chunks = (t0 + QB + SEL_CHUNK - 1) // SEL_CHUNK
    bpc = SEL_CHUNK // SEL_BLOCK
    n_act = jnp.int32(0)
    dummy = jnp.int32(T // SEL_CHUNK - 1)
    for c in reversed(range(T // SEL_CHUNK)):
        act = (jnp.sum(selm_t[c * bpc:(c + 1) * bpc] + 1.0) > 0.0) & (c < n_chunks)
        dummy = jnp.where(act, dummy, c)
    for c in range(T // SEL_CHUNK):
        act = (jnp.sum(selm_t[c * bpc:(c + 1) * bpc] + 1.0) > 0.0) & (c < n_chunks)
        clist[n_act] = c
        n_act = n_act + act.astype(I32)
    clist[n_act] = dummy

    def pair(pi, carry):
        k = 2 * pi
        y_b[...] = scores(clist[k + 1])
        carry = consume(clist[k], y_a[...], carry)
        y_a[...] = scores(clist[jnp.minimum(k + 2, n_act)])
        return consume(clist[k + 1], y_b[...], carry)

    init = (jnp.full((RQ, 1), -MASK_BIG, F32), jnp.zeros((RQ, 2 * HD), F32))
    y_a[...] = scores(clist[0])
    _, acc_s = lax.fori_loop(0, (n_act + 1) // 2, pair, init)
    o_s = acc_s[:, 0:HD] * (1.0 / acc_s[:, HD:HD + 1])

    w0 = pl.multiple_of(t0, QB)
    vrow = jnp.where(lax.broadcasted_iota(I32, (1, WIN_KEYS), 1) >= WINDOW - t0, 0.0, -MASK_BIG)
    y = _dot_nt(q4, kw_pad[pl.ds(w0, WIN_KEYS), :]) * c2 + wbias[...] + vrow
    p = jnp.exp2(y - jnp.max(y, axis=1, keepdims=True))
    acc_w = _dot(p.astype(BF16), vw_aug[pl.ds(w0, WIN_KEYS), :])
    o_w = acc_w[:, 0:HD] * (1.0 / acc_w[:, HD:HD + 1])

    smb = sm_ref[0]
    lane = lax.broadcasted_iota(I32, (QB, 128), 1)

    def gate(c):
        cols = []
        for r in range(R):
            idx = (g * R + r) * 3 + c
            cols.append(jnp.sum(jnp.where(lane == idx, smb, 0.0), axis=1, keepdims=True))
        return _sigmoid(jnp.concatenate(cols, axis=0))

    o = gate(0) * o_c + gate(1) * o_s + gate(2) * o_w
    o_ref[0] = jnp.concatenate([o[r * QB:(r + 1) * QB] for r in range(R)], axis=1).astype(o_ref.dtype)


def nsa_prompt(z3, kc, vc, slopes):
    B, T, _ = z3.shape
    G = NSA_KV
    kvb = OFF_KV // HD

    def kvspec(kind):
        return pl.BlockSpec((1, T, HD), lambda b, g, i: (b, 0, kvb + kind * G + g))

    cspec = pl.BlockSpec((1, 1, T // CMP_STRIDE, HD), lambda b, g, i: (b * G + g, 0, 0, 0))
    GW = NSA_GROUP * HD
    assert OFF_Q % GW == 0
    qspec = pl.BlockSpec((1, Q_BLOCK, GW), lambda b, g, i: (b, i, OFF_Q // GW + g))
    return pl.pallas_call(
        functools.partial(_nsa_prompt_body, T=T),
        grid=(B, G, T // Q_BLOCK),
        in_specs=[pl.BlockSpec(memory_space=pltpu.SMEM),
                  qspec, kvspec(2), kvspec(3), kvspec(4), kvspec(5), cspec, cspec,
                  pl.BlockSpec((1, Q_BLOCK, 128), lambda b, g, i: (b, i, OFF_SM // 128))],
        out_specs=pl.BlockSpec((1, Q_BLOCK, GW), lambda b, g, i: (b, i, g)),
        out_shape=jax.ShapeDtypeStruct((B, T, NSA_WIDTH), BF16),
        scratch_shapes=[pltpu.VMEM((T, HD), BF16),
                        pltpu.VMEM((T, 2 * HD), BF16),
                        pltpu.VMEM((T, HD), BF16),
                        pltpu.VMEM((T + WINDOW, HD), BF16),
                        pltpu.VMEM((T + WINDOW, 2 * HD), BF16),
                        pltpu.VMEM((NSA_GROUP * Q_BLOCK, WIN_KEYS), F32),
                        pltpu.VMEM((NSA_GROUP * Q_BLOCK, SEL_CHUNK), F32),
                        pltpu.VMEM((NSA_GROUP * Q_BLOCK, SEL_CHUNK), F32),
                        pltpu.VMEM((NSA_GROUP * Q_BLOCK, SEL_CHUNK), F32),
                        pltpu.SMEM((T // SEL_CHUNK + 8,), I32)],
        compiler_params=_cp("arbitrary", "arbitrary", "arbitrary"),
        name="nsa_prompt",
    )(slopes, z3, z3, z3, z3, z3, kc[:, None], vc[:, None], z3)


M_CHUNK_K = 128


M_HEADS_PER_STEP = 4
M_ROWS_PER_STEP = 512


def _mlstm_body(bif_ref, q_ref, k_ref, v_ref, sm_ref, mo_ref, ng_ref,
                o_ref, c_ref, n_ref, m_ref, c_sc, n_sc, m_sc, *, NT, TB):
    hb = pl.program_id(1)
    t = pl.program_id(2)
    L = M_CHUNK_K
    HB = M_HEADS_PER_STEP
    lane = lax.broadcasted_iota(I32, (L, 128), 1)
    ri = lax.broadcasted_iota(I32, (L, L), 0)
    ci = lax.broadcasted_iota(I32, (L, L), 1)
    eye = ri == ci
    tril = ri >= ci

    @pl.when(t == 0)
    def _():
        c_sc[...] = jnp.zeros_like(c_sc)
        n_sc[...] = jnp.zeros_like(n_sc)
        m_sc[...] = jnp.zeros_like(m_sc)

    def head_chunk(hh, r0):
        h = hb * HB + hh
        cs = slice(hh * HD, (hh + 1) * HD)
        bi = bif_ref[0, h]
        bf = bif_ref[1, h]
        ng = ng_ref[:, cs]
        C = c_sc[hh]
        n = n_sc[hh]
        m = m_sc[hh][:, 0:1]
        q = q_ref[0, pl.ds(r0, L), cs]
        k = k_ref[0, pl.ds(r0, L), cs] * (HD ** -0.5)
        v = v_ref[0, pl.ds(r0, L), cs]
        smb = sm_ref[0, pl.ds(r0, L), :]
        ig_col = jnp.sum(jnp.where(lane == SM_I + h, smb, 0.0), axis=1, keepdims=True) + bi
        fp_col = jnp.sum(jnp.where(lane == SM_F + h, smb, 0.0), axis=1, keepdims=True) + bf
        lf_col = -_softplus(-fp_col)
        lf_row = jnp.sum(jnp.where(eye, lf_col, 0.0), axis=0, keepdims=True)
        ig_row = jnp.sum(jnp.where(eye, ig_col, 0.0), axis=0, keepdims=True)
        b_col = jnp.sum(jnp.where(tril, lf_row, 0.0), axis=1, keepdims=True)
        b_row = jnp.sum(jnp.where(ri <= ci, lf_col, 0.0), axis=0, keepdims=True)
        d = jnp.where(tril, b_col - b_row + ig_row, NEG_INF)
        inter = b_col + m
        m_t = jnp.maximum(inter, jnp.max(d, axis=1, keepdims=True))
        w_intra = jnp.exp(d - m_t)
        w_inter = jnp.exp(inter - m_t)
        qb = q.astype(BF16)
        vb = v.astype(BF16)
        s = _dot_nt(qb, k.astype(BF16)) * w_intra
        num = _dot(s.astype(BF16), vb) + w_inter * _dot(qb, C.astype(BF16))
        den = jnp.sum(s, axis=1, keepdims=True) + w_inter * jnp.sum(q * n, axis=1, keepdims=True)
        hh_ = num * (1.0 / jnp.maximum(jnp.abs(den), jnp.exp(-m_t)))
        m_new = m_t[L - 1:L, :]
        b_last = b_col[L - 1:L, :]
        decay = jnp.exp(b_last + m - m_new)
        w_s = jnp.exp(b_last - b_col + ig_col - m_new)
        kw = k * w_s
        c_sc[hh] = decay * C + _dot(kw.T.astype(BF16), vb)
        n_sc[hh] = decay * n + jnp.sum(kw, axis=0, keepdims=True)
        m_sc[hh] = jnp.broadcast_to(m_new, (1, 128))
        mu = jnp.mean(hh_, axis=1, keepdims=True)
        hc = hh_ - mu
        var = jnp.mean(hc * hc, axis=1, keepdims=True)
        hn = hc * lax.rsqrt(var + LN_EPS) * ng
        o_ref[0, pl.ds(r0, L), cs] = (_sigmoid(mo_ref[0, pl.ds(r0, L), cs]) * hn).astype(o_ref.dtype)

    def sub(c, carry):
        r0 = pl.multiple_of(c * L, L)
        for hh in range(HB):
            head_chunk(hh, r0)
        return carry

    lax.fori_loop(0, TB // L, sub, 0)

    @pl.when(t == NT - 1)
    def _():
        c_ref[0] = c_sc[...]
        n_ref[0] = n_sc[...]
        m_ref[0] = m_sc[...]


def mlstm_prompt(z3, b_if, norm_g):
    B, T, _ = z3.shape
    H = M_HEADS
    HB = M_HEADS_PER_STEP
    TB = min(M_ROWS_PER_STEP, T)
    W = HB * HD
    assert OFF_MQKV % W == 0 and OFF_MO % W == 0 and T % TB == 0

    def hspec(off):
        return pl.BlockSpec((1, TB, W), lambda b, h, t: (b, t, off // W + h))

    return pl.pallas_call(
        functools.partial(_mlstm_body, NT=T // TB, TB=TB),
        grid=(B, H // HB, T // TB),
        in_specs=[pl.BlockSpec(memory_space=pltpu.SMEM),
                  hspec(OFF_MQKV), hspec(OFF_MQKV + M_WIDTH), hspec(OFF_MQKV + 2 * M_WIDTH),
                  pl.BlockSpec((1, TB, 128), lambda b, h, t: (b, t, OFF_SM // 128)),
                  hspec(OFF_MO),
                  pl.BlockSpec((1, W), lambda b, h, t: (0, h))],
        out_specs=[pl.BlockSpec((1, TB, W), lambda b, h, t: (b, t, h)),
                   pl.BlockSpec((1, HB, HD, HD), lambda b, h, t: (b, h, 0, 0)),
                   pl.BlockSpec((1, HB, 1, HD), lambda b, h, t: (b, h, 0, 0)),
                   pl.BlockSpec((1, HB, 1, 128), lambda b, h, t: (b, h, 0, 0))],
        out_shape=[jax.ShapeDtypeStruct((B, T, M_WIDTH), BF16),
                   jax.ShapeDtypeStruct((B, H, HD, HD), F32),
                   jax.ShapeDtypeStruct((B, H, 1, HD), F32),
                   jax.ShapeDtypeStruct((B, H, 1, 128), F32)],
        scratch_shapes=[pltpu.VMEM((HB, HD, HD), F32), pltpu.VMEM((HB, 1, HD), F32),
                        pltpu.VMEM((HB, 1, 128), F32)],
        compiler_params=_cp("arbitrary", "arbitrary", "arbitrary"),
        name="mlstm_prompt",
    )(b_if, z3, z3, z3, z3, z3, norm_g.reshape(1, M_WIDTH))


R_CHUNK = 128


def _rglru_gates(xc, was, wxs, vec):
    xb = xc.astype(BF16)
    nb = len(was)
    ya = jnp.concatenate([_dot(xb[:, j * 128:(j + 1) * 128], was[j]) for j in range(nb)], axis=1)
    yx = jnp.concatenate([_dot(xb[:, j * 128:(j + 1) * 128], wxs[j]) for j in range(nb)], axis=1)
    r = _sigmoid(ya + vec[1:2])
    i = _sigmoid(yx + vec[2:3])
    log_a = -LRU_C * r * _softplus(-vec[3:4])
    a = jnp.exp(log_a)
    at = -jnp.tanh(log_a)
    u = jnp.sqrt(2.0 * at) * lax.rsqrt(1.0 + at) * (i * xc)
    return a, u


R_BLOCKS_PER_STEP = 2


def _rglru_body(rx_ref, rgt_ref, cw_ref, vec_ref, wa_ref, wx_ref, o_ref, hl_ref, xpad_ref, *, T):
    Tc = R_CHUNK
    W = rx_ref.shape[2]
    xpad_ref[0:8, :] = jnp.zeros((8, W), F32)
    xpad_ref[8:T + 8, :] = rx_ref[0]
    cw = cw_ref[...]
    vec = vec_ref[...]
    was = [wa_ref[j].astype(BF16) for j in range(W // 128)]
    wxs = [wx_ref[j].astype(BF16) for j in range(W // 128)]
    rowmod = lax.broadcasted_iota(I32, (Tc, W), 0) & 7

    def chunk(c, h):
        r0 = pl.multiple_of(c * Tc, Tc)
        win = xpad_ref[pl.ds(r0, Tc + 8), :]
        xc = vec[0:1]
        for j in range(CONV_W):
            off = 8 - (CONV_W - 1) + j
            xc = xc + win[off:off + Tc] * cw[j:j + 1]
        a, u = _rglru_gates(xc, was, wxs, vec)
        for sft in (1, 2, 4):
            a1 = pltpu.roll(a, sft, axis=0)
            u1 = pltpu.roll(u, sft, axis=0)
            ok = rowmod >= sft
            u = jnp.where(ok, a * u1 + u, u)
            a = jnp.where(ok, a * a1, a)
        hs = []
        for gi in range(Tc // 8):
            hg = a[gi * 8:(gi + 1) * 8] * h + u[gi * 8:(gi + 1) * 8]
            hs.append(hg)
            h = hg[7:8]
        hf = jnp.concatenate(hs, axis=0)
        o_ref[0, pl.ds(r0, Tc), :] = (hf * _gelu(rgt_ref[0, pl.ds(r0, Tc), :])).astype(o_ref.dtype)
        return h

    h = lax.fori_loop(0, T // Tc, chunk, jnp.zeros((1, W), F32))
    hl_ref[0] = h


def rglru_prompt(z3, conv_w, vec, w_a, w_x):
    B, T, _ = z3.shape
    RB = R_BLOCKS_PER_STEP
    W = RB * 128
    assert OFF_RX % W == 0 and OFF_RGT % W == 0 and R_BLOCKS % RB == 0
    return pl.pallas_call(
        functools.partial(_rglru_body, T=T),
        grid=(B, R_BLOCKS // RB),
        in_specs=[pl.BlockSpec((1, T, W), lambda b, n: (b, 0, OFF_RX // W + n)),
                  pl.BlockSpec((1, T, W), lambda b, n: (b, 0, OFF_RGT // W + n)),
                  pl.BlockSpec((CONV_W, W), lambda b, n: (0, n)),
                  pl.BlockSpec((4, W), lambda b, n: (0, n)),
                  pl.BlockSpec((RB, 128, 128), lambda b, n: (n, 0, 0)),
                  pl.BlockSpec((RB, 128, 128), lambda b, n: (n, 0, 0))],
        out_specs=[pl.BlockSpec((1, T, W), lambda b, n: (b, 0, n)),
                   pl.BlockSpec((1, 1, W), lambda b, n: (b, 0, n))],
        out_shape=[jax.ShapeDtypeStruct((B, T, R_WIDTH), BF16),
                   jax.ShapeDtypeStruct((B, 1, R_WIDTH), F32)],
        scratch_shapes=[pltpu.VMEM((T + 8, W), F32)],
        compiler_params=_cp("arbitrary", "arbitrary"),
        name="rglru_prompt",
    )(z3, z3, conv_w, vec, w_a, w_x)


ROWS_PER_PAGE = PAGE_SIZE * 4 * NSA_KV
CHUNKS_PER_PAGE = PAGE_SIZE // CMP_STRIDE
PAGES_PER_STEP = 16


def _sample_cmp_body(pt_ref, slopes_ref, *refs, NP, GRP):
    page_refs = refs[:GRP]
    w1c_ref, cb_ref, w2_ref, q_ref, oc_ref, idx_ref, xs_ref, lt_ref = refs[GRP:]
    p = pl.program_id(1)
    P = NP * PAGE_SIZE
    CPG = GRP * CHUNKS_PER_PAGE
    NCH = NP * CHUNKS_PER_PAGE
    n_sel = P // SEL_BLOCK + 1
    NSP = -(-n_sel // 128) * 128
    scale = HD ** -0.5

    for k in range(GRP):
        for kg in range(2 * NSA_KV):
            plane = page_refs[k][0, pl.ds(kg, PAGE_SIZE, stride=4 * NSA_KV), :]
            xs_ref[kg, k * CHUNKS_PER_PAGE:(k + 1) * CHUNKS_PER_PAGE, :] = plane.reshape(
                CHUNKS_PER_PAGE, CMP_STRIDE * HD)

    r0 = pl.multiple_of(p * CPG, CPG)
    for kind in range(2):
        x = jnp.concatenate([xs_ref[kind * NSA_KV + g] for g in range(NSA_KV)], axis=0).astype(BF16)
        lt = _dot(x, w1c_ref[kind])
        for g in range(NSA_KV):
            lt_ref[kind * NSA_KV + g, pl.ds(r0, CPG), :] = lt[g * CPG:(g + 1) * CPG]

    @pl.when(p == NP // GRP - 1)
    def _():
        q128 = jnp.concatenate([q_ref[0], jnp.zeros((128 - NSA_HEADS, HD), F32)], axis=0).astype(BF16)
        lane = lax.broadcasted_iota(I32, (1, 128), 1)
        slope_row = jnp.zeros((1, 128), F32)
        for h in range(NSA_HEADS):
            slope_row = jnp.where(lane == h, slopes_ref[h], slope_row)
        n_col = lax.broadcasted_iota(I32, (NCH, 1), 0)
        dist = P - (n_col * CMP_STRIDE + (CMP_BLOCK - 1))
        maskc = (dist >= 0) & (n_col < NCH - 1)
        dist_f = dist.astype(F32)

        def cmp_rows(kind, g):
            lead = lt_ref[kind * NSA_KV + g, :, 0:HD]
            trail = lt_ref[kind * NSA_KV + g, :, HD:2 * HD]
            hid = _gelu(lead + pltpu.roll(trail, NCH - 1, axis=0) + cb_ref[kind, 0:1])
            return _dot(hid.astype(BF16), w2_ref[kind].astype(BF16)).astype(BF16)

        oc = jnp.zeros((128, HD), F32)
        psum2 = jnp.zeros((NCH, 128), F32)
        for g in range(NSA_KV):
            kc = cmp_rows(0, g)
            vc = cmp_rows(1, g)
            s = _dot_nt(kc, q128) * scale - slope_row * dist_f
            s = jnp.where(maskc, s, NEG_INF)
            m = jnp.max(s, axis=0, keepdims=True)
            pt = jnp.where(maskc, jnp.exp(s - m), 0.0)
            l = jnp.sum(pt, axis=0, keepdims=True)
            pt = pt / jnp.where(l > 0.0, l, 1.0)
            in_g = (lane >= g * NSA_GROUP) & (lane < (g + 1) * NSA_GROUP)
            pg = jnp.where(in_g, pt, 0.0)
            oc = oc + _dot(pg.T.astype(BF16), vc)
            psum2 = psum2 + jnp.where(lane == g, jnp.sum(pg, axis=1, keepdims=True), 0.0)
        oc_ref[0] = oc[0:NSA_HEADS]

        p_hi = psum2.astype(BF16)
        p_lo = (psum2 - p_hi.astype(F32)).astype(BF16)
        jo = lax.broadcasted_iota(I32, (NSP, NCH), 0) * SEL_BLOCK
        no = lax.broadcasted_iota(I32, (NSP, NCH), 1)
        ov = ((no * CMP_STRIDE < jo + SEL_BLOCK) & (no * CMP_STRIDE + CMP_BLOCK > jo)
              & (no < NCH - 1)).astype(BF16)
        imp = _dot(ov, p_hi) + _dot(ov, p_lo)
        jcol = lax.broadcasted_iota(I32, (NSP, 1), 0)
        cur = P // SEL_BLOCK
        forced = (jcol == 0) | (jcol == cur) | (jcol == cur - 1)
        imp = jnp.where(forced, imp + FORCE_BONUS, imp)
        imp = jnp.where(jcol * SEL_BLOCK <= P, imp, -1.0)
        imp = jnp.where(jcol < n_sel, imp, -2.0)
        ri = lax.broadcasted_iota(I32, (NSP, NSP), 0)
        ci = lax.broadcasted_iota(I32, (NSP, NSP), 1)
        lane_n = lax.broadcasted_iota(I32, (NSP, 128), 1)
        jcol_f = jcol.astype(F32)
        idx_ref[0] = jnp.zeros((8, 128), I32)
        for g in range(NSA_KV):
            col = jnp.sum(jnp.where(lane_n == g, imp, 0.0), axis=1, keepdims=True)
            rowv = jnp.sum(jnp.where(ri == ci, col, 0.0), axis=0, keepdims=True)
            beats = (rowv > col) | ((rowv == col) & (ci < ri))
            rank = jnp.sum(beats.astype(F32), axis=1, keepdims=True)
            hit = rank == lane_n.astype(F32)
            idx_ref[0, g:g + 1, :] = jnp.sum(jnp.where(hit, jcol_f, 0.0), axis=0,
                                             keepdims=True).astype(I32)


def sample_cmp_select(page_table, pool, layer, n_pool, w1c, cb, w2, q8, slopes):
    DB, NP = page_table.shape
    GRP = min(PAGES_PER_STEP, NP)
    assert NP % GRP == 0
    NCH = NP * CHUNKS_PER_PAGE
    K2 = CMP_STRIDE * HD
    base = layer * n_pool

    def page_spec(k):
        return pl.BlockSpec((1, ROWS_PER_PAGE, HD),
                            lambda b, p, pt: (base + pt[b * NP + p * GRP + k], 0, 0))

    grid_spec = pltpu.PrefetchScalarGridSpec(
        num_scalar_prefetch=1,
        grid=(DB, NP // GRP),
        in_specs=[pl.BlockSpec(memory_space=pltpu.SMEM)] + [page_spec(k) for k in range(GRP)] + [
                  pl.BlockSpec((2, K2, 2 * HD), lambda b, p, pt: (0, 0, 0)),
                  pl.BlockSpec((2, 8, HD), lambda b, p, pt: (0, 0, 0)),
                  pl.BlockSpec((2, HD, HD), lambda b, p, pt: (0, 0, 0)),
                  pl.BlockSpec((1, NSA_HEADS, HD), lambda b, p, pt: (b, 0, 0))],
        out_specs=[pl.BlockSpec((1, NSA_HEADS, HD), lambda b, p, pt: (b, 0, 0)),
                   pl.BlockSpec((1, 8, 128), lambda b, p, pt: (b, 0, 0))],
        scratch_shapes=[pltpu.VMEM((2 * NSA_KV, GRP * CHUNKS_PER_PAGE, K2), F32),
                        pltpu.VMEM((2 * NSA_KV, NCH, 2 * HD), F32)],
    )
    return pl.pallas_call(
        functools.partial(_sample_cmp_body, NP=NP, GRP=GRP),
        grid_spec=grid_spec,
        out_shape=[jax.ShapeDtypeStruct((DB, NSA_HEADS, HD), F32),
                   jax.ShapeDtypeStruct((DB, 8, 128), I32)],
        compiler_params=_cp("arbitrary", "arbitrary"),
        name="sample_cmp_select",
    )(page_table.reshape(-1), slopes, *([pool] * GRP), w1c, cb, w2, q8)


def _sample_sel_body(idx_ref, pt_ref, slopes_ref, *refs, NP, WB):
    page_refs = refs[:SEL_TOPK]
    win_ref, q_ref, new_ref, oc_ref, sm_ref, o_ref = refs[SEL_TOPK:]
    b = pl.program_id(0)
    g = pl.program_id(1)
    P = NP * PAGE_SIZE
    n_past = P // SEL_BLOCK
    scale = HD ** -0.5
    q8 = q_ref[0].astype(BF16)
    rowi = lax.broadcasted_iota(I32, (NSA_HEADS, 1), 0)
    slope = jnp.zeros((NSA_HEADS, 1), F32)
    for h in range(NSA_HEADS):
        slope = jnp.where(rowi == h, slopes_ref[h], slope)
    ibase = (b * NSA_KV + g) * SEL_TOPK

    def vec_dot(krow):
        return jnp.sum(q8.astype(F32) * krow.astype(BF16).astype(F32), axis=1, keepdims=True)

    def add_token(carry, s, valid, vrow):
        m, l, acc = carry
        s = jnp.where(valid, s, NEG_INF)
        m_new = jnp.maximum(m, s)
        alpha = jnp.exp(m - m_new)
        p = jnp.where(valid, jnp.exp(s - m_new), 0.0)
        return (m_new, alpha * l + p,
                alpha * acc + p.astype(BF16).astype(F32) * vrow.astype(BF16).astype(F32))

    n_new = jnp.int32(0)
    lane = lax.broadcasted_iota(I32, (1, PAGE_SIZE), 1)
    kbs, vbs, kps, oks = [], [], [], []
    for t in range(SEL_TOPK):
        j = idx_ref[ibase + t]
        n_new = n_new + (j == n_past).astype(I32)
        kbs.append(page_refs[t][0, pl.ds(2 * NSA_KV + g, PAGE_SIZE, stride=8), :].astype(BF16))
        vbs.append(page_refs[t][0, pl.ds(3 * NSA_KV + g, PAGE_SIZE, stride=8), :].astype(BF16))
        kpos = (j // 2) * PAGE_SIZE + lane
        jv = jnp.full((1, PAGE_SIZE), j, I32)
        kps.append(kpos)
        oks.append(jnp.where(((kpos // SEL_BLOCK) == jv) & (jv < n_past), 1, 0))
    kpos = jnp.concatenate(kps, axis=1)
    s = _dot_nt(q8, jnp.concatenate(kbs, axis=0)) * scale - slope * (P - kpos).astype(F32)
    carry = _online_update((jnp.full((NSA_HEADS, 1), NEG_INF, F32), jnp.zeros((NSA_HEADS, 1), F32),
                            jnp.zeros((NSA_HEADS, HD), F32)),
                           s, jnp.concatenate(oks, axis=1) > 0, jnp.concatenate(vbs, axis=0))

    has_new = jnp.full((NSA_HEADS, 1), n_new, I32) > 0
    k_new = new_ref[0, pl.ds(2 * NSA_KV + g, 1), :]
    v_new = new_ref[0, pl.ds(3 * NSA_KV + g, 1), :]
    _, l_s, acc_s = add_token(carry, vec_dot(k_new) * scale, has_new, v_new)
    o_s = acc_s / l_s

    kw = win_ref[0, pl.ds(g, WB, stride=2 * NSA_KV), :].astype(BF16)
    vw = win_ref[0, pl.ds(NSA_KV + g, WB, stride=2 * NSA_KV), :].astype(BF16)
    d = WB - lax.broadcasted_iota(I32, (1, WB), 1)
    s = _dot_nt(q8, kw) * scale - slope * d.astype(F32)
    mask = d < WINDOW
    s = jnp.where(mask, s, NEG_INF)
    m = jnp.max(s, axis=1, keepdims=True)
    p = jnp.where(mask, jnp.exp(s - m), 0.0)
    carry = (m, jnp.sum(p, axis=1, keepdims=True), _dot(p.astype(BF16), vw))
    kw_new = new_ref[0, pl.ds(4 * NSA_KV + g, 1), :]
    vw_new = new_ref[0, pl.ds(5 * NSA_KV + g, 1), :]
    _, l_w, acc_w = add_token(carry, vec_dot(kw_new) * scale, rowi >= 0, vw_new)
    o_w = acc_w / l_w

    smb = jnp.broadcast_to(sm_ref[0], (NSA_HEADS, 128))
    lane_h = lax.broadcasted_iota(I32, (NSA_HEADS, 128), 1)

    def gate(c):
        return _sigmoid(jnp.sum(jnp.where(lane_h == rowi * 3 + c, smb, 0.0), axis=1, keepdims=True))

    o_ref[0, 0] = gate(0) * oc_ref[0] + gate(1) * o_s + gate(2) * o_w


def sample_sel_win(idx, page_table, pool, win, layer, n_pool, q8, newkv, oc, sm, slopes):
    DB, NP = page_table.shape
    WB = win.shape[1] // (2 * NSA_KV)
    n_past = NP * PAGE_SIZE // SEL_BLOCK
    pbase = layer * n_pool
    wbase = layer * DB
    idx_flat = idx[:, :NSA_KV, :SEL_TOPK].reshape(-1)

    def page_spec(t):
        def page_map(b, g, idx_r, pt_r):
            j = jnp.minimum(idx_r[(b * NSA_KV + g) * SEL_TOPK + t], n_past - 1)
            return (pbase + pt_r[b * NP + j // 2], 0, 0)
        return pl.BlockSpec((1, ROWS_PER_PAGE, HD), page_map)

    grid_spec = pltpu.PrefetchScalarGridSpec(
        num_scalar_prefetch=2,
        grid=(DB, NSA_KV),
        in_specs=[pl.BlockSpec(memory_space=pltpu.SMEM)] + [page_spec(t) for t in range(SEL_TOPK)] + [
                  pl.BlockSpec((1, WB * 2 * NSA_KV, HD), lambda b, g, i_, p_: (wbase + b, 0, 0)),
                  pl.BlockSpec((1, NSA_HEADS, HD), lambda b, g, i_, p_: (b, 0, 0)),
                  pl.BlockSpec((1, 6 * NSA_KV, HD), lambda b, g, i_, p_: (b, 0, 0)),
                  pl.BlockSpec((1, NSA_HEADS, HD), lambda b, g, i_, p_: (b, 0, 0)),
                  pl.BlockSpec((1, 1, 128), lambda b, g, i_, p_: (b, 0, 0))],
        out_specs=pl.BlockSpec((1, 1, NSA_HEADS, HD), lambda b, g, i_, p_: (b, g, 0, 0)),
    )
    return pl.pallas_call(
        functools.partial(_sample_sel_body, NP=NP, WB=WB),
        grid_spec=grid_spec,
        out_shape=jax.ShapeDtypeStruct((DB, NSA_KV, NSA_HEADS, HD), F32),
        compiler_params=_cp("arbitrary", "arbitrary"),
        name="sample_sel_win",
    )(idx_flat, page_table.reshape(-1), slopes, *([pool] * SEL_TOPK), win, q8, newkv, oc, sm)


def _mlstm_step_body(bif_ref, gs_ref, ms_ref, qkv_ref, mo_ref, ng_ref, c_ref, n_ref,
                     o_ref, co_ref, no_ref, mo_out_ref):
    b = pl.program_id(0)
    H = M_HEADS
    ri = lax.broadcasted_iota(I32, (HD, HD), 0)
    ci = lax.broadcasted_iota(I32, (HD, HD), 1)
    for h in range(H):
        ig = jnp.full((1, HD), gs_ref[b, h] + bif_ref[0, h], F32)
        fp = jnp.full((1, HD), gs_ref[b, H + h] + bif_ref[1, h], F32)
        m = jnp.full((1, HD), ms_ref[b, h], F32)
        lf = -_softplus(-fp)
        inter = lf + m
        m_t = jnp.maximum(inter, ig)
        w_intra = jnp.exp(ig - m_t)
        w_inter = jnp.exp(inter - m_t)
        q = qkv_ref[0, h:h + 1, :]
        k = qkv_ref[0, H + h:H + h + 1, :] * (HD ** -0.5)
        v = qkv_ref[0, 2 * H + h:2 * H + h + 1, :]
        C = c_ref[0, h]
        n = n_ref[0, h:h + 1, :]
        s = jnp.sum(q * k, axis=1, keepdims=True) * w_intra
        qC = _dot(jnp.broadcast_to(q, (8, HD)).astype(BF16), C.astype(BF16))[0:1]
        num = s * v + w_inter * qC
        den = s + w_inter * jnp.sum(q * n, axis=1, keepdims=True)
        hh = num / jnp.maximum(jnp.abs(den), jnp.exp(-m_t))
        k_col = jnp.sum(jnp.where(ri == ci, jnp.broadcast_to(k, (HD, HD)), 0.0), axis=1, keepdims=True)
        co_ref[0, h] = w_inter * C + (k_col * w_intra) * v
        no_ref[0, h:h + 1, :] = w_inter * n + w_intra * k
        mo_out_ref[0, h:h + 1, :] = m_t
        mu = jnp.mean(hh, axis=1, keepdims=True)
        hc = hh - mu
        var = jnp.mean(hc * hc, axis=1, keepdims=True)
        hn = hc * lax.rsqrt(var + LN_EPS) * ng_ref[h:h + 1, :]
        o_ref[0, h:h + 1, :] = _sigmoid(mo_ref[0, h:h + 1, :]) * hn


def mlstm_step(b_if, gates, m_state, qkv, mo, norm_g, C, n):
    DB = qkv.shape[0]
    H = M_HEADS
    smem = pl.BlockSpec(memory_space=pltpu.SMEM)
    row = pl.BlockSpec((1, H, HD), lambda b: (b, 0, 0))
    cspec = pl.BlockSpec((1, H, HD, HD), lambda b: (b, 0, 0, 0))
    return pl.pallas_call(
        _mlstm_step_body,
        grid=(DB,),
        in_specs=[smem, smem, smem, pl.BlockSpec((1, 3 * H, HD), lambda b: (b, 0, 0)), row,
                  pl.BlockSpec((H, HD), lambda b: (0, 0)), cspec, row],
        out_specs=[row, cspec, row, row],
        out_shape=[jax.ShapeDtypeStruct((DB, H, HD), F32), jax.ShapeDtypeStruct((DB, H, HD, HD), F32),
                   jax.ShapeDtypeStruct((DB, H, HD), F32), jax.ShapeDtypeStruct((DB, H, HD), F32)],
        compiler_params=_cp("arbitrary"),
        name="mlstm_step",
    )(b_if, gates, m_state, qkv, mo, norm_g.reshape(H, HD), C, n)


def _rglru_step_body(rx_ref, rgt_ref, buf_ref, h_ref, cw_ref, vec_ref, wa_ref, wx_ref, o_ref, ho_ref):
    cw = cw_ref[...]
    vec = vec_ref[...]
    xc = vec[0:1] + rx_ref[...] * cw[CONV_W - 1:CONV_W]
    for j in range(CONV_W - 1):
        xc = xc + buf_ref[j] * cw[j:j + 1]
    a, u = _rglru_gates(xc, [wa_ref[0].astype(BF16)], [wx_ref[0].astype(BF16)], vec)
    h = a * h_ref[...] + u
    ho_ref[...] = h
    o_ref[...] = h * _gelu(rgt_ref[...])


def rglru_step(rx, rgt, buf_t, h0, conv_w, vec, w_a, w_x):
    DB = rx.shape[0]
    col = pl.BlockSpec((DB, 128), lambda n: (0, n))
    return pl.pallas_call(
        _rglru_step_body,
        grid=(R_BLOCKS,),
        in_specs=[col, col, pl.BlockSpec((CONV_W - 1, DB, 128), lambda n: (0, 0, n)), col,
                  pl.BlockSpec((CONV_W, 128), lambda n: (0, n)),
                  pl.BlockSpec((4, 128), lambda n: (0, n)),
                  pl.BlockSpec((1, 128, 128), lambda n: (n, 0, 0)),
                  pl.BlockSpec((1, 128, 128), lambda n: (n, 0, 0))],
        out_specs=[col, col],
        out_shape=[jax.ShapeDtypeStruct((DB, R_WIDTH), F32), jax.ShapeDtypeStruct((DB, R_WIDTH), F32)],
        compiler_params=_cp("arbitrary"),
        name="rglru_step",
    )(rx, rgt, buf_t, h0, conv_w, vec, w_a, w_x)


def pack_w_in(w):
    wt = jnp.swapaxes(w, 1, 2)
    pieces = [(8744, 14888), (0, 1024), (2584, 5656), (5672, 6696), (6696, 7720), (7720, 8744),
              (1024, 2560), (2560, 2584), (5656, 5672)]
    n_src = sum(b - a for a, b in pieces)
    parts = [wt[:, a:b] for a, b in pieces] + [jnp.zeros((w.shape[0], N_AL - n_src, w.shape[1]), w.dtype)]
    return jnp.concatenate(parts, axis=1).astype(BF16)


def prompt_layer(xp, lw, gw, layer, kv_buf):
    B, T, D = xp.shape
    x2 = xp.reshape(B * T, D)
    M = B * T
    z = mm_nt(x2, gw["w_in"], layer, tm=min(IN_TM, M), tn=IN_TN, out_dtype=F32)
    z3 = z.reshape(B, T, N_AL)
    kv_buf = kv_rows(z, kv_buf, layer)
    kc, vc, cb = compress_from_z(z3, lw["phi_w1"], lw["pe"], lw["phi_b1"], lw["phi_w2"])
    o_a = nsa_prompt(z3, kc, vc, lw["slopes"])
    o_b, Cp, np_, mp = mlstm_prompt(z3, lw["b_if"], lw["norm_g"])
    o_c, hp = rglru_prompt(z3, lw["conv_w"], lw["rg_vec"], lw["w_a"], lw["w_x"])
    merged = merge(o_a.reshape(B * T, -1), o_b.reshape(B * T, -1), o_c.reshape(B * T, -1), z,
                   gw["w_branch"], layer, tm=min(MERGE_TM, M), tn=MERGE_TN, out_dtype=BF16)
    h = mm_res_ln(merged, gw["w_out"], layer, x2, lw["ln_g"][0:1], lw["ln_b"][0:1], tm=min(OUT_TM, M), tk=D)
    f1 = mm(h, gw["mlp_w1"], layer, tm=min(UP_TM, M), tn=UP_TN, out_dtype=BF16, act="relu2")
    x_new = mm_res_ln(f1, gw["mlp_w2"], layer, h, lw["ln_g"][1:2], lw["ln_b"][1:2], tm=min(DOWN_TM, M),
                      tk=DOWN_TK)
    n_win = min(WINDOW, T)
    wkv = z3[:, T - n_win:, OFF_KV + 4 * NSA_KV * HD:OFF_KV + 6 * NSA_KV * HD]
    states = (wkv.reshape(B, n_win, 2, NSA_KV, HD), Cp, np_[:, :, 0], mp[:, :, 0, 0], hp[:, 0],
              z3[:, T - (CONV_W - 1):, OFF_RX:OFF_RX + R_WIDTH])
    return x_new.reshape(B, T, D), states, cb, kv_buf


def sample_layer(xs, lw, gw, cb, layer, page_table, pool, n_pool, win, win_l, C0, n0, m0, h0, conv0):
    DB, D = xs.shape
    z = mm_nt(xs, gw["w_in"], layer, tm=DB, tn=IN_TN, out_dtype=F32)
    q8 = z[:, OFF_Q:OFF_Q + NSA_WIDTH].reshape(DB, NSA_HEADS, HD)
    newkv = z[:, OFF_KV:OFF_KV + 6 * NSA_KV * HD].reshape(DB, 6 * NSA_KV, HD)
    sm = z[:, OFF_SM:OFF_SM + 128].reshape(DB, 1, 128)
    oc, idx = sample_cmp_select(page_table, pool, layer, n_pool, lw["w1c"], cb, lw["phi_w2"], q8,
                                lw["slopes"])
    osw = sample_sel_win(idx, page_table, pool, win, layer, n_pool, q8, newkv, oc, sm, lw["slopes"])
    o_a = jnp.concatenate([osw[:, g, g * NSA_GROUP:(g + 1) * NSA_GROUP] for g in range(NSA_KV)],
                          axis=1).reshape(DB, NSA_WIDTH)
    qkv = z[:, OFF_MQKV:OFF_MQKV + 3 * M_WIDTH].reshape(DB, 3 * M_HEADS, HD)
    mo = z[:, OFF_MO:OFF_MO + M_WIDTH].reshape(DB, M_HEADS, HD)
    gates = z[:, OFF_SM + SM_I:OFF_SM + SM_I + 2 * M_HEADS]
    o_b, Cs, ns, ms = mlstm_step(lw["b_if"], gates, m0, qkv, mo, lw["norm_g"], C0, n0)
    rx = z[:, OFF_RX:OFF_RX + R_WIDTH]
    o_c, hs = rglru_step(rx, z[:, OFF_RGT:OFF_RGT + R_WIDTH], conv0.transpose(1, 0, 2), h0,
                         lw["conv_w"], lw["rg_vec"], lw["w_a"], lw["w_x"])
    merged = merge(o_a, o_b.reshape(DB, M_WIDTH), o_c, z, gw["w_branch"], layer, tm=DB, tn=MERGE_TN,
                   out_dtype=F32)
    h = mm_res_ln(merged, gw["w_out"], layer, xs, lw["ln_g"][0:1], lw["ln_b"][0:1], tm=DB, tk=D)
    f1 = mm(h, gw["mlp_w1"], layer, tm=DB, tn=UP_TN, out_dtype=F32, act="relu2")
    x_new = mm_res_ln(f1, gw["mlp_w2"], layer, h, lw["ln_g"][1:2], lw["ln_b"][1:2], tm=DB, tk=DOWN_TK)
    kvn = newkv.reshape(DB, 1, 6, NSA_KV, HD)
    Wb = win_l.shape[1]
    keep = min(WINDOW, Wb + 1)
    win_new = jnp.concatenate([win_l, kvn[:, :, 4:6]], axis=1)[:, Wb + 1 - keep:]
    conv_new = jnp.concatenate([conv0[:, 1:], rx[:, None]], axis=1)
    states = (kvn[:, :, :4], win_new, Cs, ns, ms[:, :, 0], hs, conv_new)
    return x_new, states


def kernel(x_prompt, x_sample, cache_nsa_kv, cache_win_kv, state_mlstm_C, state_mlstm_n, state_mlstm_m,
           state_rglru_h, state_rglru_conv, page_table, w_in, nsa_pe, nsa_phi_w1, nsa_phi_b1, nsa_phi_w2,
           mlstm_b_if, mlstm_norm_g, rg_conv_w, rg_conv_b, rg_w_a, rg_b_a, rg_w_x, rg_b_x, rg_lambda,
           w_branch, w_out, ln_g, ln_b, mlp_w1, mlp_w2):
    DB, Tn, D = x_sample.shape
    assert Tn == 1 and D == D_MODEL
    depth, n_pool = cache_nsa_kv.shape[:2]
    pool = cache_nsa_kv.reshape(depth * n_pool, ROWS_PER_PAGE, HD)
    Wb = cache_win_kv.shape[2]
    win = cache_win_kv.reshape(depth * DB, Wb * 2 * NSA_KV, HD)
    slopes = alibi_slopes()
    half = CMP_STRIDE * HD
    B, T = x_prompt.shape[:2]
    xp = x_prompt
    xs = x_sample.reshape(DB, D)
    gw = {"w_in": pack_w_in(w_in), "w_branch": w_branch.astype(BF16), "w_out": w_out.astype(BF16),
          "mlp_w1": mlp_w1.astype(BF16), "mlp_w2": mlp_w2.astype(BF16)}
    new_p = [[] for _ in range(6)]
    new_s = [[] for _ in range(7)]
    kv_buf = jnp.zeros((depth * B * T * 4 * NSA_KV, HD), F32)
    for l in range(depth):
        lw = {
            "pe": nsa_pe[l], "phi_w1": nsa_phi_w1[l], "phi_b1": nsa_phi_b1[l], "phi_w2": nsa_phi_w2[l],
            "w1c": jnp.concatenate([nsa_phi_w1[l][:, :half], nsa_phi_w1[l][:, half:]], axis=2).astype(BF16),
            "slopes": slopes, "b_if": mlstm_b_if[l], "norm_g": mlstm_norm_g[l],
            "conv_w": rg_conv_w[l],
            "rg_vec": jnp.stack([rg_conv_b[l], rg_b_a[l], rg_b_x[l], rg_lambda[l]]),
            "w_a": rg_w_a[l], "w_x": rg_w_x[l],
            "ln_g": ln_g[l], "ln_b": ln_b[l],
        }
        xp, st_p, cb, kv_buf = prompt_layer(xp, lw, gw, l, kv_buf)
        xs, st_s = sample_layer(xs, lw, gw, cb, l, page_table, pool, n_pool, win, cache_win_kv[l],
                                state_mlstm_C[l], state_mlstm_n[l], state_mlstm_m[l],
                                state_rglru_h[l], state_rglru_conv[l])
        for lst, val in zip(new_p, st_p):
            lst.append(val)
        for lst, val in zip(new_s, st_s):
            lst.append(val)
    P = [jnp.stack(a) for a in new_p]
    S = [jnp.stack(a) for a in new_s]
    nsa_kv_p = kv_buf.reshape(depth, B, T, 4, NSA_KV, HD)
    return (xp, xs.reshape(DB, Tn, D), nsa_kv_p, S[0], P[0], S[1], P[1], S[2], P[2], S[3], P[3], S[4],
            P[4], S[5], P[5], S[6])
```
